```python
import math
import jax, jax.numpy as jnp
from jax import lax
import numpy as np

D_MODEL = 2048
BATCH = 4
SEQ = 4096
DEPTH = 2

N_EVEN = (DEPTH + 1) // 2
N_ODD = DEPTH // 2
CHUNK = 64
EPS = 1e-6

GLA_HEADS = 4
GLA_WIDTH = D_MODEL // 2
GLA_DV = GLA_WIDTH // GLA_HEADS
GLA_DK = GLA_DV // 2
GLA_KW = GLA_HEADS * GLA_DK
GLA_LOWRANK = 16
GLA_TAU = 16.0
GLA_SPLITS = (GLA_KW, GLA_KW, GLA_WIDTH, GLA_WIDTH, GLA_LOWRANK, GLA_LOWRANK)
GLA_COLS = sum(GLA_SPLITS)

RWKV_HEAD = 64
RWKV_WIDTH = D_MODEL // 2
RWKV_HEADS = RWKV_WIDTH // RWKV_HEAD
DECAY_LORA = 96
ICLR_LORA = 96
GATE_LORA = 256
RWKV_LN_EPS = 64e-5
RWKV_SPLITS = (RWKV_WIDTH, RWKV_WIDTH, RWKV_WIDTH, DECAY_LORA, DECAY_LORA, ICLR_LORA, GATE_LORA)
RWKV_COLS = sum(RWKV_SPLITS)
EVEN_COLS = GLA_COLS + RWKV_COLS

GDN_DK = 128
GDN_DV = 128
GDN_KHEADS = D_MODEL // 128
GDN_VHEADS = 2 * GDN_KHEADS
GDN_KW = GDN_KHEADS * GDN_DK
GDN_VW = GDN_VHEADS * GDN_DV
GDN_QKV = 2 * GDN_KW + GDN_VW
GDN_CONV = 5
ODD_SPLITS = (GDN_QKV, GDN_VW, GDN_VHEADS, GDN_VHEADS, GDN_VHEADS, GDN_VHEADS)
ODD_COLS = sum(ODD_SPLITS)

N_GROUPS = 4
EXPERTS_PER_GROUP = 8
N_EXPERTS = N_GROUPS * EXPERTS_PER_GROUP
EXPERT_FF = 512
TOP_K = 2

kernel_name = 'bidir_hybrid_gla_rwkv7_gdn_hmoe'


def rms_norm(x, g):
    xf = x.astype(jnp.float32)
    y = xf * lax.rsqrt(jnp.mean(xf * xf, axis=-1, keepdims=True) + EPS)
    return (y * g).astype(x.dtype)


def l2_normalize(t):
    return t * lax.rsqrt(jnp.sum(t * t, axis=-1, keepdims=True) + EPS)


def split_cols(z, sizes):
    idx = np.cumsum(np.array(sizes))[:-1].tolist()
    return jnp.split(z, idx, axis=-1)


def to_heads(z, n_heads):
    b, s, c = z.shape
    return z.reshape(b, s, n_heads, c // n_heads).transpose(0, 2, 1, 3)


def merge_heads(z):
    b, h, s, d = z.shape
    return z.transpose(0, 2, 1, 3).reshape(b, s, h * d)


def flip_seq(t):
    return jnp.flip(t, axis=2)


def centred_shift(z, mu):
    prev = jnp.pad(z, ((0, 0), (1, 0), (0, 0)))[:, :-1]
    nxt = jnp.pad(z, ((0, 0), (0, 1), (0, 0)))[:, 1:]
    return z + mu[0] * (prev - z) + mu[1] * (nxt - z)


def centred_depthwise_conv(z, w):
    width, c = w.shape
    pad = width // 2
    return lax.conv_general_dilated(z, w[:, None, :].astype(z.dtype), window_strides=(1,),
                                    padding=[(pad, pad)], dimension_numbers=('NWC', 'WIO', 'NWC'),
                                    feature_group_count=c)


def gla_chunked(q, k, v, log_a):
    b, h, s, dk = q.shape
    dv = v.shape[-1]
    n = s // CHUNK
    q = q.reshape(b, h, n, CHUNK, dk)
    k = k.reshape(b, h, n, CHUNK, dk)
    v = v.reshape(b, h, n, CHUNK, dv)
    cum = jnp.cumsum(log_a.reshape(b, h, n, CHUNK, dk), axis=3)
    cum_last = cum[:, :, :, -1:, :]
    q_dec = q * jnp.exp(cum)
    scores = jnp.einsum('bhnik,bhnjk->bhnij', q_dec, k * jnp.exp(-cum))
    scores = jnp.where(jnp.tril(jnp.ones((CHUNK, CHUNK), bool)), scores, 0.0)
    o_intra = jnp.einsum('bhnij,bhnjv->bhniv', scores, v)
    k_end = k * jnp.exp(cum_last - cum)
    dec = jnp.exp(cum_last[:, :, :, 0, :])

    def step(state, inp):
        qd_n, ke_n, v_n, dec_n = inp
        o_n = qd_n @ state
        state = dec_n[..., None] * state + jnp.swapaxes(ke_n, -1, -2) @ v_n
        return state, o_n

    tm = lambda t: jnp.moveaxis(t, 2, 0)
    _, o_inter = lax.scan(step, jnp.zeros((b, h, dk, dv), q.dtype), (tm(q_dec), tm(k_end), tm(v), tm(dec)))
    return (o_intra + jnp.moveaxis(o_inter, 0, 2)).reshape(b, h, s, dv)


def gla_mixer(q, k, v, gate, alpha_lr, w_alpha, b_alpha, norm_w):
    qh = to_heads(q, GLA_HEADS) * (GLA_DK ** -0.5)
    kh = to_heads(k, GLA_HEADS)
    vh = to_heads(v, GLA_HEADS)
    log_a = jax.nn.log_sigmoid(jnp.einsum('bsnr,nrk->bsnk', alpha_lr, w_alpha) + b_alpha) / GLA_TAU
    la_f = to_heads(log_a[:, :, 0], GLA_HEADS)
    la_b = to_heads(log_a[:, :, 1], GLA_HEADS)
    o = gla_chunked(qh, kh, vh, la_f) + flip_seq(
        gla_chunked(flip_seq(qh), flip_seq(kh), flip_seq(vh), flip_seq(la_b)))
    o = rms_norm(o, norm_w)
    return merge_heads(o) * jax.nn.silu(gate)


def rwkv7_scan(r, decay, k, v, kk, a, reverse):
    b, s, h, n = r.shape

    def step(state, inp):
        r_t, w_t, k_t, v_t, kk_t, a_t = inp
        s_kk = jnp.einsum('bhvk,bhk->bhv', state, kk_t)
        state = (state * w_t[:, :, None, :] - s_kk[..., None] * (kk_t * a_t)[:, :, None, :]
                 + v_t[..., None] * k_t[:, :, None, :])
        return state, jnp.einsum('bhvk,bhk->bhv', state, r_t)

    tm = lambda t: jnp.moveaxis(t, 1, 0)
    _, out = lax.scan(step, jnp.zeros((b, h, n, n), r.dtype),
                      (tm(r), tm(decay), tm(k), tm(v), tm(kk), tm(a)), reverse=reverse)
    return jnp.moveaxis(out, 0, 1)


def rwkv7_mixer(feats, mu, w0, w2, a0, a2, g2, k_k, k_a, r_k, ln_w, ln_b):
    z = centred_shift(feats, mu)
    r, k, v, w_lr_f, w_lr_b, a_lr, g_lr = split_cols(z, RWKV_SPLITS)
    b, s, c = r.shape
    hn = lambda t: t.reshape(b, s, RWKV_HEADS, RWKV_HEAD)
    w_raw = w0 + jnp.einsum('bsnr,nrc->bsnc', jnp.tanh(jnp.stack([w_lr_f, w_lr_b], 2)), w2)
    decay = jnp.exp(-jnp.exp(-jax.nn.softplus(-w_raw) - 0.5))
    a = jax.nn.sigmoid(a0 + a_lr @ a2)
    g = jax.nn.sigmoid(g_lr) @ g2
    r, k, v, a = hn(r), hn(k), hn(v), hn(a)
    kk = l2_normalize(k * k_k.reshape(RWKV_HEADS, RWKV_HEAD))
    k = k * (1.0 + (a - 1.0) * k_a.reshape(RWKV_HEADS, RWKV_HEAD))
    o = (rwkv7_scan(r, hn(decay[:, :, 0]), k, v, kk, a, False)
         + rwkv7_scan(r, hn(decay[:, :, 1]), k, v, kk, a, True))
    mean = jnp.mean(o, axis=-1, keepdims=True)
    var = jnp.mean(jnp.square(o - mean), axis=-1, keepdims=True)
    o = ((o - mean) * lax.rsqrt(var + RWKV_LN_EPS)).reshape(b, s, c) * ln_w + ln_b
    bonus = (jnp.sum(r * k * r_k, axis=-1, keepdims=True) * v).reshape(b, s, c)
    return (o + bonus) * g


def gla_rwkv_layer(h, w_in, gla_w_alpha, gla_b_alpha, gla_norm, mu, w0, w2, a0, a2, g2,
                   k_k, k_a, r_k, ln_w, ln_b, w_out):
    z = (h @ w_in).astype(jnp.float32)
    q, k, v, gate, al_f, al_b = split_cols(z[..., :GLA_COLS], GLA_SPLITS)
    y_gla = gla_mixer(q, k, v, gate, jnp.stack([al_f, al_b], 2), gla_w_alpha, gla_b_alpha, gla_norm)
    y_rwkv = rwkv7_mixer(z[..., GLA_COLS:], mu, w0, w2, a0, a2, g2, k_k, k_a, r_k, ln_w, ln_b)
    return jnp.concatenate([y_gla, y_rwkv], axis=-1) @ w_out


def gated_delta_chunked(q, k, v, g, beta):
    b, h, s, dk = q.shape
    dv = v.shape[-1]
    n = s // CHUNK
    q = q.reshape(b, h, n, CHUNK, dk)
    k = k.reshape(b, h, n, CHUNK, dk)
    v = v.reshape(b, h, n, CHUNK, dv)
    beta = beta.reshape(b, h, n, CHUNK)
    gam = jnp.cumsum(g.reshape(b, h, n, CHUNK), axis=-1)
    incl = jnp.tril(jnp.ones((CHUNK, CHUNK), bool))
    strict = jnp.tril(jnp.ones((CHUNK, CHUNK), bool), -1)
    decay = jnp.exp(jnp.where(incl, gam[..., :, None] - gam[..., None, :], -jnp.inf))
    k_beta = k * beta[..., None]
    a_mat = jnp.where(strict, jnp.einsum('bhnik,bhnjk->bhnij', k_beta, k) * decay, 0.0)
    eye = jnp.eye(CHUNK, dtype=q.dtype)
    t_inv = lax.linalg.triangular_solve(a_mat + eye, jnp.broadcast_to(eye, a_mat.shape),
                                        left_side=True, lower=True, unit_diagonal=True)
    u = t_inv @ (v * beta[..., None])
    w = t_inv @ (k_beta * jnp.exp(gam)[..., None])
    qk = jnp.einsum('bhnik,bhnjk->bhnij', q, k) * decay
    q_dec = q * jnp.exp(gam)[..., None]
    k_end = k * jnp.exp(gam[..., -1:] - gam)[..., None]
    dec_last = jnp.exp(gam[..., -1])

    def step(state, inp):
        u_n, w_n, qk_n, qd_n, ke_n, dl_n = inp
        v_new = u_n - w_n @ state
        o_n = qd_n @ state + qk_n @ v_new
        state = state * dl_n[..., None, None] + jnp.swapaxes(ke_n, -1, -2) @ v_new
        return state, o_n

    tm = lambda t: jnp.moveaxis(t, 2, 0)
    _, o = lax.scan(step, jnp.zeros((b, h, dk, dv), q.dtype),
                    (tm(u), tm(w), tm(qk), tm(q_dec), tm(k_end), tm(dec_last)))
    return jnp.moveaxis(o, 0, 2).reshape(b, h, s, dv)


def gated_deltanet_layer(h, w_in, conv_w, a_log, dt_bias, norm_w, w_out):
    z = (h @ w_in).astype(jnp.float32)
    qkv, gate, b_f, b_b, a_f, a_b = split_cols(z, ODD_SPLITS)
    qkv = jax.nn.silu(centred_depthwise_conv(qkv, conv_w))
    q, k, v = split_cols(qkv, (GDN_KW, GDN_KW, GDN_VW))
    b, s, _ = q.shape
    rep = GDN_VHEADS // GDN_KHEADS
    q = l2_normalize(q.reshape(b, s, GDN_KHEADS, GDN_DK)) * (GDN_DK ** -0.5)
    k = l2_normalize(k.reshape(b, s, GDN_KHEADS, GDN_DK))
    q = jnp.repeat(q, rep, axis=2).transpose(0, 2, 1, 3)
    k = jnp.repeat(k, rep, axis=2).transpose(0, 2, 1, 3)
    v = to_heads(v, GDN_VHEADS)
    beta = jax.nn.sigmoid(jnp.stack([b_f, b_b], 0)).transpose(0, 1, 3, 2)
    g = (-jnp.exp(a_log)[:, None, None, :]
         * jax.nn.softplus(jnp.stack([a_f, a_b], 0) + dt_bias[:, None, None, :])).transpose(0, 1, 3, 2)
    o = gated_delta_chunked(q, k, v, g[0], beta[0]) + flip_seq(
        gated_delta_chunked(flip_seq(q), flip_seq(k), flip_seq(v), flip_seq(g[1]), flip_seq(beta[1])))
    o = rms_norm(o, norm_w)
    return (merge_heads(o) * jax.nn.silu(gate)) @ w_out


def hier_moe(h, w_group, b_group, w_router, b_router, w_gate, w_up, w_down):
    b, s, d = h.shape
    t = h.reshape(b * s, d)
    grp_logits = (t @ w_group).astype(jnp.float32) + b_group
    grp_prob = jax.nn.softmax(grp_logits, axis=-1)
    grp_onehot = jax.nn.one_hot(jnp.argmax(grp_logits, axis=-1), N_GROUPS, dtype=jnp.float32)
    grp_w = jnp.sum(grp_prob * grp_onehot, axis=-1, keepdims=True)
    exp_logits = ((t @ w_router).astype(jnp.float32) + b_router).reshape(-1, N_GROUPS, EXPERTS_PER_GROUP)
    sel_logits = jnp.einsum('tge,tg->te', exp_logits, grp_onehot)
    top_v, top_i = lax.top_k(sel_logits, TOP_K)
    top_w = jax.nn.softmax(top_v, axis=-1) * grp_w
    within = jnp.sum(jax.nn.one_hot(top_i, EXPERTS_PER_GROUP, dtype=jnp.float32) * top_w[..., None], axis=1)
    gates = (grp_onehot[:, :, None] * within[:, None, :]).reshape(-1, N_EXPERTS)
    y = jnp.zeros((b * s, d), jnp.float32)
    for e in range(N_EXPERTS):
        hid = jax.nn.silu(t @ w_gate[e]) * (t @ w_up[e])
        y = y + gates[:, e:e + 1] * (hid @ w_down[e])
    return y.reshape(b, s, d)


def setup_inputs(seed: int = 0) -> dict:
    key = jax.random.key(seed)
    ks = iter(jax.random.split(key, 48))
    f32 = jnp.float32
    D = D_MODEL
    NE, NO = N_EVEN, N_ODD

    def nrm(shape, scale):
        return jax.random.normal(next(ks), shape, f32) * scale

    def unif(shape, lo, hi):
        return jax.random.uniform(next(ks), shape, f32, lo, hi)

    x = nrm((BATCH, SEQ, D), 1.0)
    norm_mix = 1.0 + nrm((DEPTH, D), 0.02)
    norm_ffn = 1.0 + nrm((DEPTH, D), 0.02)
    norm_final = 1.0 + nrm((D,), 0.02)
    ev_w_in = nrm((NE, D, EVEN_COLS), D ** -0.5)
    ev_gla_w_alpha = nrm((NE, 2, GLA_LOWRANK, GLA_KW), GLA_LOWRANK ** -0.5)
    ev_gla_b_alpha = nrm((NE, 2, GLA_KW), 0.5)
    ev_gla_norm = 1.0 + nrm((NE, GLA_DV), 0.02)
    ev_rwkv_mu = unif((NE, 2, RWKV_COLS), 0.0, 0.5)
    ev_rwkv_w0 = nrm((NE, 2, RWKV_WIDTH), 0.5)
    ev_rwkv_w2 = nrm((NE, 2, DECAY_LORA, RWKV_WIDTH), DECAY_LORA ** -0.5)
    ev_rwkv_a0 = nrm((NE, RWKV_WIDTH), 0.1)
    ev_rwkv_a2 = nrm((NE, ICLR_LORA, RWKV_WIDTH), ICLR_LORA ** -0.5)
    ev_rwkv_g2 = nrm((NE, GATE_LORA, RWKV_WIDTH), GATE_LORA ** -0.5)
    ev_rwkv_k_k = 0.85 + nrm((NE, RWKV_WIDTH), 0.05)
    ev_rwkv_k_a = 1.0 + nrm((NE, RWKV_WIDTH), 0.05)
    ev_rwkv_r_k = nrm((NE, RWKV_HEADS, RWKV_HEAD), 0.1)
    ev_rwkv_ln_w = 1.0 + nrm((NE, RWKV_WIDTH), 0.02)
    ev_rwkv_ln_b = nrm((NE, RWKV_WIDTH), 0.02)
    ev_w_out = nrm((NE, GLA_WIDTH + RWKV_WIDTH, D), (GLA_WIDTH + RWKV_WIDTH) ** -0.5)
    od_w_in = nrm((NO, D, ODD_COLS), D ** -0.5)
    od_conv = nrm((NO, GDN_CONV, GDN_QKV), GDN_CONV ** -0.5)
    od_a_log = jnp.log(unif((NO, 2, GDN_VHEADS), 1.0, 16.0))
    dt = jnp.exp(unif((NO, 2, GDN_VHEADS), math.log(1e-3), math.log(1e-1)))
    od_dt_bias = dt + jnp.log(-jnp.expm1(-dt))
    od_norm = 1.0 + nrm((NO, GDN_DV), 0.02)
    od_w_out = nrm((NO, GDN_VW, D), GDN_VW ** -0.5)
    moe_w_group = nrm((DEPTH, D, N_GROUPS), D ** -0.5)
    moe_b_group = nrm((DEPTH, N_GROUPS), 0.01)
    moe_w_router = nrm((DEPTH, D, N_EXPERTS), D ** -0.5)
    moe_b_router = nrm((DEPTH, N_EXPERTS), 0.01)
    moe_w_gate = nrm((DEPTH, N_EXPERTS, D, EXPERT_FF), D ** -0.5)
    moe_w_up = nrm((DEPTH, N_EXPERTS, D, EXPERT_FF), D ** -0.5)
    moe_w_down = nrm((DEPTH, N_EXPERTS, EXPERT_FF, D), EXPERT_FF ** -0.5)
    return {'x': x, 'norm_mix': norm_mix, 'norm_ffn': norm_ffn, 'norm_final': norm_final,
            'ev_w_in': ev_w_in, 'ev_gla_w_alpha': ev_gla_w_alpha, 'ev_gla_b_alpha': ev_gla_b_alpha,
            'ev_gla_norm': ev_gla_norm, 'ev_rwkv_mu': ev_rwkv_mu, 'ev_rwkv_w0': ev_rwkv_w0,
            'ev_rwkv_w2': ev_rwkv_w2, 'ev_rwkv_a0': ev_rwkv_a0, 'ev_rwkv_a2': ev_rwkv_a2,
            'ev_rwkv_g2': ev_rwkv_g2, 'ev_rwkv_k_k': ev_rwkv_k_k, 'ev_rwkv_k_a': ev_rwkv_k_a,
            'ev_rwkv_r_k': ev_rwkv_r_k, 'ev_rwkv_ln_w': ev_rwkv_ln_w, 'ev_rwkv_ln_b': ev_rwkv_ln_b,
            'ev_w_out': ev_w_out, 'od_w_in': od_w_in, 'od_conv': od_conv, 'od_a_log': od_a_log,
            'od_dt_bias': od_dt_bias, 'od_norm': od_norm, 'od_w_out': od_w_out,
            'moe_w_group': moe_w_group, 'moe_b_group': moe_b_group, 'moe_w_router': moe_w_router,
            'moe_b_router': moe_b_router, 'moe_w_gate': moe_w_gate, 'moe_w_up': moe_w_up,
            'moe_w_down': moe_w_down}


def reference(x, norm_mix, norm_ffn, norm_final,
              ev_w_in, ev_gla_w_alpha, ev_gla_b_alpha, ev_gla_norm,
              ev_rwkv_mu, ev_rwkv_w0, ev_rwkv_w2, ev_rwkv_a0, ev_rwkv_a2, ev_rwkv_g2,
              ev_rwkv_k_k, ev_rwkv_k_a, ev_rwkv_r_k, ev_rwkv_ln_w, ev_rwkv_ln_b, ev_w_out,
              od_w_in, od_conv, od_a_log, od_dt_bias, od_norm, od_w_out,
              moe_w_group, moe_b_group, moe_w_router, moe_b_router, moe_w_gate, moe_w_up, moe_w_down):
    for i in range(DEPTH):
        j = i // 2
        h = rms_norm(x, norm_mix[i])
        if i % 2 == 0:
            y = gla_rwkv_layer(h, ev_w_in[j], ev_gla_w_alpha[j], ev_gla_b_alpha[j], ev_gla_norm[j],
                               ev_rwkv_mu[j], ev_rwkv_w0[j], ev_rwkv_w2[j], ev_rwkv_a0[j], ev_rwkv_a2[j],
                               ev_rwkv_g2[j], ev_rwkv_k_k[j], ev_rwkv_k_a[j], ev_rwkv_r_k[j],
                               ev_rwkv_ln_w[j], ev_rwkv_ln_b[j], ev_w_out[j])
        else:
            y = gated_deltanet_layer(h, od_w_in[j], od_conv[j], od_a_log[j], od_dt_bias[j],
                                     od_norm[j], od_w_out[j])
        x = x + y.astype(x.dtype)
        f = hier_moe(rms_norm(x, norm_ffn[i]), moe_w_group[i], moe_b_group[i], moe_w_router[i],
                     moe_b_router[i], moe_w_gate[i], moe_w_up[i], moe_w_down[i])
        x = x + f.astype(x.dtype)
    return rms_norm(x, norm_final)
```

```python
import functools
import math

import jax
import jax.numpy as jnp
from jax import lax
from jax.experimental import pallas as pl
from jax.experimental.pallas import tpu as pltpu

F32 = jnp.float32
BF16 = jnp.bfloat16
HI = lax.Precision.HIGHEST

EPS = 1e-6
CHUNK = 64
LANES = 128
VMEM_LIMIT = 56 * 1024 * 1024

D_MODEL = 2048
GLA_HEADS = 4
GLA_DK = 128
GLA_DV = 256
GLA_KW = 512
GLA_WIDTH = 1024
GLA_LOWRANK = 16
GLA_TAU = 16.0
RWKV_HEAD = 64
RWKV_WIDTH = 1024
RWKV_HEADS = 16
DECAY_LORA = 96
ICLR_LORA = 96
GATE_LORA = 256
RWKV_LN_EPS = 64e-5
GDN_DK = 128
GDN_DV = 128
GDN_KHEADS = 16
GDN_VHEADS = 32
GDN_KW = 2048
GDN_VW = 4096
GDN_QKV = 8192
N_GROUPS = 4
EXPERTS_PER_GROUP = 8
N_EXPERTS = 32
EXPERT_FF = 512

R_OFF, K_OFF, V_OFF = 0, 1024, 2048
GLR_OFF = 3072
WLF_OFF, WLB_OFF, ALR_OFF, AL_OFF = 3328, 3456, 3584, 3712
RWKV_BLOCK = 3840
GQ_OFF, GK_OFF, GV_OFF, GG_OFF = 3840, 4352, 4864, 5888
EVEN_PAD = 6912


def _cparams(sem):
    return pltpu.CompilerParams(dimension_semantics=sem, vmem_limit_bytes=VMEM_LIMIT)


def _dot(a, b, precision=None):
    return jnp.dot(a, b, preferred_element_type=F32, precision=precision)


def _dot_nt(a, b, precision=None):
    return lax.dot_general(a, b, (((1,), (1,)), ((), ())), preferred_element_type=F32, precision=precision)


def _dot_tn(a, b, precision=None):
    return lax.dot_general(a, b, (((0,), (0,)), ((), ())), preferred_element_type=F32, precision=precision)


def _bdot(a, b):
    return _dot(a.astype(BF16), b.astype(BF16))


def _bdot_nt(a, b):
    return _dot_nt(a.astype(BF16), b.astype(BF16))


def _bdot_tn(a, b):
    return _dot_tn(a.astype(BF16), b.astype(BF16))


def _sigmoid(x):
    return 1.0 / (1.0 + jnp.exp(-x))


def _silu(x):
    return x * _sigmoid(x)


def _softplus(x):
    return jnp.maximum(x, 0.0) + jnp.log(1.0 + jnp.exp(-jnp.abs(x)))


def _iota2(shape, dim):
    return lax.broadcasted_iota(jnp.int32, shape, dim)


def _order_mask(n, rev, strict, block):
    i = _iota2((n, n), 0)
    j = _iota2((n, n), 1)
    if rev:
        m = (j > i) if strict else (j >= i)
    else:
        m = (j < i) if strict else (j <= i)
    if block < n:
        sh = block.bit_length() - 1
        m = m & (lax.shift_right_logical(i, sh) == lax.shift_right_logical(j, sh))
    return m


def _neumann_inverse(nmat):
    n = nmat.shape[0]
    eye = (_iota2((n, n), 0) == _iota2((n, n), 1)).astype(F32)
    t = eye + nmat
    nb = nmat.astype(BF16)
    p = _dot(nb, nb)
    for step in range(5):
        pb = p.astype(BF16)
        if step < 4:
            both = _dot(pb, jnp.concatenate([pb, t.astype(BF16)], axis=1))
            p = both[:, :n]
            t = t + both[:, n:]
        else:
            t = t + _dot(pb, t.astype(BF16))
    return t


def _norm_matmul_kernel(x_ref, g_ref, w_ref, o_ref, h_ref):
    @pl.when(pl.program_id(1) == 0)
    def _():
        x = x_ref[...]
        ms = jnp.mean(x * x, axis=-1, keepdims=True)
        h_ref[...] = (x * lax.rsqrt(ms + EPS) * g_ref[...]).astype(BF16)

    o_ref[...] = _dot(h_ref[...], w_ref[...])


def norm_matmul(x, g, w, tm, tn):
    m, d = x.shape
    tm = min(tm, m)
    n = w.shape[1]
    return pl.pallas_call(
        _norm_matmul_kernel,
        grid=(m // tm, n // tn),
        in_specs=[pl.BlockSpec((tm, d), lambda i, j: (i, 0)),
                  pl.BlockSpec((1, d), lambda i, j: (0, 0)),
                  pl.BlockSpec((d, tn), lambda i, j: (0, j))],
        out_specs=pl.BlockSpec((tm, tn), lambda i, j: (i, j)),
        out_shape=jax.ShapeDtypeStruct((m, n), F32),
        scratch_shapes=[pltpu.VMEM((tm, d), BF16)],
        compiler_params=_cparams(("parallel", "arbitrary")),
        name="norm_matmul",
    )(x, g.reshape(1, d), w)


def _proj_residual_kernel(*refs, n_lhs):
    x_ref = refs[2 * n_lhs]
    o_ref = refs[2 * n_lhs + 1]
    acc = x_ref[...]
    for t in range(n_lhs):
        acc = acc + _dot(refs[t][...], refs[n_lhs + t][...])
    o_ref[...] = acc


def proj_residual(ys, ws, x, tm, tn):
    m, n = x.shape
    tm = min(tm, m)
    n_lhs = len(ys)
    in_specs = [pl.BlockSpec((tm, y.shape[1]), lambda i, j: (i, 0)) for y in ys]
    in_specs += [pl.BlockSpec((w.shape[0], tn), lambda i, j: (0, j)) for w in ws]
    in_specs += [pl.BlockSpec((tm, tn), lambda i, j: (i, j))]
    return pl.pallas_call(
        functools.partial(_proj_residual_kernel, n_lhs=n_lhs),
        grid=(m // tm, n // tn),
        in_specs=in_specs,
        out_specs=pl.BlockSpec((tm, tn), lambda i, j: (i, j)),
        out_shape=jax.ShapeDtypeStruct((m, n), F32),
        compiler_params=_cparams(("parallel", "arbitrary")),
        name="proj_residual",
    )(*ys, *ws, x)


def _final_norm_kernel(x_ref, g_ref, o_ref):
    x = x_ref[...]
    ms = jnp.mean(x * x, axis=-1, keepdims=True)
    o_ref[...] = x * lax.rsqrt(ms + EPS) * g_ref[...]


def final_norm(x, g, tm):
    m, d = x.shape
    tm = min(tm, m)
    return pl.pallas_call(
        _final_norm_kernel,
        grid=(m // tm,),
        in_specs=[pl.BlockSpec((tm, d), lambda i: (i, 0)), pl.BlockSpec((1, d), lambda i: (0, 0))],
        out_specs=pl.BlockSpec((tm, d), lambda i: (i, 0)),
        out_shape=jax.ShapeDtypeStruct((m, d), F32),
        compiler_params=_cparams(("parallel",)),
        name="final_norm",
    )(x, g.reshape(1, d))


def _gla_dir(q, k, v, al, wa, ba, s_ref, rev):
    c = q.shape[0]
    pre = _dot(al, wa, HI) + ba
    log_a = (jnp.minimum(pre, 0.0) - jnp.log(1.0 + jnp.exp(-jnp.abs(pre)))) * (1.0 / GLA_TAU)
    incl = _order_mask(c, rev, False, c)
    cum = _dot(incl.astype(F32), log_a, HI)
    tot = cum[0:1] if rev else cum[c - 1:c]
    q_dec = q * ((GLA_DK ** -0.5) * jnp.exp(cum))
    k_dec = k * jnp.exp(-cum)
    k_end = k * jnp.exp(tot - cum)
    scores = jnp.where(incl, _bdot_nt(q_dec, k_dec), 0.0)
    state = s_ref[...]
    o = _bdot(scores, v) + _bdot(q_dec, state)
    tot_col = _dot_tn(log_a, jnp.ones((c, GLA_DV), F32), HI)
    s_ref[...] = jnp.exp(tot_col) * state + _bdot_tn(k_end, v)
    return o


def _gla_kernel(qf, kf, vf, alf, qb, kb, vb, alb, waf, wab, ba, of_ref, ob_ref, sf_ref, sb_ref):
    @pl.when(pl.program_id(2) == 0)
    def _():
        sf_ref[...] = jnp.zeros_like(sf_ref)
        sb_ref[...] = jnp.zeros_like(sb_ref)

    of_ref[0] = _gla_dir(qf[0], kf[0], vf[0], alf[0], waf[...], ba[0:1], sf_ref, False)
    ob_ref[0] = _gla_dir(qb[0], kb[0], vb[0], alb[0], wab[...], ba[1:2], sb_ref, True)


def gla_scan(z, waf, wab, ba):
    b, s, _ = z.shape
    n = s // CHUNK
    c = CHUNK

    def fwd(off, w):
        return pl.BlockSpec((1, c, w), lambda bi, h, t: (bi, t, off // w + h))

    def bwd(off, w):
        return pl.BlockSpec((1, c, w), lambda bi, h, t: (bi, n - 1 - t, off // w + h))

    al_f = pl.BlockSpec((1, c, LANES), lambda bi, h, t: (bi, t, AL_OFF // LANES))
    al_b = pl.BlockSpec((1, c, LANES), lambda bi, h, t: (bi, n - 1 - t, AL_OFF // LANES))
    wspec = pl.BlockSpec((LANES, GLA_DK), lambda bi, h, t: (0, h))
    in_specs = [fwd(GQ_OFF, GLA_DK), fwd(GK_OFF, GLA_DK), fwd(GV_OFF, GLA_DV), al_f,
                bwd(GQ_OFF, GLA_DK), bwd(GK_OFF, GLA_DK), bwd(GV_OFF, GLA_DV), al_b,
                wspec, wspec, pl.BlockSpec((2, GLA_DK), lambda bi, h, t: (0, h))]
    out_specs = [pl.BlockSpec((1, c, GLA_DV), lambda bi, h, t: (bi, t, h)),
                 pl.BlockSpec((1, c, GLA_DV), lambda bi, h, t: (bi, n - 1 - t, h))]
    return pl.pallas_call(
        _gla_kernel,
        grid=(b, GLA_HEADS, n),
        in_specs=in_specs,
        out_specs=out_specs,
        out_shape=[jax.ShapeDtypeStruct((b, s, GLA_WIDTH), F32)] * 2,
        scratch_shapes=[pltpu.VMEM((GLA_DK, GLA_DV), F32)] * 2,
        compiler_params=_cparams(("parallel", "parallel", "arbitrary")),
        name="gla_scan",
    )(z, z, z, z, z, z, z, z, waf, wab, ba)


def _gla_post_kernel(of_ref, ob_ref, gate_ref, nw_ref, y_ref):
    o = of_ref[0] + ob_ref[0]
    ms = jnp.mean(o * o, axis=-1, keepdims=True)
    y = o * lax.rsqrt(ms + EPS) * nw_ref[...]
    y_ref[0] = (y * _silu(gate_ref[0])).astype(y_ref.dtype)


def gla_post(o_f, o_b, z, norm_w, tm):
    b, s, w = o_f.shape
    blk = pl.BlockSpec((1, tm, GLA_DV), lambda bi, t, h: (bi, t, h))
    return pl.pallas_call(
        _gla_post_kernel,
        grid=(b, s // tm, GLA_HEADS),
        in_specs=[blk, blk, pl.BlockSpec((1, tm, GLA_DV), lambda bi, t, h: (bi, t, GG_OFF // GLA_DV + h)),
                  pl.BlockSpec((1, GLA_DV), lambda bi, t, h: (0, 0))],
        out_specs=blk,
        out_shape=jax.ShapeDtypeStruct((b, s, w), BF16),
        compiler_params=_cparams(("parallel", "parallel", "parallel")),
        name="gla_post",
    )(o_f, o_b, z, norm_w.reshape(1, GLA_DV))


def _seg_sum(x, e, et):
    return _dot(_dot(x, e, HI), et, HI)


def _rwkv_prep_kernel(z_ref, zp_ref, zn_ref, mu_ref, w0_ref, w2f_ref, w2b_ref, a0_ref, a2_ref, g2_ref,
                      kk_ref, ka_ref, rk_ref, e_ref, et_ref,
                      r_out, k_out, v_out, kk_out, b_out, lwf_out, lwb_out, g_out, bonus_out):
    t = pl.program_id(1)
    tm = z_ref.shape[1]
    row = _iota2((tm, 1), 0)
    first = t == 0
    last = t == pl.num_programs(1) - 1

    def shifted(lo, hi):
        z = z_ref[0, :, lo:hi]
        prev_row = jnp.where(first, 0.0, zp_ref[0, 7:8, lo:hi])
        next_row = jnp.where(last, 0.0, zn_ref[0, 0:1, lo:hi])
        zprev = jnp.where(row == 0, prev_row, pltpu.roll(z, 1, axis=0))
        znext = jnp.where(row == tm - 1, next_row, pltpu.roll(z, tm - 1, axis=0))
        return z + mu_ref[0:1, lo:hi] * (zprev - z) + mu_ref[1:2, lo:hi] * (znext - z)

    e = e_ref[...]
    et = et_ref[...]
    a = _sigmoid(a0_ref[...] + _dot(shifted(ALR_OFF, ALR_OFF + LANES), a2_ref[...], HI))
    k = shifted(K_OFF, K_OFF + RWKV_WIDTH)
    kk_raw = k * kk_ref[...]
    kk = kk_raw * lax.rsqrt(_seg_sum(kk_raw * kk_raw, e, et) + EPS)
    kk_out[0] = kk
    b_out[0] = kk * a
    kmod = k * (1.0 + (a - 1.0) * ka_ref[...])
    k_out[0] = kmod
    r = shifted(R_OFF, R_OFF + RWKV_WIDTH)
    r_out[0] = r
    v = shifted(V_OFF, V_OFF + RWKV_WIDTH)
    v_out[0] = v
    bonus_out[0] = _seg_sum(r * kmod * rk_ref[...], e, et) * v
    g_out[0] = _dot(_sigmoid(shifted(GLR_OFF, GLR_OFF + GATE_LORA)), g2_ref[...], HI)
    decay_scale = -math.exp(-0.5)
    wf = w0_ref[0:1] + _dot(jnp.tanh(shifted(WLF_OFF, WLF_OFF + LANES)), w2f_ref[...], HI)
    lwf_out[0] = decay_scale * _sigmoid(wf)
    wb = w0_ref[1:2] + _dot(jnp.tanh(shifted(WLB_OFF, WLB_OFF + LANES)), w2b_ref[...], HI)
    lwb_out[0] = decay_scale * _sigmoid(wb)


def rwkv_prep(z, mu, w0, w2f, w2b, a0, a2, g2, k_k, k_a, r_k, seg_e, seg_et, tm):
    b, s, _ = z.shape
    nt = s // tm
    hb = tm // 8
    full = lambda arr: pl.BlockSpec(arr.shape, lambda bi, t: (0,) * arr.ndim)
    in_specs = [pl.BlockSpec((1, tm, RWKV_BLOCK), lambda bi, t: (bi, t, 0)),
                pl.BlockSpec((1, 8, RWKV_BLOCK), lambda bi, t: (bi, jnp.maximum(t * hb - 1, 0), 0)),
                pl.BlockSpec((1, 8, RWKV_BLOCK), lambda bi, t: (bi, jnp.minimum((t + 1) * hb, nt * hb - 1), 0))]
    params = [mu, w0, w2f, w2b, a0, a2, g2, k_k, k_a, r_k, seg_e, seg_et]
    in_specs += [full(p) for p in params]
    out_blk = pl.BlockSpec((1, tm, RWKV_WIDTH), lambda bi, t: (bi, t, 0))
    return pl.pallas_call(
        _rwkv_prep_kernel,
        grid=(b, nt),
        in_specs=in_specs,
        out_specs=[out_blk] * 9,
        out_shape=[jax.ShapeDtypeStruct((b, s, RWKV_WIDTH), F32)] * 9,
        compiler_params=_cparams(("parallel", "parallel")),
        name="rwkv_prep",
    )(z, z, z, *params)


def _stack_heads(x):
    lane = _iota2(x.shape, 1)
    return jnp.concatenate([jnp.where(lane < RWKV_HEAD, x, 0.0), jnp.where(lane < RWKV_HEAD, 0.0, x)], axis=0)


def _rwkv_dir(r, k, v, kk, bb, lw, h_ref, rev):
    c = r.shape[0]
    n2 = 2 * c
    incl_c = _order_mask(c, rev, False, c)
    cum = _dot(incl_c.astype(F32), lw, HI)
    tot = cum[0:1] if rev else cum[c - 1:c]
    e_neg = jnp.exp(-cum)
    e_end = jnp.exp(tot - cum)
    al2 = _stack_heads(-kk * jnp.exp(cum - lw))
    rb2 = _stack_heads(r * jnp.exp(cum))
    bt2 = _stack_heads(bb * e_neg)
    kt2 = _stack_heads(k * e_neg)
    be2 = _stack_heads(bb * e_end)
    ke2 = _stack_heads(k * e_end)
    v2 = _stack_heads(v)
    gram = _bdot_nt(jnp.concatenate([al2, rb2], axis=0), jnp.concatenate([bt2, kt2], axis=0))
    strict = _order_mask(n2, rev, True, c)
    incl = _order_mask(n2, rev, False, c)
    a_ab = jnp.where(strict, gram[:n2, :n2], 0.0)
    a_ak = jnp.where(strict, gram[:n2, n2:], 0.0)
    a_rb = jnp.where(incl, gram[n2:, :n2], 0.0)
    a_rk = jnp.where(incl, gram[n2:, n2:], 0.0)
    t_inv = _neumann_inverse(a_ab)
    wu = _bdot(t_inv, jnp.concatenate([al2, _bdot(a_ak, v2)], axis=1))
    state = h_ref[...]
    proj = _bdot(jnp.concatenate([wu[:, :LANES], rb2], axis=0), state)
    u2 = proj[:n2] + wu[:, LANES:]
    uv = jnp.concatenate([u2, v2], axis=0)
    o2 = proj[n2:] + _bdot(jnp.concatenate([a_rb, a_rk], axis=1), uv)
    tot_col = _dot_tn(lw, jnp.ones((c, LANES), F32), HI)
    h_ref[...] = jnp.exp(tot_col) * state + _bdot_tn(jnp.concatenate([be2, ke2], axis=0), uv)
    return o2[:c] + o2[c:]


def _rwkv_scan_kernel(rf, kf, vf, kkf, bf, lwf, rb, kb, vb, kkb, bb, lwb, of_ref, ob_ref, hf_ref, hb_ref):
    @pl.when(pl.program_id(2) == 0)
    def _():
        hf_ref[...] = jnp.zeros_like(hf_ref)
        hb_ref[...] = jnp.zeros_like(hb_ref)

    of_ref[0] = _rwkv_dir(rf[0], kf[0], vf[0], kkf[0], bf[0], lwf[0], hf_ref, False)
    ob_ref[0] = _rwkv_dir(rb[0], kb[0], vb[0], kkb[0], bb[0], lwb[0], hb_ref, True)


def rwkv_scan(r, k, v, kk, bb, lwf, lwb):
    b, s, _ = r.shape
    n = s // CHUNK
    fwd = pl.BlockSpec((1, CHUNK, LANES), lambda bi, p, t: (bi, t, p))
    bwd = pl.BlockSpec((1, CHUNK, LANES), lambda bi, p, t: (bi, n - 1 - t, p))
    return pl.pallas_call(
        _rwkv_scan_kernel,
        grid=(b, RWKV_WIDTH // LANES, n),
        in_specs=[fwd] * 6 + [bwd] * 6,
        out_specs=[fwd, bwd],
        out_shape=[jax.ShapeDtypeStruct((b, s, RWKV_WIDTH), F32)] * 2,
        scratch_shapes=[pltpu.VMEM((LANES, LANES), F32)] * 2,
        compiler_params=_cparams(("parallel", "parallel", "arbitrary")),
        name="rwkv_scan",
    )(r, k, v, kk, bb, lwf, r, k, v, kk, bb, lwb)


def _rwkv_post_kernel(of_ref, ob_ref, bonus_ref, g_ref, lnw_ref, lnb_ref, e_ref, et_ref, y_ref):
    e = e_ref[...]
    et = et_ref[...]
    o = of_ref[0] + ob_ref[0]
    mean = _seg_sum(o, e, et) * (1.0 / RWKV_HEAD)
    cen = o - mean
    var = _seg_sum(cen * cen, e, et) * (1.0 / RWKV_HEAD)
    y = cen * lax.rsqrt(var + RWKV_LN_EPS) * lnw_ref[...] + lnb_ref[...]
    y_ref[0] = ((y + bonus_ref[0]) * g_ref[0]).astype(y_ref.dtype)


def rwkv_post(o_f, o_b, bonus, g, ln_w, ln_b, seg_e, seg_et, tm):
    b, s, w = o_f.shape
    blk = pl.BlockSpec((1, tm, w), lambda bi, t: (bi, t, 0))
    full = lambda arr: pl.BlockSpec(arr.shape, lambda bi, t: (0,) * arr.ndim)
    params = [ln_w.reshape(1, w), ln_b.reshape(1, w), seg_e, seg_et]
    return pl.pallas_call(
        _rwkv_post_kernel,
        grid=(b, s // tm),
        in_specs=[blk] * 4 + [full(p) for p in params],
        out_specs=blk,
        out_shape=jax.ShapeDtypeStruct((b, s, w), BF16),
        compiler_params=_cparams(("parallel", "parallel")),
        name="rwkv_post",
    )(o_f, o_b, bonus, g, *params)


def _gdn_conv_kernel(z_ref, zp_ref, zn_ref, cw_ref, sc_ref, o_ref, *, l2norm):
    t = pl.program_id(1)
    tm = z_ref.shape[1]
    row = _iota2((tm, 1), 0)
    first = t == 0
    last = t == pl.num_programs(1) - 1
    z = z_ref[0]
    p6 = jnp.where(first, 0.0, zp_ref[0, 6:7, :])
    p7 = jnp.where(first, 0.0, zp_ref[0, 7:8, :])
    n0 = jnp.where(last, 0.0, zn_ref[0, 0:1, :])
    n1 = jnp.where(last, 0.0, zn_ref[0, 1:2, :])
    zm1 = jnp.where(row == 0, p7, pltpu.roll(z, 1, axis=0))
    zm2 = jnp.where(row == 0, p6, jnp.where(row == 1, p7, pltpu.roll(z, 2, axis=0)))
    zp1 = jnp.where(row == tm - 1, n0, pltpu.roll(z, tm - 1, axis=0))
    zp2 = jnp.where(row == tm - 1, n1, jnp.where(row == tm - 2, n0, pltpu.roll(z, tm - 2, axis=0)))
    y = (cw_ref[0:1] * zm2 + cw_ref[1:2] * zm1 + cw_ref[2:3] * z + cw_ref[3:4] * zp1 + cw_ref[4:5] * zp2)
    y = _silu(y)
    if l2norm:
        for h in range(y.shape[1] // GDN_DK):
            sl = slice(h * GDN_DK, (h + 1) * GDN_DK)
            yh = y[:, sl]
            ss = jnp.sum(yh * yh, axis=-1, keepdims=True)
            o_ref[0, :, sl] = yh * lax.rsqrt(ss + EPS) * sc_ref[0:1, sl]
    else:
        o_ref[0] = y


def gdn_conv(z, conv_w, scale, col_off, width, l2norm, tm, cw):
    b, s, _ = z.shape
    nt = s // tm
    hb = tm // 8
    cb = col_off // cw
    in_specs = [pl.BlockSpec((1, tm, cw), lambda bi, t, c: (bi, t, cb + c)),
                pl.BlockSpec((1, 8, cw), lambda bi, t, c: (bi, jnp.maximum(t * hb - 1, 0), cb + c)),
                pl.BlockSpec((1, 8, cw), lambda bi, t, c: (bi, jnp.minimum((t + 1) * hb, nt * hb - 1), cb + c)),
                pl.BlockSpec((conv_w.shape[0], cw), lambda bi, t, c: (0, cb + c)),
                pl.BlockSpec((1, cw), lambda bi, t, c: (0, c))]
    return pl.pallas_call(
        functools.partial(_gdn_conv_kernel, l2norm=l2norm),
        grid=(b, nt, width // cw),
        in_specs=in_specs,
        out_specs=pl.BlockSpec((1, tm, cw), lambda bi, t, c: (bi, t, c)),
        out_shape=jax.ShapeDtypeStruct((b, s, width), F32),
        compiler_params=_cparams(("parallel", "parallel", "parallel")),
        name="gdn_conv_norm" if l2norm else "gdn_conv",
    )(z, z, z, conv_w, scale)


def _gdn_dir(q, k, v, zs, neg_a, dtb, s_ref, d, kh, rev):
    c = q.shape[0]
    n2 = 2 * c
    sig = _sigmoid(zs)
    gg = neg_a * _softplus(zs + dtb)
    sel_r = _iota2((LANES, LANES), 0)
    beta_cols, g_cols = [], []
    for e in range(2):
        vh = 2 * kh + e
        beta_cols.append(_dot(sig, (sel_r == d * GDN_VHEADS + vh).astype(F32), HI))
        g_cols.append(_dot(gg, (sel_r == 2 * GDN_VHEADS + d * GDN_VHEADS + vh).astype(F32), HI))
    beta2 = jnp.concatenate(beta_cols, axis=0)
    g2 = jnp.concatenate(g_cols, axis=0)
    incl = _order_mask(n2, rev, False, c)
    strict = _order_mask(n2, rev, True, c)
    same = _order_mask(n2, False, False, c) | _order_mask(n2, True, False, c)
    gam_c = _dot(incl.astype(F32), g2, HI)
    gam_r = _dot_tn(g2, _order_mask(n2, not rev, False, c).astype(F32), HI)
    tot_c = _dot(same.astype(F32), g2, HI)
    diff = gam_c - gam_r
    dec_s = jnp.where(strict, jnp.exp(jnp.where(strict, diff, 0.0)), 0.0)
    dec_i = jnp.where(incl, jnp.exp(jnp.where(incl, diff, 0.0)), 0.0)
    k2 = jnp.concatenate([k, k], axis=0)
    q2 = jnp.concatenate([q, q], axis=0)
    gram = _bdot_nt(jnp.concatenate([k2, q2], axis=0), k2)
    a_mat = gram[:n2] * beta2 * dec_s
    qk = gram[n2:] * dec_i
    t_inv = _neumann_inverse(-a_mat)
    e_gam = jnp.exp(gam_c)
    v2 = jnp.concatenate([v[:, :GDN_DV], v[:, GDN_DV:]], axis=0)
    uw = _bdot(t_inv, jnp.concatenate([v2 * beta2, k2 * beta2 * e_gam], axis=1))
    u2 = uw[:, :GDN_DV]
    w2 = uw[:, GDN_DV:]
    qd2 = q2 * e_gam
    ke2 = k2 * jnp.exp(tot_c - gam_c)
    dl2 = jnp.exp(tot_c)
    ws, qs = [], []
    for e in range(2):
        rows = slice(e * c, (e + 1) * c)
        st = s_ref[e]
        both = _bdot(jnp.concatenate([w2[rows], qd2[rows]], axis=0), st)
        ws.append(both[:c])
        qs.append(both[c:])
    vnew2 = u2 - jnp.concatenate(ws, axis=0)
    o2 = jnp.concatenate(qs, axis=0) + _bdot(qk, vnew2)
    for e in range(2):
        rows = slice(e * c, (e + 1) * c)
        s_ref[e] = s_ref[e] * dl2[e * c:e * c + 1, :] + _bdot_tn(ke2[rows], vnew2[rows])
    return jnp.concatenate([o2[:c], o2[c:]], axis=1)


def _gdn_scan_kernel(qf, kf, vf, zf, qb, kb, vb, zb, na_ref, dtb_ref, of_ref, ob_ref, sf_ref, sb_ref):
    @pl.when(pl.program_id(2) == 0)
    def _():
        sf_ref[...] = jnp.zeros_like(sf_ref)
        sb_ref[...] = jnp.zeros_like(sb_ref)

    kh = pl.program_id(1)
    na = na_ref[...]
    dtb = dtb_ref[...]
    of_ref[0] = _gdn_dir(qf[0], kf[0], vf[0], zf[0], na, dtb, sf_ref, 0, kh, False)
    ob_ref[0] = _gdn_dir(qb[0], kb[0], vb[0], zb[0], na, dtb, sb_ref, 1, kh, True)


def gdn_scan(qk, v, zs, neg_a, dtb):
    b, s, _ = v.shape
    n = s // CHUNK
    c = CHUNK
    nk = GDN_KHEADS

    def spec(w, off, rev):
        if rev:
            return pl.BlockSpec((1, c, w), lambda bi, h, t: (bi, n - 1 - t, off + h))
        return pl.BlockSpec((1, c, w), lambda bi, h, t: (bi, t, off + h))

    def zspec(rev):
        if rev:
            return pl.BlockSpec((1, c, LANES), lambda bi, h, t: (bi, n - 1 - t, 0))
        return pl.BlockSpec((1, c, LANES), lambda bi, h, t: (bi, t, 0))

    row = pl.BlockSpec((1, LANES), lambda bi, h, t: (0, 0))
    in_specs = [spec(GDN_DK, 0, False), spec(GDN_DK, nk, False), spec(2 * GDN_DV, 0, False), zspec(False),
                spec(GDN_DK, 0, True), spec(GDN_DK, nk, True), spec(2 * GDN_DV, 0, True), zspec(True), row, row]
    return pl.pallas_call(
        _gdn_scan_kernel,
        grid=(b, nk, n),
        in_specs=in_specs,
        out_specs=[spec(2 * GDN_DV, 0, False), spec(2 * GDN_DV, 0, True)],
        out_shape=[jax.ShapeDtypeStruct((b, s, GDN_VW), F32)] * 2,
        scratch_shapes=[pltpu.VMEM((2, GDN_DK, GDN_DV), F32)] * 2,
        compiler_params=_cparams(("parallel", "parallel", "arbitrary")),
        name="gdn_scan",
    )(qk, qk, v, zs, qk, qk, v, zs, neg_a, dtb)


def _gdn_post_kernel(of_ref, ob_ref, gate_ref, nw_ref, y_ref):
    nw = nw_ref[...]
    for h in range(of_ref.shape[2] // GDN_DV):
        sl = slice(h * GDN_DV, (h + 1) * GDN_DV)
        o = of_ref[0, :, sl] + ob_ref[0, :, sl]
        ms = jnp.mean(o * o, axis=-1, keepdims=True)
        y = o * lax.rsqrt(ms + EPS) * nw
        y_ref[0, :, sl] = (y * _silu(gate_ref[0, :, sl])).astype(y_ref.dtype)


def gdn_post(o_f, o_b, z, gate_off, norm_w, tm, cw):
    b, s, w = o_f.shape
    blk = pl.BlockSpec((1, tm, cw), lambda bi, t, c: (bi, t, c))
    gb = gate_off // cw
    return pl.pallas_call(
        _gdn_post_kernel,
        grid=(b, s // tm, w // cw),
        in_specs=[blk, blk, pl.BlockSpec((1, tm, cw), lambda bi, t, c: (bi, t, gb + c)),
                  pl.BlockSpec((1, GDN_DV), lambda bi, t, c: (0, 0))],
        out_specs=blk,
        out_shape=jax.ShapeDtypeStruct((b, s, w), BF16),
        compiler_params=_cparams(("parallel", "parallel", "parallel")),
        name="gdn_post",
    )(o_f, o_b, z, norm_w.reshape(1, GDN_DV))


def _moe_router_kernel(x_ref, g_ref, wg_ref, bg_ref, wr_ref, br_ref, h_ref, gates_ref):
    x = x_ref[...]
    ms = jnp.mean(x * x, axis=-1, keepdims=True)
    h = x * lax.rsqrt(ms + EPS) * g_ref[...]
    h_ref[...] = h.astype(h_ref.dtype)
    lane_i = _iota2((x.shape[0], LANES), 1)
    lane = lane_i.astype(F32)
    lane_grp = lax.shift_right_logical(lane_i, 3).astype(F32)
    neg = -jnp.inf
    gl = jnp.where(lane_i < N_GROUPS, _dot(h, wg_ref[...], HI) + bg_ref[...], neg)
    gmax = jnp.max(gl, axis=-1, keepdims=True)
    gidx = jnp.min(jnp.where(gl == gmax, lane, float(LANES)), axis=-1, keepdims=True)
    grp_w = 1.0 / jnp.sum(jnp.exp(gl - gmax), axis=-1, keepdims=True)
    el = _dot(h, wr_ref[...], HI) + br_ref[...]
    sel = jnp.where((lane_i < N_EXPERTS) & (lane_grp == gidx), el, neg)
    m1 = jnp.max(sel, axis=-1, keepdims=True)
    i1 = jnp.min(jnp.where(sel == m1, lane, float(LANES)), axis=-1, keepdims=True)
    sel2 = jnp.where(lane == i1, neg, sel)
    m2 = jnp.max(sel2, axis=-1, keepdims=True)
    i2 = jnp.min(jnp.where(sel2 == m2, lane, float(LANES)), axis=-1, keepdims=True)
    e2 = jnp.exp(m2 - m1)
    w1 = grp_w / (1.0 + e2)
    w2 = grp_w * e2 / (1.0 + e2)
    gates_ref[...] = jnp.where(lane == i1, w1, 0.0) + jnp.where(lane == i2, w2, 0.0)


def moe_router(x, g, w_group, b_group, w_router, b_router, tm):
    m, d = x.shape
    tm = min(tm, m)
    pad = lambda w: jnp.pad(w, ((0, 0), (0, LANES - w.shape[1])))
    full = lambda arr: pl.BlockSpec(arr.shape, lambda i: (0, 0))
    params = [g.reshape(1, d), pad(w_group), pad(b_group.reshape(1, -1)), pad(w_router), pad(b_router.reshape(1, -1))]
    return pl.pallas_call(
        _moe_router_kernel,
        grid=(m // tm,),
        in_specs=[pl.BlockSpec((tm, d), lambda i: (i, 0))] + [full(p) for p in params],
        out_specs=[pl.BlockSpec((tm, d), lambda i: (i, 0)), pl.BlockSpec((tm, LANES), lambda i: (i, 0))],
        out_shape=[jax.ShapeDtypeStruct((m, d), BF16), jax.ShapeDtypeStruct((m, LANES), F32)],
        compiler_params=_cparams(("parallel",)),
        name="moe_router",
    )(x, *params)


def _moe_ffn_kernel(h_ref, gates_ref, x_ref, wg_ref, wu_ref, wd_ref, o_ref, acc_ref):
    e = pl.program_id(1)

    @pl.when(e == 0)
    def _():
        acc_ref[...] = x_ref[...]

    h = h_ref[...]
    lane = _iota2(gates_ref.shape, 1)
    gcol = jnp.sum(jnp.where(lane == e, gates_ref[...], 0.0), axis=-1, keepdims=True)
    hid = _silu(_dot(h, wg_ref[0])) * _dot(h, wu_ref[0]) * gcol
    acc_ref[...] += _dot(hid.astype(BF16), wd_ref[0])

    @pl.when(e == pl.num_programs(1) - 1)
    def _():
        o_ref[...] = acc_ref[...]


def moe_ffn(h, gates, x, w_gate, w_up, w_down, tm):
    m, d = x.shape
    tm = min(tm, m)
    ne, _, ff = w_gate.shape
    return pl.pallas_call(
        _moe_ffn_kernel,
        grid=(m // tm, ne),
        in_specs=[pl.BlockSpec((tm, d), lambda i, e: (i, 0)),
                  pl.BlockSpec((tm, LANES), lambda i, e: (i, 0)),
                  pl.BlockSpec((tm, d), lambda i, e: (i, 0)),
                  pl.BlockSpec((1, d, ff), lambda i, e: (e, 0, 0)),
                  pl.BlockSpec((1, d, ff), lambda i, e: (e, 0, 0)),
                  pl.BlockSpec((1, ff, d), lambda i, e: (e, 0, 0))],
        out_specs=pl.BlockSpec((tm, d), lambda i, e: (i, 0)),
        out_shape=jax.ShapeDtypeStruct((m, d), F32),
        scratch_shapes=[pltpu.VMEM((tm, d), F32)],
        compiler_params=_cparams(("parallel", "arbitrary")),
        name="moe_ffn",
    )(h, gates, x, w_gate, w_up, w_down)


def _pad_rows(w, rows):
    return jnp.pad(w, ((0, rows - w.shape[0]), (0, 0)))


def _even_layer(x, norm_g, w_in, gla_w_alpha, gla_b_alpha, gla_norm, mu, w0, w2, a0, a2, g2,
                k_k, k_a, r_k, ln_w, ln_b, w_out):
    b, s, d = x.shape
    gla_cols, rw = w_in[:, :3104], w_in[:, 3104:]
    mu_g = lambda lo, hi, width: jnp.pad(mu[:, lo:hi], ((0, 0), (0, width - (hi - lo))))
    pad_c = lambda w, width: jnp.pad(w, ((0, 0), (0, width - w.shape[1])))
    w_cat = jnp.concatenate([
        rw[:, 0:3072], rw[:, 3360:3616],
        pad_c(rw[:, 3072:3168], LANES), pad_c(rw[:, 3168:3264], LANES), pad_c(rw[:, 3264:3360], LANES),
        pad_c(gla_cols[:, 3072:3104], LANES),
        gla_cols[:, 0:3072]], axis=1).astype(BF16)
    mu_cat = jnp.concatenate([
        mu[:, 0:3072], mu[:, 3360:3616], mu_g(3072, 3168, LANES), mu_g(3168, 3264, LANES),
        mu_g(3264, 3360, LANES), jnp.zeros((2, LANES), F32)], axis=1)
    z = norm_matmul(x.reshape(b * s, d), norm_g, w_cat, 512, 768).reshape(b, s, EVEN_PAD)

    waf = _pad_rows(gla_w_alpha[0], LANES)
    wab = jnp.pad(gla_w_alpha[1], ((GLA_LOWRANK, LANES - 2 * GLA_LOWRANK), (0, 0)))
    gla_f, gla_b = gla_scan(z, waf, wab, gla_b_alpha)
    y_gla = gla_post(gla_f, gla_b, z, gla_norm, 256)

    head_of_lane = jnp.arange(RWKV_WIDTH) // RWKV_HEAD
    seg_e = (head_of_lane[:, None] == jnp.arange(LANES)[None, :]).astype(F32)
    seg_et = seg_e.T
    row = lambda p: p.reshape(1, RWKV_WIDTH)
    r, k, v, kk, bb, lwf, lwb, g, bonus = rwkv_prep(
        z, mu_cat, w0, _pad_rows(w2[0], LANES), _pad_rows(w2[1], LANES), row(a0), _pad_rows(a2, LANES), g2,
        row(k_k), row(k_a), row(r_k), seg_e, seg_et, 128)
    rw_f, rw_b = rwkv_scan(r, k, v, kk, bb, lwf, lwb)
    y_rwkv = rwkv_post(rw_f, rw_b, bonus, g, ln_w, ln_b, seg_e, seg_et, 256)

    w_out = w_out.astype(BF16)
    out = proj_residual([y_gla.reshape(b * s, GLA_WIDTH), y_rwkv.reshape(b * s, RWKV_WIDTH)],
                        [w_out[:GLA_WIDTH], w_out[GLA_WIDTH:]], x.reshape(b * s, d), 512, 1024)
    return out.reshape(b, s, d)


def _odd_layer(x, norm_g, w_in, conv_w, a_log, dt_bias, norm_w, w_out):
    b, s, d = x.shape
    x2 = x.reshape(b * s, d)
    main = GDN_QKV + GDN_VW
    z = norm_matmul(x2, norm_g, w_in[:, :main].astype(BF16), 512, 1024).reshape(b, s, main)
    zs = norm_matmul(x2, norm_g, w_in[:, main:].astype(BF16), 512, LANES).reshape(b, s, LANES)
    scale = jnp.concatenate([jnp.full((1, GDN_KW), GDN_DK ** -0.5, F32), jnp.ones((1, GDN_KW), F32)], axis=1)
    qk = gdn_conv(z, conv_w, scale, 0, 2 * GDN_KW, True, 256, 1024)
    v = gdn_conv(z, conv_w, scale, 2 * GDN_KW, GDN_VW, False, 256, 1024)
    zero = jnp.zeros((2 * GDN_VHEADS,), F32)
    neg_a = jnp.concatenate([zero, -jnp.exp(a_log.reshape(-1))]).reshape(1, LANES)
    dtb = jnp.concatenate([zero, dt_bias.reshape(-1)]).reshape(1, LANES)
    o_f, o_b = gdn_scan(qk, v, zs, neg_a, dtb)
    y = gdn_post(o_f, o_b, z, GDN_QKV, norm_w, 256, 1024)
    out = proj_residual([y.reshape(b * s, GDN_VW)], [w_out.astype(BF16)], x2, 512, 1024)
    return out.reshape(b, s, d)


def _moe_layer(x, norm_g, w_group, b_group, w_router, b_router, w_gate, w_up, w_down):
    b, s, d = x.shape
    x2 = x.reshape(b * s, d)
    h, gates = moe_router(x2, norm_g, w_group, b_group, w_router, b_router, 256)
    out = moe_ffn(h, gates, x2, w_gate.astype(BF16), w_up.astype(BF16), w_down.astype(BF16), 512)
    return out.reshape(b, s, d)


def kernel(x, norm_mix, norm_ffn, norm_final, ev_w_in, ev_gla_w_alpha, ev_gla_b_alpha, ev_gla_norm, ev_rwkv_mu, ev_rwkv_w0, ev_rwkv_w2, ev_rwkv_a0, ev_rwkv_a2, ev_rwkv_g2, ev_rwkv_k_k, ev_rwkv_k_a, ev_rwkv_r_k, ev_rwkv_ln_w, ev_rwkv_ln_b, ev_w_out, od_w_in, od_conv, od_a_log, od_dt_bias, od_norm, od_w_out, moe_w_group, moe_b_group, moe_w_router, moe_b_router, moe_w_gate, moe_w_up, moe_w_down):
    depth = norm_mix.shape[0]
    for i in range(depth):
        j = i // 2
        if i % 2 == 0:
            x = _even_layer(x, norm_mix[i], ev_w_in[j], ev_gla_w_alpha[j], ev_gla_b_alpha[j], ev_gla_norm[j],
                            ev_rwkv_mu[j], ev_rwkv_w0[j], ev_rwkv_w2[j], ev_rwkv_a0[j], ev_rwkv_a2[j],
                            ev_rwkv_g2[j], ev_rwkv_k_k[j], ev_rwkv_k_a[j], ev_rwkv_r_k[j],
                            ev_rwkv_ln_w[j], ev_rwkv_ln_b[j], ev_w_out[j])
        else:
            x = _odd_layer(x, norm_mix[i], od_w_in[j], od_conv[j], od_a_log[j], od_dt_bias[j],
                           od_norm[j], od_w_out[j])
        x = _moe_layer(x, norm_ffn[i], moe_w_group[i], moe_b_group[i], moe_w_router[i], moe_b_router[i],
                       moe_w_gate[i], moe_w_up[i], moe_w_down[i])
    b, s, d = x.shape
    return final_norm(x.reshape(b * s, d), norm_final, 512).reshape(b, s, d)
```

```python
import functools
import math

import jax
import jax.numpy as jnp
from jax import lax
from jax.experimental import pallas as pl
from jax.experimental.pallas import tpu as pltpu

F32 = jnp.float32
BF16 = jnp.bfloat16
HI = lax.Precision.HIGHEST

EPS = 1e-6
CHUNK = 64
LANES = 128
VMEM_LIMIT = 56 * 1024 * 1024

D_MODEL = 2048
GLA_HEADS = 4
GLA_DK = 128
GLA_DV = 256
GLA_KW = 512
GLA_WIDTH = 1024
GLA_LOWRANK = 16
GLA_TAU = 16.0
RWKV_HEAD = 64
RWKV_WIDTH = 1024
RWKV_HEADS = 16
DECAY_LORA = 96
ICLR_LORA = 96
GATE_LORA = 256
RWKV_LN_EPS = 64e-5
GDN_DK = 128
GDN_DV = 128
GDN_KHEADS = 16
GDN_VHEADS = 32
GDN_KW = 2048
GDN_VW = 4096
GDN_QKV = 8192
RWKV_PAIRS_PER_STEP = 4
GDN_HEADS_PER_STEP = 4
N_GROUPS = 4
EXPERTS_PER_GROUP = 8
N_EXPERTS = 32
EXPERT_FF = 512

GQ_OFF, GK_OFF, GV_OFF, GG_OFF = 0, 512, 1024, 2048
R_OFF, K_OFF, V_OFF = 3072, 4096, 5120
GLR_OFF, WLF_OFF, WLB_OFF, ALR_OFF, AL_OFF = 6144, 6400, 6528, 6656, 6784
RWKV_MAIN = 3072
RWKV_SMALL = 768
EVEN_PAD = 6912


def _cparams(sem):
    return pltpu.CompilerParams(dimension_semantics=sem, vmem_limit_bytes=VMEM_LIMIT)


def _mm(a, b, ca, cb, precision):
    if a.ndim == 3:
        dims = (((ca + 1,), (cb + 1,)), ((0,), (0,)))
    else:
        dims = (((ca,), (cb,)), ((), ()))
    return lax.dot_general(a, b, dims, preferred_element_type=F32, precision=precision)


def _dot(a, b, precision=None):
    return _mm(a, b, 1, 0, precision)


def _dot_nt(a, b, precision=None):
    return _mm(a, b, 1, 1, precision)


def _dot_tn(a, b, precision=None):
    return _mm(a, b, 0, 0, precision)


def _bdot(a, b):
    return _dot(a.astype(BF16), b.astype(BF16))


def _bdot_nt(a, b):
    return _dot_nt(a.astype(BF16), b.astype(BF16))


def _bdot_tn(a, b):
    return _dot_tn(a.astype(BF16), b.astype(BF16))


def _sigmoid(x):
    return 1.0 / (1.0 + jnp.exp(-x))


def _silu(x):
    return x * _sigmoid(x)


def _softplus(x):
    return jnp.maximum(x, 0.0) + jnp.log(1.0 + jnp.exp(-jnp.abs(x)))


def _iota2(shape, dim):
    return lax.broadcasted_iota(jnp.int32, shape, dim)


def _order_mask(n, rev, strict, block):
    i = _iota2((n, n), 0)
    j = _iota2((n, n), 1)
    if rev:
        m = (j > i) if strict else (j >= i)
    else:
        m = (j < i) if strict else (j <= i)
    if block < n:
        sh = block.bit_length() - 1
        m = m & (lax.shift_right_logical(i, sh) == lax.shift_right_logical(j, sh))
    return m


def _neumann_inverse(nmat):
    n = nmat.shape[-1]
    eye = (_iota2((n, n), 0) == _iota2((n, n), 1)).astype(F32)
    t = eye + nmat
    nb = nmat.astype(BF16)
    p = _dot(nb, nb)
    for step in range(5):
        pb = p.astype(BF16)
        if step < 4:
            both = _dot(pb, jnp.concatenate([pb, t.astype(BF16)], axis=-1))
            p = both[..., :n]
            t = t + both[..., n:]
        else:
            t = t + _dot(pb, t.astype(BF16))
    return t


def _norm_matmul_kernel(x_ref, g_ref, w_ref, o_ref, h_ref):
    @pl.when(pl.program_id(1) == 0)
    def _():
        x = x_ref[...]
        ms = jnp.mean(x * x, axis=-1, keepdims=True)
        h_ref[...] = (x * lax.rsqrt(ms + EPS) * g_ref[...]).astype(BF16)

    o_ref[...] = _dot(h_ref[...], w_ref[...])


def norm_matmul(x, g, w, tm, tn):
    m, d = x.shape
    tm = min(tm, m)
    n = w.shape[1]
    return pl.pallas_call(
        _norm_matmul_kernel,
        grid=(m // tm, n // tn),
        in_specs=[pl.BlockSpec((tm, d), lambda i, j: (i, 0)),
                  pl.BlockSpec((1, d), lambda i, j: (0, 0)),
                  pl.BlockSpec((d, tn), lambda i, j: (0, j))],
        out_specs=pl.BlockSpec((tm, tn), lambda i, j: (i, j)),
        out_shape=jax.ShapeDtypeStruct((m, n), F32),
        scratch_shapes=[pltpu.VMEM((tm, d), BF16)],
        compiler_params=_cparams(("parallel", "arbitrary")),
        name="norm_matmul",
    )(x, g.reshape(1, d), w)


def _proj_residual_kernel(*refs, n_lhs):
    x_ref = refs[2 * n_lhs]
    o_ref = refs[2 * n_lhs + 1]
    acc = x_ref[...]
    for t in range(n_lhs):
        acc = acc + _dot(refs[t][...], refs[n_lhs + t][...])
    o_ref[...] = acc


def proj_residual(ys, ws, x, tm, tn):
    m, n = x.shape
    tm = min(tm, m)
    n_lhs = len(ys)
    in_specs = [pl.BlockSpec((tm, y.shape[1]), lambda i, j: (i, 0)) for y in ys]
    in_specs += [pl.BlockSpec((w.shape[0], tn), lambda i, j: (0, j)) for w in ws]
    in_specs += [pl.BlockSpec((tm, tn), lambda i, j: (i, j))]
    return pl.pallas_call(
        functools.partial(_proj_residual_kernel, n_lhs=n_lhs),
        grid=(m // tm, n // tn),
        in_specs=in_specs,
        out_specs=pl.BlockSpec((tm, tn), lambda i, j: (i, j)),
        out_shape=jax.ShapeDtypeStruct((m, n), F32),
        compiler_params=_cparams(("parallel", "arbitrary")),
        name="proj_residual",
    )(*ys, *ws, x)


def _final_norm_kernel(x_ref, g_ref, o_ref):
    x = x_ref[...]
    ms = jnp.mean(x * x, axis=-1, keepdims=True)
    o_ref[...] = x * lax.rsqrt(ms + EPS) * g_ref[...]


def final_norm(x, g, tm):
    m, d = x.shape
    tm = min(tm, m)
    return pl.pallas_call(
        _final_norm_kernel,
        grid=(m // tm,),
        in_specs=[pl.BlockSpec((tm, d), lambda i: (i, 0)), pl.BlockSpec((1, d), lambda i: (0, 0))],
        out_specs=pl.BlockSpec((tm, d), lambda i: (i, 0)),
        out_shape=jax.ShapeDtypeStruct((m, d), F32),
        compiler_params=_cparams(("parallel",)),
        name="final_norm",
    )(x, g.reshape(1, d))


def _gla_chunks(q, k, v, cum, revs, s_ref):
    c = q.shape[1]
    tot = jnp.stack([cum[j, (0 if rev else c - 1)][None] for j, rev in enumerate(revs)], axis=0)
    incl = jnp.stack([_order_mask(c, rev, False, c) for rev in revs], axis=0)
    q_dec = q * ((GLA_DK ** -0.5) * jnp.exp(cum))
    k_dec = k * jnp.exp(-cum)
    k_end = k * jnp.exp(tot - cum)
    scores = jnp.where(incl, _bdot_nt(q_dec, k_dec), 0.0)
    state = s_ref[...]
    o = _bdot(scores, v) + _bdot_nt(q_dec, state)
    s_ref[...] = jnp.exp(tot) * state + _bdot_tn(v, k_end)
    return o


def _gla_kernel(qf, kf, vf, alf, qb, kb, vb, alb, wa_ref, ba_ref, of_ref, ob_ref, s_ref):
    @pl.when(pl.program_id(1) == 0)
    def _():
        s_ref[...] = jnp.zeros_like(s_ref)

    c = qf.shape[1]
    qs, ks, vs, cums, revs = [], [], [], [], []
    for d, (q_ref, k_ref, v_ref, al_ref) in enumerate(((qf, kf, vf, alf), (qb, kb, vb, alb))):
        pre = _dot(al_ref[0], wa_ref[d], HI) + ba_ref[d:d + 1]
        log_a = (jnp.minimum(pre, 0.0) - jnp.log(1.0 + jnp.exp(-jnp.abs(pre)))) * (1.0 / GLA_TAU)
        cum = _dot(_order_mask(c, d == 1, False, c).astype(F32), log_a, HI)
        for h in range(GLA_HEADS):
            ksl = slice(h * GLA_DK, (h + 1) * GLA_DK)
            qs.append(q_ref[0, :, ksl])
            ks.append(k_ref[0, :, ksl])
            vs.append(v_ref[0, :, h * GLA_DV:(h + 1) * GLA_DV])
            cums.append(cum[:, ksl])
            revs.append(d == 1)
    o = _gla_chunks(jnp.stack(qs), jnp.stack(ks), jnp.stack(vs), jnp.stack(cums), revs, s_ref)
    for d, o_ref in enumerate((of_ref, ob_ref)):
        for h in range(GLA_HEADS):
            o_ref[0, :, h * GLA_DV:(h + 1) * GLA_DV] = o[d * GLA_HEADS + h]


def gla_scan(z, wa, ba):
    b, s, _ = z.shape
    n = s // CHUNK
    c = CHUNK

    def fwd(off, w):
        return pl.BlockSpec((1, c, w), lambda bi, t: (bi, t, off // w))

    def bwd(off, w):
        return pl.BlockSpec((1, c, w), lambda bi, t: (bi, n - 1 - t, off // w))

    in_specs = [fwd(GQ_OFF, GLA_KW), fwd(GK_OFF, GLA_KW), fwd(GV_OFF, GLA_WIDTH), fwd(AL_OFF, LANES),
                bwd(GQ_OFF, GLA_KW), bwd(GK_OFF, GLA_KW), bwd(GV_OFF, GLA_WIDTH), bwd(AL_OFF, LANES),
                pl.BlockSpec(wa.shape, lambda bi, t: (0, 0, 0)), pl.BlockSpec(ba.shape, lambda bi, t: (0, 0))]
    out_specs = [pl.BlockSpec((1, c, GLA_WIDTH), lambda bi, t: (bi, t, 0)),
                 pl.BlockSpec((1, c, GLA_WIDTH), lambda bi, t: (bi, n - 1 - t, 0))]
    return pl.pallas_call(
        _gla_kernel,
        grid=(b, n),
        in_specs=in_specs,
        out_specs=out_specs,
        out_shape=[jax.ShapeDtypeStruct((b, s, GLA_WIDTH), F32)] * 2,
        scratch_shapes=[pltpu.VMEM((2 * GLA_HEADS, GLA_DV, GLA_DK), F32)],
        compiler_params=_cparams(("parallel", "arbitrary")),
        name="gla_scan",
    )(z, z, z, z, z, z, z, z, wa, ba)


def _gla_post_kernel(of_ref, ob_ref, gate_ref, nw_ref, y_ref):
    o = of_ref[0] + ob_ref[0]
    ms = jnp.mean(o * o, axis=-1, keepdims=True)
    y = o * lax.rsqrt(ms + EPS) * nw_ref[...]
    y_ref[0] = (y * _silu(gate_ref[0])).astype(y_ref.dtype)


def gla_post(o_f, o_b, z, norm_w, tm):
    b, s, w = o_f.shape
    blk = pl.BlockSpec((1, tm, GLA_DV), lambda bi, t, h: (bi, t, h))
    return pl.pallas_call(
        _gla_post_kernel,
        grid=(b, s // tm, GLA_HEADS),
        in_specs=[blk, blk, pl.BlockSpec((1, tm, GLA_DV), lambda bi, t, h: (bi, t, GG_OFF // GLA_DV + h)),
                  pl.BlockSpec((1, GLA_DV), lambda bi, t, h: (0, 0))],
        out_specs=blk,
        out_shape=jax.ShapeDtypeStruct((b, s, w), BF16),
        compiler_params=_cparams(("parallel", "parallel", "parallel")),
        name="gla_post",
    )(o_f, o_b, z, norm_w.reshape(1, GLA_DV))


def _seg_sum(x, e, et):
    return _dot(_dot(x, e, HI), et, HI)


def _rwkv_prep_kernel(z_ref, zp_ref, zn_ref, y_ref, yp_ref, yn_ref, mu_ref, w0_ref, w2f_ref, w2b_ref, a0_ref, a2_ref, g2_ref,
                      kk_ref, ka_ref, rk_ref, e_ref, et_ref,
                      r_out, k_out, v_out, kk_out, b_out, cumf_out, excf_out, cumb_out, excb_out, g_out, bonus_out):
    t = pl.program_id(1)
    tm = z_ref.shape[1]
    row = _iota2((tm, 1), 0)
    first = t == 0
    last = t == pl.num_programs(1) - 1

    def shifted(lo, hi):
        cur, prv, nxt, base = (z_ref, zp_ref, zn_ref, R_OFF) if lo < GLR_OFF else (y_ref, yp_ref, yn_ref, GLR_OFF)
        z = cur[0, :, lo - base:hi - base]
        prev_row = jnp.where(first, 0.0, prv[0, 7:8, lo - base:hi - base])
        next_row = jnp.where(last, 0.0, nxt[0, 0:1, lo - base:hi - base])
        zprev = jnp.where(row == 0, prev_row, pltpu.roll(z, 1, axis=0))
        znext = jnp.where(row == tm - 1, next_row, pltpu.roll(z, tm - 1, axis=0))
        mu = mu_ref[:, lo - R_OFF:hi - R_OFF]
        return z + mu[0:1] * (zprev - z) + mu[1:2] * (znext - z)

    e = e_ref[...]
    et = et_ref[...]
    a = _sigmoid(a0_ref[...] + _dot(shifted(ALR_OFF, ALR_OFF + LANES), a2_ref[...], HI))
    k = shifted(K_OFF, K_OFF + RWKV_WIDTH)
    kk_raw = k * kk_ref[...]
    kk = kk_raw * lax.rsqrt(_seg_sum(kk_raw * kk_raw, e, et) + EPS)
    kk_out[0] = kk
    b_out[0] = kk * a
    kmod = k * (1.0 + (a - 1.0) * ka_ref[...])
    k_out[0] = kmod
    r = shifted(R_OFF, R_OFF + RWKV_WIDTH)
    r_out[0] = r
    v = shifted(V_OFF, V_OFF + RWKV_WIDTH)
    v_out[0] = v
    bonus_out[0] = _seg_sum(r * kmod * rk_ref[...], e, et) * v
    g_out[0] = _dot(_sigmoid(shifted(GLR_OFF, GLR_OFF + GATE_LORA)), g2_ref[...], HI)
    decay_scale = -math.exp(-0.5)
    wf = w0_ref[0:1] + _dot(jnp.tanh(shifted(WLF_OFF, WLF_OFF + LANES)), w2f_ref[...], HI)
    lwf = decay_scale * _sigmoid(wf)
    cum_f = _dot(_order_mask(tm, False, False, CHUNK).astype(F32), lwf, HI)
    cumf_out[0] = cum_f
    excf_out[0] = cum_f - lwf
    wb = w0_ref[1:2] + _dot(jnp.tanh(shifted(WLB_OFF, WLB_OFF + LANES)), w2b_ref[...], HI)
    lwb = decay_scale * _sigmoid(wb)
    cum_b = _dot(_order_mask(tm, True, False, CHUNK).astype(F32), lwb, HI)
    cumb_out[0] = cum_b
    excb_out[0] = cum_b - lwb


def rwkv_prep(z, mu, w0, w2f, w2b, a0, a2, g2, k_k, k_a, r_k, seg_e, seg_et, tm):
    b, s, _ = z.shape
    nt = s // tm
    hb = tm // 8
    full = lambda arr: pl.BlockSpec(arr.shape, lambda bi, t: (0,) * arr.ndim)
    in_specs = []
    for width, off in ((RWKV_MAIN, R_OFF), (RWKV_SMALL, GLR_OFF)):
        cb = off // width
        in_specs += [pl.BlockSpec((1, tm, width), lambda bi, t, cb=cb: (bi, t, cb)),
                     pl.BlockSpec((1, 8, width), lambda bi, t, cb=cb: (bi, jnp.maximum(t * hb - 1, 0), cb)),
                     pl.BlockSpec((1, 8, width), lambda bi, t, cb=cb: (bi, jnp.minimum((t + 1) * hb, nt * hb - 1), cb))]
    params = [mu, w0, w2f, w2b, a0, a2, g2, k_k, k_a, r_k, seg_e, seg_et]
    in_specs += [full(p) for p in params]
    out_blk = pl.BlockSpec((1, tm, RWKV_WIDTH), lambda bi, t: (bi, t, 0))
    return pl.pallas_call(
        _rwkv_prep_kernel,
        grid=(b, nt),
        in_specs=in_specs,
        out_specs=[out_blk] * 11,
        out_shape=[jax.ShapeDtypeStruct((b, s, RWKV_WIDTH), F32)] * 11,
        compiler_params=_cparams(("parallel", "parallel")),
        name="rwkv_prep",
    )(z, z, z, z, z, z, *params)


def _stack_heads(x):
    lane = _iota2(x.shape, 2)
    return jnp.concatenate([jnp.where(lane < RWKV_HEAD, x, 0.0), jnp.where(lane < RWKV_HEAD, 0.0, x)], axis=1)


def _rwkv_chunks(r, k, v, kk, bb, cum, exc, revs, g_ref):
    nb, c, _ = r.shape
    n2 = 2 * c
    tot = jnp.stack([cum[j, (0 if rev else c - 1)][None] for j, rev in enumerate(revs)], axis=0)
    e_neg = jnp.exp(-cum)
    e_end = jnp.exp(tot - cum)
    al2 = _stack_heads(-kk * jnp.exp(exc))
    rb2 = _stack_heads(r * jnp.exp(cum))
    bt2 = _stack_heads(bb * e_neg)
    kt2 = _stack_heads(k * e_neg)
    be2 = _stack_heads(bb * e_end)
    ke2 = _stack_heads(k * e_end)
    v2 = _stack_heads(v)
    gram = _bdot_nt(jnp.concatenate([al2, rb2], axis=1), jnp.concatenate([bt2, kt2], axis=1))
    strict = jnp.stack([_order_mask(n2, rev, True, c) for rev in revs], axis=0)
    incl = jnp.stack([_order_mask(n2, rev, False, c) for rev in revs], axis=0)
    a_ab = jnp.where(strict, gram[:, :n2, :n2], 0.0)
    a_ak = jnp.where(strict, gram[:, :n2, n2:], 0.0)
    a_rb = jnp.where(incl, gram[:, n2:, :n2], 0.0)
    a_rk = jnp.where(incl, gram[:, n2:, n2:], 0.0)
    t_inv = _neumann_inverse(a_ab)
    wu = _bdot(t_inv, jnp.concatenate([al2, _bdot(a_ak, v2)], axis=2))
    g = g_ref[...]
    proj = _bdot_nt(jnp.concatenate([wu[:, :, :LANES], rb2], axis=1), g)
    u2 = proj[:, :n2] + wu[:, :, LANES:]
    uv = jnp.concatenate([u2, v2], axis=1)
    o2 = proj[:, n2:] + _bdot(jnp.concatenate([a_rb, a_rk], axis=2), uv)
    g_ref[...] = g * jnp.exp(tot) + _bdot_tn(uv, jnp.concatenate([be2, ke2], axis=1))
    return o2[:, :c] + o2[:, c:]


def _rwkv_scan_kernel(*refs, pairs):
    fwd_refs, bwd_refs = refs[0:7], refs[7:14]
    of_ref, ob_ref, g_ref = refs[14:17]

    @pl.when(pl.program_id(2) == 0)
    def _():
        g_ref[...] = jnp.zeros_like(g_ref)

    revs = [False] * pairs + [True] * pairs
    operands = []
    for t in range(7):
        operands.append(jnp.stack([ref[0, :, p * LANES:(p + 1) * LANES]
                                   for ref in (fwd_refs[t], bwd_refs[t]) for p in range(pairs)], axis=0))
    o = _rwkv_chunks(*operands, revs, g_ref)
    for d, o_ref in enumerate((of_ref, ob_ref)):
        for p in range(pairs):
            o_ref[0, :, p * LANES:(p + 1) * LANES] = o[d * pairs + p]


def rwkv_scan(r, k, v, kk, bb, cum_f, exc_f, cum_b, exc_b, pairs):
    b, s, _ = r.shape
    n = s // CHUNK
    w = pairs * LANES
    fwd = pl.BlockSpec((1, CHUNK, w), lambda bi, p, t: (bi, t, p))
    bwd = pl.BlockSpec((1, CHUNK, w), lambda bi, p, t: (bi, n - 1 - t, p))
    return pl.pallas_call(
        functools.partial(_rwkv_scan_kernel, pairs=pairs),
        grid=(b, RWKV_WIDTH // w, n),
        in_specs=[fwd] * 7 + [bwd] * 7,
        out_specs=[fwd, bwd],
        out_shape=[jax.ShapeDtypeStruct((b, s, RWKV_WIDTH), F32)] * 2,
        scratch_shapes=[pltpu.VMEM((2 * pairs, LANES, LANES), F32)],
        compiler_params=_cparams(("parallel", "parallel", "arbitrary")),
        name="rwkv_scan",
    )(r, k, v, kk, bb, cum_f, exc_f, r, k, v, kk, bb, cum_b, exc_b)


def _rwkv_post_kernel(of_ref, ob_ref, bonus_ref, g_ref, lnw_ref, lnb_ref, e_ref, et_ref, y_ref):
    e = e_ref[...]
    et = et_ref[...]
    o = of_ref[0] + ob_ref[0]
    mean = _seg_sum(o, e, et) * (1.0 / RWKV_HEAD)
    cen = o - mean
    var = _seg_sum(cen * cen, e, et) * (1.0 / RWKV_HEAD)
    y = cen * lax.rsqrt(var + RWKV_LN_EPS) * lnw_ref[...] + lnb_ref[...]
    y_ref[0] = ((y + bonus_ref[0]) * g_ref[0]).astype(y_ref.dtype)


def rwkv_post(o_f, o_b, bonus, g, ln_w, ln_b, seg_e, seg_et, tm):
    b, s, w = o_f.shape
    blk = pl.BlockSpec((1, tm, w), lambda bi, t: (bi, t, 0))
    full = lambda arr: pl.BlockSpec(arr.shape, lambda bi, t: (0,) * arr.ndim)
    params = [ln_w.reshape(1, w), ln_b.reshape(1, w), seg_e, seg_et]
    return pl.pallas_call(
        _rwkv_post_kernel,
        grid=(b, s // tm),
        in_specs=[blk] * 4 + [full(p) for p in params],
        out_specs=blk,
        out_shape=jax.ShapeDtypeStruct((b, s, w), BF16),
        compiler_params=_cparams(("parallel", "parallel")),
        name="rwkv_post",
    )(o_f, o_b, bonus, g, *params)


def _gdn_conv_kernel(z_ref, zp_ref, zn_ref, cw_ref, sc_ref, o_ref, *, l2norm):
    t = pl.program_id(1)
    tm = z_ref.shape[1]
    row = _iota2((tm, 1), 0)
    first = t == 0
    last = t == pl.num_programs(1) - 1
    z = z_ref[0]
    p6 = jnp.where(first, 0.0, zp_ref[0, 6:7, :])
    p7 = jnp.where(first, 0.0, zp_ref[0, 7:8, :])
    n0 = jnp.where(last, 0.0, zn_ref[0, 0:1, :])
    n1 = jnp.where(last, 0.0, zn_ref[0, 1:2, :])
    zm1 = jnp.where(row == 0, p7, pltpu.roll(z, 1, axis=0))
    zm2 = jnp.where(row == 0, p6, jnp.where(row == 1, p7, pltpu.roll(z, 2, axis=0)))
    zp1 = jnp.where(row == tm - 1, n0, pltpu.roll(z, tm - 1, axis=0))
    zp2 = jnp.where(row == tm - 1, n1, jnp.where(row == tm - 2, n0, pltpu.roll(z, tm - 2, axis=0)))
    y = (cw_ref[0:1] * zm2 + cw_ref[1:2] * zm1 + cw_ref[2:3] * z + cw_ref[3:4] * zp1 + cw_ref[4:5] * zp2)
    y = _silu(y)
    if l2norm:
        for h in range(y.shape[1] // GDN_DK):
            sl = slice(h * GDN_DK, (h + 1) * GDN_DK)
            yh = y[:, sl]
            ss = jnp.sum(yh * yh, axis=-1, keepdims=True)
            o_ref[0, :, sl] = yh * lax.rsqrt(ss + EPS) * sc_ref[0:1, sl]
    else:
        o_ref[0] = y


def gdn_conv(z, conv_w, scale, col_off, width, l2norm, tm, cw):
    b, s, _ = z.shape
    nt = s // tm
    hb = tm // 8
    cb = col_off // cw
    in_specs = [pl.BlockSpec((1, tm, cw), lambda bi, t, c: (bi, t, cb + c)),
                pl.BlockSpec((1, 8, cw), lambda bi, t, c: (bi, jnp.maximum(t * hb - 1, 0), cb + c)),
                pl.BlockSpec((1, 8, cw), lambda bi, t, c: (bi, jnp.minimum((t + 1) * hb, nt * hb - 1), cb + c)),
                pl.BlockSpec((conv_w.shape[0], cw), lambda bi, t, c: (0, cb + c)),
                pl.BlockSpec((1, cw), lambda bi, t, c: (0, c))]
    return pl.pallas_call(
        functools.partial(_gdn_conv_kernel, l2norm=l2norm),
        grid=(b, nt, width // cw),
        in_specs=in_specs,
        out_specs=pl.BlockSpec((1, tm, cw), lambda bi, t, c: (bi, t, c)),
        out_shape=jax.ShapeDtypeStruct((b, s, width), F32),
        compiler_params=_cparams(("parallel", "parallel", "parallel")),
        name="gdn_conv_norm" if l2norm else "gdn_conv",
    )(z, z, z, conv_w, scale)


def _gdn_gates_kernel(zs_ref, na_ref, dtb_ref, o_ref):
    zs = zs_ref[0]
    tm = zs.shape[0]
    gg = na_ref[...] * _softplus(zs + dtb_ref[...])
    gam_f = _dot(_order_mask(tm, False, False, CHUNK).astype(F32), gg, HI)
    gam_b = _dot(_order_mask(tm, True, False, CHUNK).astype(F32), gg, HI)
    lane = _iota2(zs.shape, 1)
    o_ref[0] = jnp.where(lane < 2 * GDN_VHEADS, _sigmoid(zs), jnp.where(lane < 3 * GDN_VHEADS, gam_f, gam_b))


def gdn_gates(zs, neg_a, dtb, tm):
    b, s, w = zs.shape
    blk = pl.BlockSpec((1, tm, w), lambda bi, t: (bi, t, 0))
    row = pl.BlockSpec((1, w), lambda bi, t: (0, 0))
    return pl.pallas_call(
        _gdn_gates_kernel,
        grid=(b, s // tm),
        in_specs=[blk, row, row],
        out_specs=blk,
        out_shape=jax.ShapeDtypeStruct((b, s, w), F32),
        compiler_params=_cparams(("parallel", "parallel")),
        name="gdn_gates",
    )(zs, neg_a, dtb)


def _gdn_chunks(q, k, v2, beta2, gam2, revs, s_ref):
    nb, c, _ = q.shape
    n2 = 2 * c
    per_batch = lambda fn: jnp.stack([fn(r) for r in revs], axis=0)
    incl = per_batch(lambda r: _order_mask(n2, r, False, c))
    strict = per_batch(lambda r: _order_mask(n2, r, True, c))
    gam_c = jnp.broadcast_to(gam2, (nb, n2, n2))
    gam_r = jnp.swapaxes(gam_c, 1, 2)
    tot2 = jnp.stack([jnp.concatenate(
        [jnp.broadcast_to(gam2[j, h * c + (0 if r else c - 1)][None], (c, 1)) for h in range(2)], axis=0)
        for j, r in enumerate(revs)], axis=0)
    diff = gam_c - gam_r
    dec_s = jnp.where(strict, jnp.exp(jnp.where(strict, diff, 0.0)), 0.0)
    dec_i = jnp.where(incl, jnp.exp(jnp.where(incl, diff, 0.0)), 0.0)
    k2 = jnp.concatenate([k, k], axis=1)
    q2 = jnp.concatenate([q, q], axis=1)
    gram = _bdot_nt(jnp.concatenate([k2, q2], axis=1), k2)
    a_mat = gram[:, :n2] * beta2 * dec_s
    qk = gram[:, n2:] * dec_i
    t_inv = _neumann_inverse(-a_mat)
    e_gam = jnp.exp(gam2)
    uw = _bdot(t_inv, jnp.concatenate([v2 * beta2, k2 * (beta2 * e_gam)], axis=2))
    u2 = uw[:, :, :GDN_DV]
    w2 = uw[:, :, GDN_DV:]
    qd2 = q2 * e_gam
    ke2 = k2 * jnp.exp(tot2 - gam2)
    dl2 = jnp.exp(tot2)
    ws, qs = [], []
    for e in range(2):
        rows = slice(e * c, (e + 1) * c)
        both = _bdot(jnp.concatenate([w2[:, rows], qd2[:, rows]], axis=1), s_ref[e])
        ws.append(both[:, :c])
        qs.append(both[:, c:])
    vnew2 = u2 - jnp.concatenate(ws, axis=1)
    o2 = jnp.concatenate(qs, axis=1) + _bdot(qk, vnew2)
    for e in range(2):
        rows = slice(e * c, (e + 1) * c)
        s_ref[e] = s_ref[e] * dl2[:, e * c:e * c + 1, :] + _bdot_tn(ke2[:, rows], vnew2[:, rows])
    return o2


def _gdn_scan_kernel(qf, kf, vf, gf, qb, kb, vb, gb, of_ref, ob_ref, s_ref, *, heads):
    @pl.when(pl.program_id(2) == 0)
    def _():
        s_ref[...] = jnp.zeros_like(s_ref)

    c = qf.shape[1]
    lane = _iota2((c, LANES), 1)

    def column(gates, idx):
        return jnp.sum(jnp.where(lane == idx, gates, 0.0), axis=1, keepdims=True)

    qs, ks, vs, betas, gams, revs = [], [], [], [], [], []
    for d, (q_ref, k_ref, v_ref, g_ref) in enumerate(((qf, kf, vf, gf), (qb, kb, vb, gb))):
        gates = g_ref[0]
        for i in range(heads):
            vh = 2 * (pl.program_id(1) * heads + i)
            ksl = slice(i * GDN_DK, (i + 1) * GDN_DK)
            qs.append(q_ref[0, :, ksl])
            ks.append(k_ref[0, :, ksl])
            vs.append(jnp.concatenate([v_ref[0, :, (2 * i + e) * GDN_DV:(2 * i + e + 1) * GDN_DV] for e in range(2)],
                                      axis=0))
            betas.append(jnp.concatenate([column(gates, d * GDN_VHEADS + vh + e) for e in range(2)], axis=0))
            gams.append(jnp.concatenate([column(gates, (2 + d) * GDN_VHEADS + vh + e) for e in range(2)], axis=0))
            revs.append(d == 1)
    o2 = _gdn_chunks(jnp.stack(qs), jnp.stack(ks), jnp.stack(vs), jnp.stack(betas), jnp.stack(gams), revs, s_ref)
    for d, o_ref in enumerate((of_ref, ob_ref)):
        for i in range(heads):
            j = d * heads + i
            for e in range(2):
                o_ref[0, :, (2 * i + e) * GDN_DV:(2 * i + e + 1) * GDN_DV] = o2[j, e * c:(e + 1) * c]


def gdn_scan(qk, v, gates, heads):
    b, s, _ = v.shape
    n = s // CHUNK
    c = CHUNK
    ng = GDN_KHEADS // heads

    def spec(w, off, rev):
        if rev:
            return pl.BlockSpec((1, c, w), lambda bi, h, t: (bi, n - 1 - t, off + h))
        return pl.BlockSpec((1, c, w), lambda bi, h, t: (bi, t, off + h))

    def gspec(rev):
        if rev:
            return pl.BlockSpec((1, c, LANES), lambda bi, h, t: (bi, n - 1 - t, 0))
        return pl.BlockSpec((1, c, LANES), lambda bi, h, t: (bi, t, 0))

    kw = heads * GDN_DK
    vw = heads * 2 * GDN_DV
    in_specs = [spec(kw, 0, False), spec(kw, ng, False), spec(vw, 0, False), gspec(False),
                spec(kw, 0, True), spec(kw, ng, True), spec(vw, 0, True), gspec(True)]
    return pl.pallas_call(
        functools.partial(_gdn_scan_kernel, heads=heads),
        grid=(b, ng, n),
        in_specs=in_specs,
        out_specs=[spec(vw, 0, False), spec(vw, 0, True)],
        out_shape=[jax.ShapeDtypeStruct((b, s, GDN_VW), F32)] * 2,
        scratch_shapes=[pltpu.VMEM((2, 2 * heads, GDN_DK, GDN_DV), F32)],
        compiler_params=_cparams(("parallel", "parallel", "arbitrary")),
        name="gdn_scan",
    )(qk, qk, v, gates, qk, qk, v, gates)


def _gdn_post_kernel(of_ref, ob_ref, gate_ref, nw_ref, y_ref):
    nw = nw_ref[...]
    for h in range(of_ref.shape[2] // GDN_DV):
        sl = slice(h * GDN_DV, (h + 1) * GDN_DV)
        o = of_ref[0, :, sl] + ob_ref[0, :, sl]
        ms = jnp.mean(o * o, axis=-1, keepdims=True)
        y = o * lax.rsqrt(ms + EPS) * nw
        y_ref[0, :, sl] = (y * _silu(gate_ref[0, :, sl])).astype(y_ref.dtype)


def gdn_post(o_f, o_b, z, gate_off, norm_w, tm, cw):
    b, s, w = o_f.shape
    blk = pl.BlockSpec((1, tm, cw), lambda bi, t, c: (bi, t, c))
    gb = gate_off // cw
    return pl.pallas_call(
        _gdn_post_kernel,
        grid=(b, s // tm, w // cw),
        in_specs=[blk, blk, pl.BlockSpec((1, tm, cw), lambda bi, t, c: (bi, t, gb + c)),
                  pl.BlockSpec((1, GDN_DV), lambda bi, t, c: (0, 0))],
        out_specs=blk,
        out_shape=jax.ShapeDtypeStruct((b, s, w), BF16),
        compiler_params=_cparams(("parallel", "parallel", "parallel")),
        name="gdn_post",
    )(o_f, o_b, z, norm_w.reshape(1, GDN_DV))


def _moe_router_kernel(x_ref, g_ref, wg_ref, bg_ref, wr_ref, br_ref, h_ref, gates_ref):
    x = x_ref[...]
    ms = jnp.mean(x * x, axis=-1, keepdims=True)
    h = x * lax.rsqrt(ms + EPS) * g_ref[...]
    h_ref[...] = h.astype(h_ref.dtype)
    lane_i = _iota2((x.shape[0], LANES), 1)
    lane = lane_i.astype(F32)
    lane_grp = lax.shift_right_logical(lane_i, 3).astype(F32)
    neg = -jnp.inf
    gl = jnp.where(lane_i < N_GROUPS, _dot(h, wg_ref[...], HI) + bg_ref[...], neg)
    gmax = jnp.max(gl, axis=-1, keepdims=True)
    gidx = jnp.min(jnp.where(gl == gmax, lane, float(LANES)), axis=-1, keepdims=True)
    grp_w = 1.0 / jnp.sum(jnp.exp(gl - gmax), axis=-1, keepdims=True)
    el = _dot(h, wr_ref[...], HI) + br_ref[...]
    sel = jnp.where((lane_i < N_EXPERTS) & (lane_grp == gidx), el, neg)
    m1 = jnp.max(sel, axis=-1, keepdims=True)
    i1 = jnp.min(jnp.where(sel == m1, lane, float(LANES)), axis=-1, keepdims=True)
    sel2 = jnp.where(lane == i1, neg, sel)
    m2 = jnp.max(sel2, axis=-1, keepdims=True)
    i2 = jnp.min(jnp.where(sel2 == m2, lane, float(LANES)), axis=-1, keepdims=True)
    e2 = jnp.exp(m2 - m1)
    w1 = grp_w / (1.0 + e2)
    w2 = grp_w * e2 / (1.0 + e2)
    gates_ref[...] = jnp.where(lane == i1, w1, 0.0) + jnp.where(lane == i2, w2, 0.0)


def moe_router(x, g, w_group, b_group, w_router, b_router, tm):
    m, d = x.shape
    tm = min(tm, m)
    pad = lambda w: jnp.pad(w, ((0, 0), (0, LANES - w.shape[1])))
    full = lambda arr: pl.BlockSpec(arr.shape, lambda i: (0, 0))
    params = [g.reshape(1, d), pad(w_group), pad(b_group.reshape(1, -1)), pad(w_router), pad(b_router.reshape(1, -1))]
    return pl.pallas_call(
        _moe_router_kernel,
        grid=(m // tm,),
        in_specs=[pl.BlockSpec((tm, d), lambda i: (i, 0))] + [full(p) for p in params],
        out_specs=[pl.BlockSpec((tm, d), lambda i: (i, 0)), pl.BlockSpec((tm, LANES), lambda i: (i, 0))],
        out_shape=[jax.ShapeDtypeStruct((m, d), BF16), jax.ShapeDtypeStruct((m, LANES), F32)],
        compiler_params=_cparams(("parallel",)),
        name="moe_router",
    )(x, *params)


def _moe_ffn_kernel(h_ref, gates_ref, x_ref, wg_ref, wu_ref, wd_ref, o_ref, acc_ref):
    e = pl.program_id(1)

    @pl.when(e == 0)
    def _():
        acc_ref[...] = x_ref[...]

    h = h_ref[...]
    lane = _iota2(gates_ref.shape, 1)
    gcol = jnp.sum(jnp.where(lane == e, gates_ref[...], 0.0), axis=-1, keepdims=True)
    hid = _silu(_dot(h, wg_ref[0])) * _dot(h, wu_ref[0]) * gcol
    acc_ref[...] += _dot(hid.astype(BF16), wd_ref[0])

    @pl.when(e == pl.num_programs(1) - 1)
    def _():
        o_ref[...] = acc_ref[...]


def moe_ffn(h, gates, x, w_gate, w_up, w_down, tm):
    m, d = x.shape
    tm = min(tm, m)
    ne, _, ff = w_gate.shape
    return pl.pallas_call(
        _moe_ffn_kernel,
        grid=(m // tm, ne),
        in_specs=[pl.BlockSpec((tm, d), lambda i, e: (i, 0)),
                  pl.BlockSpec((tm, LANES), lambda i, e: (i, 0)),
                  pl.BlockSpec((tm, d), lambda i, e: (i, 0)),
                  pl.BlockSpec((1, d, ff), lambda i, e: (e, 0, 0)),
                  pl.BlockSpec((1, d, ff), lambda i, e: (e, 0, 0)),
                  pl.BlockSpec((1, ff, d), lambda i, e: (e, 0, 0))],
        out_specs=pl.BlockSpec((tm, d), lambda i, e: (i, 0)),
        out_shape=jax.ShapeDtypeStruct((m, d), F32),
        scratch_shapes=[pltpu.VMEM((tm, d), F32)],
        compiler_params=_cparams(("parallel", "arbitrary")),
        name="moe_ffn",
    )(h, gates, x, w_gate, w_up, w_down)


def _pad_rows(w, rows):
    return jnp.pad(w, ((0, rows - w.shape[0]), (0, 0)))


def _even_layer(x, norm_g, w_in, gla_w_alpha, gla_b_alpha, gla_norm, mu, w0, w2, a0, a2, g2,
                k_k, k_a, r_k, ln_w, ln_b, w_out):
    b, s, d = x.shape
    gla_cols, rw = w_in[:, :3104], w_in[:, 3104:]
    mu_g = lambda lo, hi, width: jnp.pad(mu[:, lo:hi], ((0, 0), (0, width - (hi - lo))))
    pad_c = lambda w, width: jnp.pad(w, ((0, 0), (0, width - w.shape[1])))
    w_cat = jnp.concatenate([
        gla_cols[:, 0:3072],
        rw[:, 0:3072], rw[:, 3360:3616],
        pad_c(rw[:, 3072:3168], LANES), pad_c(rw[:, 3168:3264], LANES), pad_c(rw[:, 3264:3360], LANES),
        pad_c(gla_cols[:, 3072:3104], LANES)], axis=1).astype(BF16)
    mu_cat = jnp.concatenate([
        mu[:, 0:3072], mu[:, 3360:3616], mu_g(3072, 3168, LANES), mu_g(3168, 3264, LANES),
        mu_g(3264, 3360, LANES)], axis=1)
    z = norm_matmul(x.reshape(b * s, d), norm_g, w_cat, 512, 768).reshape(b, s, EVEN_PAD)

    wa = jnp.stack([_pad_rows(gla_w_alpha[0], LANES),
                    jnp.pad(gla_w_alpha[1], ((GLA_LOWRANK, LANES - 2 * GLA_LOWRANK), (0, 0)))])
    gla_f, gla_b = gla_scan(z, wa, gla_b_alpha)
    y_gla = gla_post(gla_f, gla_b, z, gla_norm, 256)

    head_of_lane = jnp.arange(RWKV_WIDTH) // RWKV_HEAD
    seg_e = (head_of_lane[:, None] == jnp.arange(LANES)[None, :]).astype(F32)
    seg_et = seg_e.T
    row = lambda p: p.reshape(1, RWKV_WIDTH)
    r, k, v, kk, bb, cum_f, exc_f, cum_b, exc_b, g, bonus = rwkv_prep(
        z, mu_cat, w0, _pad_rows(w2[0], LANES), _pad_rows(w2[1], LANES), row(a0), _pad_rows(a2, LANES), g2,
        row(k_k), row(k_a), row(r_k), seg_e, seg_et, 128)
    rw_f, rw_b = rwkv_scan(r, k, v, kk, bb, cum_f, exc_f, cum_b, exc_b, RWKV_PAIRS_PER_STEP)
    y_rwkv = rwkv_post(rw_f, rw_b, bonus, g, ln_w, ln_b, seg_e, seg_et, 256)

    w_out = w_out.astype(BF16)
    out = proj_residual([y_gla.reshape(b * s, GLA_WIDTH), y_rwkv.reshape(b * s, RWKV_WIDTH)],
                        [w_out[:GLA_WIDTH], w_out[GLA_WIDTH:]], x.reshape(b * s, d), 512, 1024)
    return out.reshape(b, s, d)


def _odd_layer(x, norm_g, w_in, conv_w, a_log, dt_bias, norm_w, w_out):
    b, s, d = x.shape
    x2 = x.reshape(b * s, d)
    main = GDN_QKV + GDN_VW
    z = norm_matmul(x2, norm_g, w_in[:, :main].astype(BF16), 512, 1024).reshape(b, s, main)
    zs = norm_matmul(x2, norm_g, w_in[:, main:].astype(BF16), 512, LANES).reshape(b, s, LANES)
    scale = jnp.concatenate([jnp.full((1, GDN_KW), GDN_DK ** -0.5, F32), jnp.ones((1, GDN_KW), F32)], axis=1)
    qk = gdn_conv(z, conv_w, scale, 0, 2 * GDN_KW, True, 256, 1024)
    v = gdn_conv(z, conv_w, scale, 2 * GDN_KW, GDN_VW, False, 256, 1024)
    zero = jnp.zeros((2 * GDN_VHEADS,), F32)
    neg_a = jnp.concatenate([zero, -jnp.exp(a_log.reshape(-1))]).reshape(1, LANES)
    dtb = jnp.concatenate([zero, dt_bias.reshape(-1)]).reshape(1, LANES)
    gates = gdn_gates(zs, neg_a, dtb, 256)
    o_f, o_b = gdn_scan(qk, v, gates, GDN_HEADS_PER_STEP)
    y = gdn_post(o_f, o_b, z, GDN_QKV, norm_w, 256, 1024)
    out = proj_residual([y.reshape(b * s, GDN_VW)], [w_out.astype(BF16)], x2, 512, 1024)
    return out.reshape(b, s, d)


def _moe_layer(x, norm_g, w_group, b_group, w_router, b_router, w_gate, w_up, w_down):
    b, s, d = x.shape
    x2 = x.reshape(b * s, d)
    h, gates = moe_router(x2, norm_g, w_group, b_group, w_router, b_router, 256)
    out = moe_ffn(h, gates, x2, w_gate.astype(BF16), w_up.astype(BF16), w_down.astype(BF16), 512)
    return out.reshape(b, s, d)


def kernel(x, norm_mix, norm_ffn, norm_final, ev_w_in, ev_gla_w_alpha, ev_gla_b_alpha, ev_gla_norm, ev_rwkv_mu, ev_rwkv_w0, ev_rwkv_w2, ev_rwkv_a0, ev_rwkv_a2, ev_rwkv_g2, ev_rwkv_k_k, ev_rwkv_k_a, ev_rwkv_r_k, ev_rwkv_ln_w, ev_rwkv_ln_b, ev_w_out, od_w_in, od_conv, od_a_log, od_dt_bias, od_norm, od_w_out, moe_w_group, moe_b_group, moe_w_router, moe_b_router, moe_w_gate, moe_w_up, moe_w_down):
    depth = norm_mix.shape[0]
    for i in range(depth):
        j = i // 2
        if i % 2 == 0:
            x = _even_layer(x, norm_mix[i], ev_w_in[j], ev_gla_w_alpha[j], ev_gla_b_alpha[j], ev_gla_norm[j],
                            ev_rwkv_mu[j], ev_rwkv_w0[j], ev_rwkv_w2[j], ev_rwkv_a0[j], ev_rwkv_a2[j],
                            ev_rwkv_g2[j], ev_rwkv_k_k[j], ev_rwkv_k_a[j], ev_rwkv_r_k[j],
                            ev_rwkv_ln_w[j], ev_rwkv_ln_b[j], ev_w_out[j])
        else:
            x = _odd_layer(x, norm_mix[i], od_w_in[j], od_conv[j], od_a_log[j], od_dt_bias[j],
                           od_norm[j], od_w_out[j])
        x = _moe_layer(x, norm_ffn[i], moe_w_group[i], moe_b_group[i], moe_w_router[i], moe_b_router[i],
                       moe_w_gate[i], moe_w_up[i], moe_w_down[i])
    b, s, d = x.shape
    return final_norm(x.reshape(b * s, d), norm_final, 512).reshape(b, s, d)
```

```python
import functools
import math

import jax
import jax.numpy as jnp
from jax import lax
from jax.experimental import pallas as pl
from jax.experimental.pallas import tpu as pltpu

F32 = jnp.float32
BF16 = jnp.bfloat16
HI = lax.Precision.HIGHEST

EPS = 1e-6
CHUNK = 64
LANES = 128
VMEM_LIMIT = 56 * 1024 * 1024

D_MODEL = 2048
GLA_HEADS = 4
GLA_DK = 128
GLA_DV = 256
GLA_KW = 512
GLA_WIDTH = 1024
GLA_LOWRANK = 16
GLA_TAU = 16.0
RWKV_HEAD = 64
RWKV_WIDTH = 1024
RWKV_HEADS = 16
DECAY_LORA = 96
ICLR_LORA = 96
GATE_LORA = 256
RWKV_LN_EPS = 64e-5
GDN_DK = 128
GDN_DV = 128
GDN_KHEADS = 16
GDN_VHEADS = 32
GDN_KW = 2048
GDN_VW = 4096
GDN_QKV = 8192
RWKV_PAIRS_PER_STEP = 4
GDN_HEADS_PER_STEP = 4
MOE_ROW_TILE = 256
N_GROUPS = 4
EXPERTS_PER_GROUP = 8
N_EXPERTS = 32
EXPERT_FF = 512

GQ_OFF, GK_OFF, GV_OFF, GG_OFF = 0, 512, 1024, 2048
R_OFF, K_OFF, V_OFF = 3072, 4096, 5120
GLR_OFF, WLF_OFF, WLB_OFF, ALR_OFF, AL_OFF = 6144, 6400, 6528, 6656, 6784
RWKV_MAIN = 3072
RWKV_SMALL = 768
EVEN_PAD = 6912


def _cparams(sem):
    return pltpu.CompilerParams(dimension_semantics=sem, vmem_limit_bytes=VMEM_LIMIT)


def _mm(a, b, ca, cb, precision):
    if a.ndim == 3:
        dims = (((ca + 1,), (cb + 1,)), ((0,), (0,)))
    else:
        dims = (((ca,), (cb,)), ((), ()))
    return lax.dot_general(a, b, dims, preferred_element_type=F32, precision=precision)


def _dot(a, b, precision=None):
    return _mm(a, b, 1, 0, precision)


def _dot_nt(a, b, precision=None):
    return _mm(a, b, 1, 1, precision)


def _dot_tn(a, b, precision=None):
    return _mm(a, b, 0, 0, precision)


def _bdot(a, b):
    return _dot(a.astype(BF16), b.astype(BF16))


def _bdot_nt(a, b):
    return _dot_nt(a.astype(BF16), b.astype(BF16))


def _bdot_tn(a, b):
    return _dot_tn(a.astype(BF16), b.astype(BF16))


def _sigmoid(x):
    return 1.0 / (1.0 + jnp.exp(-x))


def _silu(x):
    return x * _sigmoid(x)


def _softplus(x):
    return jnp.maximum(x, 0.0) + jnp.log(1.0 + jnp.exp(-jnp.abs(x)))


def _iota2(shape, dim):
    return lax.broadcasted_iota(jnp.int32, shape, dim)


def _order_mask(n, rev, strict, block):
    i = _iota2((n, n), 0)
    j = _iota2((n, n), 1)
    if rev:
        m = (j > i) if strict else (j >= i)
    else:
        m = (j < i) if strict else (j <= i)
    if block < n:
        sh = block.bit_length() - 1
        m = m & (lax.shift_right_logical(i, sh) == lax.shift_right_logical(j, sh))
    return m


def _neumann_inverse(nmat):
    n = nmat.shape[-1]
    eye = (_iota2((n, n), 0) == _iota2((n, n), 1)).astype(F32)
    t = eye + nmat
    nb = nmat.astype(BF16)
    p = _dot(nb, nb)
    for step in range(5):
        pb = p.astype(BF16)
        if step < 4:
            both = _dot(pb, jnp.concatenate([pb, t.astype(BF16)], axis=-1))
            p = both[..., :n]
            t = t + both[..., n:]
        else:
            t = t + _dot(pb, t.astype(BF16))
    return t


def _norm_matmul_kernel(x_ref, g_ref, w_ref, o_ref, h_ref):
    @pl.when(pl.program_id(1) == 0)
    def _():
        x = x_ref[...]
        ms = jnp.mean(x * x, axis=-1, keepdims=True)
        h_ref[...] = (x * lax.rsqrt(ms + EPS) * g_ref[...]).astype(BF16)

    o_ref[...] = _dot(h_ref[...], w_ref[...])


def norm_matmul(x, g, w, tm, tn):
    m, d = x.shape
    tm = min(tm, m)
    n = w.shape[1]
    return pl.pallas_call(
        _norm_matmul_kernel,
        grid=(m // tm, n // tn),
        in_specs=[pl.BlockSpec((tm, d), lambda i, j: (i, 0)),
                  pl.BlockSpec((1, d), lambda i, j: (0, 0)),
                  pl.BlockSpec((d, tn), lambda i, j: (0, j))],
        out_specs=pl.BlockSpec((tm, tn), lambda i, j: (i, j)),
        out_shape=jax.ShapeDtypeStruct((m, n), F32),
        scratch_shapes=[pltpu.VMEM((tm, d), BF16)],
        compiler_params=_cparams(("parallel", "arbitrary")),
        name="norm_matmul",
    )(x, g.reshape(1, d), w)


def _proj_residual_kernel(*refs, n_lhs):
    x_ref = refs[2 * n_lhs]
    o_ref = refs[2 * n_lhs + 1]
    acc = x_ref[...]
    for t in range(n_lhs):
        acc = acc + _dot(refs[t][...], refs[n_lhs + t][...])
    o_ref[...] = acc


def proj_residual(ys, ws, x, tm, tn):
    m, n = x.shape
    tm = min(tm, m)
    n_lhs = len(ys)
    in_specs = [pl.BlockSpec((tm, y.shape[1]), lambda i, j: (i, 0)) for y in ys]
    in_specs += [pl.BlockSpec((w.shape[0], tn), lambda i, j: (0, j)) for w in ws]
    in_specs += [pl.BlockSpec((tm, tn), lambda i, j: (i, j))]
    return pl.pallas_call(
        functools.partial(_proj_residual_kernel, n_lhs=n_lhs),
        grid=(m // tm, n // tn),
        in_specs=in_specs,
        out_specs=pl.BlockSpec((tm, tn), lambda i, j: (i, j)),
        out_shape=jax.ShapeDtypeStruct((m, n), F32),
        compiler_params=_cparams(("parallel", "arbitrary")),
        name="proj_residual",
    )(*ys, *ws, x)


def _final_norm_kernel(x_ref, g_ref, o_ref):
    x = x_ref[...]
    ms = jnp.mean(x * x, axis=-1, keepdims=True)
    o_ref[...] = x * lax.rsqrt(ms + EPS) * g_ref[...]


def final_norm(x, g, tm):
    m, d = x.shape
    tm = min(tm, m)
    return pl.pallas_call(
        _final_norm_kernel,
        grid=(m // tm,),
        in_specs=[pl.BlockSpec((tm, d), lambda i: (i, 0)), pl.BlockSpec((1, d), lambda i: (0, 0))],
        out_specs=pl.BlockSpec((tm, d), lambda i: (i, 0)),
        out_shape=jax.ShapeDtypeStruct((m, d), F32),
        compiler_params=_cparams(("parallel",)),
        name="final_norm",
    )(x, g.reshape(1, d))


def _gla_chunks(q, k, v, cum, revs, s_ref):
    c = q.shape[1]
    tot = jnp.stack([cum[j, (0 if rev else c - 1)][None] for j, rev in enumerate(revs)], axis=0)
    incl = jnp.stack([_order_mask(c, rev, False, c) for rev in revs], axis=0)
    q_dec = q * ((GLA_DK ** -0.5) * jnp.exp(cum))
    k_dec = k * jnp.exp(-cum)
    k_end = k * jnp.exp(tot - cum)
    scores = jnp.where(incl, _bdot_nt(q_dec, k_dec), 0.0)
    state = s_ref[...]
    o = _bdot(scores, v) + _bdot_nt(q_dec, state)
    s_ref[...] = jnp.exp(tot) * state + _bdot_tn(v, k_end)
    return o


def _gla_kernel(qf, kf, vf, alf, qb, kb, vb, alb, wa_ref, ba_ref, of_ref, ob_ref, s_ref):
    @pl.when(pl.program_id(1) == 0)
    def _():
        s_ref[...] = jnp.zeros_like(s_ref)

    c = qf.shape[1]
    qs, ks, vs, cums, revs = [], [], [], [], []
    for d, (q_ref, k_ref, v_ref, al_ref) in enumerate(((qf, kf, vf, alf), (qb, kb, vb, alb))):
        pre = _dot(al_ref[0], wa_ref[d], HI) + ba_ref[d:d + 1]
        log_a = (jnp.minimum(pre, 0.0) - jnp.log(1.0 + jnp.exp(-jnp.abs(pre)))) * (1.0 / GLA_TAU)
        cum = _dot(_order_mask(c, d == 1, False, c).astype(F32), log_a, HI)
        for h in range(GLA_HEADS):
            ksl = slice(h * GLA_DK, (h + 1) * GLA_DK)
            qs.append(q_ref[0, :, ksl])
            ks.append(k_ref[0, :, ksl])
            vs.append(v_ref[0, :, h * GLA_DV:(h + 1) * GLA_DV])
            cums.append(cum[:, ksl])
            revs.append(d == 1)
    o = _gla_chunks(jnp.stack(qs), jnp.stack(ks), jnp.stack(vs), jnp.stack(cums), revs, s_ref)
    for d, o_ref in enumerate((of_ref, ob_ref)):
        for h in range(GLA_HEADS):
            o_ref[0, :, h * GLA_DV:(h + 1) * GLA_DV] = o[d * GLA_HEADS + h]


def gla_scan(z, wa, ba):
    b, s, _ = z.shape
    n = s // CHUNK
    c = CHUNK

    def fwd(off, w):
        return pl.BlockSpec((1, c, w), lambda bi, t: (bi, t, off // w))

    def bwd(off, w):
        return pl.BlockSpec((1, c, w), lambda bi, t: (bi, n - 1 - t, off // w))

    in_specs = [fwd(GQ_OFF, GLA_KW), fwd(GK_OFF, GLA_KW), fwd(GV_OFF, GLA_WIDTH), fwd(AL_OFF, LANES),
                bwd(GQ_OFF, GLA_KW), bwd(GK_OFF, GLA_KW), bwd(GV_OFF, GLA_WIDTH), bwd(AL_OFF, LANES),
                pl.BlockSpec(wa.shape, lambda bi, t: (0, 0, 0)), pl.BlockSpec(ba.shape, lambda bi, t: (0, 0))]
    out_specs = [pl.BlockSpec((1, c, GLA_WIDTH), lambda bi, t: (bi, t, 0)),
                 pl.BlockSpec((1, c, GLA_WIDTH), lambda bi, t: (bi, n - 1 - t, 0))]
    return pl.pallas_call(
        _gla_kernel,
        grid=(b, n),
        in_specs=in_specs,
        out_specs=out_specs,
        out_shape=[jax.ShapeDtypeStruct((b, s, GLA_WIDTH), F32)] * 2,
        scratch_shapes=[pltpu.VMEM((2 * GLA_HEADS, GLA_DV, GLA_DK), F32)],
        compiler_params=_cparams(("parallel", "arbitrary")),
        name="gla_scan",
    )(z, z, z, z, z, z, z, z, wa, ba)


def _gla_post_kernel(of_ref, ob_ref, gate_ref, nw_ref, y_ref):
    o = of_ref[0] + ob_ref[0]
    ms = jnp.mean(o * o, axis=-1, keepdims=True)
    y = o * lax.rsqrt(ms + EPS) * nw_ref[...]
    y_ref[0] = (y * _silu(gate_ref[0])).astype(y_ref.dtype)


def gla_post(o_f, o_b, z, norm_w, tm):
    b, s, w = o_f.shape
    blk = pl.BlockSpec((1, tm, GLA_DV), lambda bi, t, h: (bi, t, h))
    return pl.pallas_call(
        _gla_post_kernel,
        grid=(b, s // tm, GLA_HEADS),
        in_specs=[blk, blk, pl.BlockSpec((1, tm, GLA_DV), lambda bi, t, h: (bi, t, GG_OFF // GLA_DV + h)),
                  pl.BlockSpec((1, GLA_DV), lambda bi, t, h: (0, 0))],
        out_specs=blk,
        out_shape=jax.ShapeDtypeStruct((b, s, w), BF16),
        compiler_params=_cparams(("parallel", "parallel", "parallel")),
        name="gla_post",
    )(o_f, o_b, z, norm_w.reshape(1, GLA_DV))


def _seg_sum(x, e, et):
    return _dot(_dot(x, e, HI), et, HI)


def _rwkv_prep_kernel(z_ref, zp_ref, zn_ref, y_ref, yp_ref, yn_ref, mu_ref, w0_ref, w2f_ref, w2b_ref, a0_ref, a2_ref, g2_ref,
                      kk_ref, ka_ref, rk_ref, e_ref, et_ref,
                      r_out, k_out, v_out, kk_out, b_out, cumf_out, excf_out, cumb_out, excb_out, g_out, bonus_out):
    t = pl.program_id(1)
    tm = z_ref.shape[1]
    row = _iota2((tm, 1), 0)
    first = t == 0
    last = t == pl.num_programs(1) - 1

    def shifted(lo, hi):
        cur, prv, nxt, base = (z_ref, zp_ref, zn_ref, R_OFF) if lo < GLR_OFF else (y_ref, yp_ref, yn_ref, GLR_OFF)
        z = cur[0, :, lo - base:hi - base]
        prev_row = jnp.where(first, 0.0, prv[0, 7:8, lo - base:hi - base])
        next_row = jnp.where(last, 0.0, nxt[0, 0:1, lo - base:hi - base])
        zprev = jnp.where(row == 0, prev_row, pltpu.roll(z, 1, axis=0))
        znext = jnp.where(row == tm - 1, next_row, pltpu.roll(z, tm - 1, axis=0))
        mu = mu_ref[:, lo - R_OFF:hi - R_OFF]
        return z + mu[0:1] * (zprev - z) + mu[1:2] * (znext - z)

    e = e_ref[...]
    et = et_ref[...]
    a = _sigmoid(a0_ref[...] + _dot(shifted(ALR_OFF, ALR_OFF + LANES), a2_ref[...], HI))
    k = shifted(K_OFF, K_OFF + RWKV_WIDTH)
    kk_raw = k * kk_ref[...]
    kk = kk_raw * lax.rsqrt(_seg_sum(kk_raw * kk_raw, e, et) + EPS)
    kk_out[0] = kk
    b_out[0] = kk * a
    kmod = k * (1.0 + (a - 1.0) * ka_ref[...])
    k_out[0] = kmod
    r = shifted(R_OFF, R_OFF + RWKV_WIDTH)
    r_out[0] = r
    v = shifted(V_OFF, V_OFF + RWKV_WIDTH)
    v_out[0] = v
    bonus_out[0] = _seg_sum(r * kmod * rk_ref[...], e, et) * v
    g_out[0] = _dot(_sigmoid(shifted(GLR_OFF, GLR_OFF + GATE_LORA)), g2_ref[...], HI)
    decay_scale = -math.exp(-0.5)
    wf = w0_ref[0:1] + _dot(jnp.tanh(shifted(WLF_OFF, WLF_OFF + LANES)), w2f_ref[...], HI)
    lwf = decay_scale * _sigmoid(wf)
    cum_f = _dot(_order_mask(tm, False, False, CHUNK).astype(F32), lwf, HI)
    cumf_out[0] = cum_f
    excf_out[0] = cum_f - lwf
    wb = w0_ref[1:2] + _dot(jnp.tanh(shifted(WLB_OFF, WLB_OFF + LANES)), w2b_ref[...], HI)
    lwb = decay_scale * _sigmoid(wb)
    cum_b = _dot(_order_mask(tm, True, False, CHUNK).astype(F32), lwb, HI)
    cumb_out[0] = cum_b
    excb_out[0] = cum_b - lwb


def rwkv_prep(z, mu, w0, w2f, w2b, a0, a2, g2, k_k, k_a, r_k, seg_e, seg_et, tm):
    b, s, _ = z.shape
    nt = s // tm
    hb = tm // 8
    full = lambda arr: pl.BlockSpec(arr.shape, lambda bi, t: (0,) * arr.ndim)
    in_specs = []
    for width, off in ((RWKV_MAIN, R_OFF), (RWKV_SMALL, GLR_OFF)):
        cb = off // width
        in_specs += [pl.BlockSpec((1, tm, width), lambda bi, t, cb=cb: (bi, t, cb)),
                     pl.BlockSpec((1, 8, width), lambda bi, t, cb=cb: (bi, jnp.maximum(t * hb - 1, 0), cb)),
                     pl.BlockSpec((1, 8, width), lambda bi, t, cb=cb: (bi, jnp.minimum((t + 1) * hb, nt * hb - 1), cb))]
    params = [mu, w0, w2f, w2b, a0, a2, g2, k_k, k_a, r_k, seg_e, seg_et]
    in_specs += [full(p) for p in params]
    out_blk = pl.BlockSpec((1, tm, RWKV_WIDTH), lambda bi, t: (bi, t, 0))
    return pl.pallas_call(
        _rwkv_prep_kernel,
        grid=(b, nt),
        in_specs=in_specs,
        out_specs=[out_blk] * 11,
        out_shape=[jax.ShapeDtypeStruct((b, s, RWKV_WIDTH), F32)] * 11,
        compiler_params=_cparams(("parallel", "parallel")),
        name="rwkv_prep",
    )(z, z, z, z, z, z, *params)


def _stack_heads(x):
    lane = _iota2(x.shape, 2)
    return jnp.concatenate([jnp.where(lane < RWKV_HEAD, x, 0.0), jnp.where(lane < RWKV_HEAD, 0.0, x)], axis=1)


def _rwkv_chunks(r, k, v, kk, bb, cum, exc, revs, g_ref):
    nb, c, _ = r.shape
    n2 = 2 * c
    tot = jnp.stack([cum[j, (0 if rev else c - 1)][None] for j, rev in enumerate(revs)], axis=0)
    e_neg = jnp.exp(-cum)
    e_end = jnp.exp(tot - cum)
    al2 = _stack_heads(-kk * jnp.exp(exc))
    rb2 = _stack_heads(r * jnp.exp(cum))
    bt2 = _stack_heads(bb * e_neg)
    kt2 = _stack_heads(k * e_neg)
    be2 = _stack_heads(bb * e_end)
    ke2 = _stack_heads(k * e_end)
    v2 = _stack_heads(v)
    gram = _bdot_nt(jnp.concatenate([al2, rb2], axis=1), jnp.concatenate([bt2, kt2], axis=1))
    strict = jnp.stack([_order_mask(n2, rev, True, c) for rev in revs], axis=0)
    incl = jnp.stack([_order_mask(n2, rev, False, c) for rev in revs], axis=0)
    a_ab = jnp.where(strict, gram[:, :n2, :n2], 0.0)
    a_ak = jnp.where(strict, gram[:, :n2, n2:], 0.0)
    a_rb = jnp.where(incl, gram[:, n2:, :n2], 0.0)
    a_rk = jnp.where(incl, gram[:, n2:, n2:], 0.0)
    t_inv = _neumann_inverse(a_ab)
    wu = _bdot(t_inv, jnp.concatenate([al2, _bdot(a_ak, v2)], axis=2))
    g = g_ref[...]
    proj = _bdot_nt(jnp.concatenate([wu[:, :, :LANES], rb2], axis=1), g)
    u2 = proj[:, :n2] + wu[:, :, LANES:]
    uv = jnp.concatenate([u2, v2], axis=1)
    o2 = proj[:, n2:] + _bdot(jnp.concatenate([a_rb, a_rk], axis=2), uv)
    g_ref[...] = g * jnp.exp(tot) + _bdot_tn(uv, jnp.concatenate([be2, ke2], axis=1))
    return o2[:, :c] + o2[:, c:]


def _rwkv_scan_kernel(*refs, pairs):
    fwd_refs, bwd_refs = refs[0:7], refs[7:14]
    of_ref, ob_ref, g_ref = refs[14:17]

    @pl.when(pl.program_id(2) == 0)
    def _():
        g_ref[...] = jnp.zeros_like(g_ref)

    revs = [False] * pairs + [True] * pairs
    operands = []
    for t in range(7):
        operands.append(jnp.stack([ref[0, :, p * LANES:(p + 1) * LANES]
                                   for ref in (fwd_refs[t], bwd_refs[t]) for p in range(pairs)], axis=0))
    o = _rwkv_chunks(*operands, revs, g_ref)
    for d, o_ref in enumerate((of_ref, ob_ref)):
        for p in range(pairs):
            o_ref[0, :, p * LANES:(p + 1) * LANES] = o[d * pairs + p]


def rwkv_scan(r, k, v, kk, bb, cum_f, exc_f, cum_b, exc_b, pairs):
    b, s, _ = r.shape
    n = s // CHUNK
    w = pairs * LANES
    fwd = pl.BlockSpec((1, CHUNK, w), lambda bi, p, t: (bi, t, p))
    bwd = pl.BlockSpec((1, CHUNK, w), lambda bi, p, t: (bi, n - 1 - t, p))
    return pl.pallas_call(
        functools.partial(_rwkv_scan_kernel, pairs=pairs),
        grid=(b, RWKV_WIDTH // w, n),
        in_specs=[fwd] * 7 + [bwd] * 7,
        out_specs=[fwd, bwd],
        out_shape=[jax.ShapeDtypeStruct((b, s, RWKV_WIDTH), F32)] * 2,
        scratch_shapes=[pltpu.VMEM((2 * pairs, LANES, LANES), F32)],
        compiler_params=_cparams(("parallel", "parallel", "arbitrary")),
        name="rwkv_scan",
    )(r, k, v, kk, bb, cum_f, exc_f, r, k, v, kk, bb, cum_b, exc_b)


def _rwkv_post_kernel(of_ref, ob_ref, bonus_ref, g_ref, lnw_ref, lnb_ref, e_ref, et_ref, y_ref):
    e = e_ref[...]
    et = et_ref[...]
    o = of_ref[0] + ob_ref[0]
    mean = _seg_sum(o, e, et) * (1.0 / RWKV_HEAD)
    cen = o - mean
    var = _seg_sum(cen * cen, e, et) * (1.0 / RWKV_HEAD)
    y = cen * lax.rsqrt(var + RWKV_LN_EPS) * lnw_ref[...] + lnb_ref[...]
    y_ref[0] = ((y + bonus_ref[0]) * g_ref[0]).astype(y_ref.dtype)


def rwkv_post(o_f, o_b, bonus, g, ln_w, ln_b, seg_e, seg_et, tm):
    b, s, w = o_f.shape
    blk = pl.BlockSpec((1, tm, w), lambda bi, t: (bi, t, 0))
    full = lambda arr: pl.BlockSpec(arr.shape, lambda bi, t: (0,) * arr.ndim)
    params = [ln_w.reshape(1, w), ln_b.reshape(1, w), seg_e, seg_et]
    return pl.pallas_call(
        _rwkv_post_kernel,
        grid=(b, s // tm),
        in_specs=[blk] * 4 + [full(p) for p in params],
        out_specs=blk,
        out_shape=jax.ShapeDtypeStruct((b, s, w), BF16),
        compiler_params=_cparams(("parallel", "parallel")),
        name="rwkv_post",
    )(o_f, o_b, bonus, g, *params)


def _gdn_conv_kernel(z_ref, zp_ref, zn_ref, cw_ref, sc_ref, o_ref, *, l2norm):
    t = pl.program_id(1)
    tm = z_ref.shape[1]
    row = _iota2((tm, 1), 0)
    first = t == 0
    last = t == pl.num_programs(1) - 1
    z = z_ref[0]
    p6 = jnp.where(first, 0.0, zp_ref[0, 6:7, :])
    p7 = jnp.where(first, 0.0, zp_ref[0, 7:8, :])
    n0 = jnp.where(last, 0.0, zn_ref[0, 0:1, :])
    n1 = jnp.where(last, 0.0, zn_ref[0, 1:2, :])
    zm1 = jnp.where(row == 0, p7, pltpu.roll(z, 1, axis=0))
    zm2 = jnp.where(row == 0, p6, jnp.where(row == 1, p7, pltpu.roll(z, 2, axis=0)))
    zp1 = jnp.where(row == tm - 1, n0, pltpu.roll(z, tm - 1, axis=0))
    zp2 = jnp.where(row == tm - 1, n1, jnp.where(row == tm - 2, n0, pltpu.roll(z, tm - 2, axis=0)))
    y = (cw_ref[0:1] * zm2 + cw_ref[1:2] * zm1 + cw_ref[2:3] * z + cw_ref[3:4] * zp1 + cw_ref[4:5] * zp2)
    y = _silu(y)
    if l2norm:
        for h in range(y.shape[1] // GDN_DK):
            sl = slice(h * GDN_DK, (h + 1) * GDN_DK)
            yh = y[:, sl]
            ss = jnp.sum(yh * yh, axis=-1, keepdims=True)
            o_ref[0, :, sl] = yh * lax.rsqrt(ss + EPS) * sc_ref[0:1, sl]
    else:
        o_ref[0] = y


def gdn_conv(z, conv_w, scale, col_off, width, l2norm, tm, cw):
    b, s, _ = z.shape
    nt = s // tm
    hb = tm // 8
    cb = col_off // cw
    in_specs = [pl.BlockSpec((1, tm, cw), lambda bi, t, c: (bi, t, cb + c)),
                pl.BlockSpec((1, 8, cw), lambda bi, t, c: (bi, jnp.maximum(t * hb - 1, 0), cb + c)),
                pl.BlockSpec((1, 8, cw), lambda bi, t, c: (bi, jnp.minimum((t + 1) * hb, nt * hb - 1), cb + c)),
                pl.BlockSpec((conv_w.shape[0], cw), lambda bi, t, c: (0, cb + c)),
                pl.BlockSpec((1, cw), lambda bi, t, c: (0, c))]
    return pl.pallas_call(
        functools.partial(_gdn_conv_kernel, l2norm=l2norm),
        grid=(b, nt, width // cw),
        in_specs=in_specs,
        out_specs=pl.BlockSpec((1, tm, cw), lambda bi, t, c: (bi, t, c)),
        out_shape=jax.ShapeDtypeStruct((b, s, width), F32),
        compiler_params=_cparams(("parallel", "parallel", "parallel")),
        name="gdn_conv_norm" if l2norm else "gdn_conv",
    )(z, z, z, conv_w, scale)


def _gdn_gates_kernel(zs_ref, na_ref, dtb_ref, o_ref):
    zs = zs_ref[0]
    tm = zs.shape[0]
    gg = na_ref[...] * _softplus(zs + dtb_ref[...])
    gam_f = _dot(_order_mask(tm, False, False, CHUNK).astype(F32), gg, HI)
    gam_b = _dot(_order_mask(tm, True, False, CHUNK).astype(F32), gg, HI)
    lane = _iota2(zs.shape, 1)
    o_ref[0] = jnp.where(lane < 2 * GDN_VHEADS, _sigmoid(zs), jnp.where(lane < 3 * GDN_VHEADS, gam_f, gam_b))


def gdn_gates(zs, neg_a, dtb, tm):
    b, s, w = zs.shape
    blk = pl.BlockSpec((1, tm, w), lambda bi, t: (bi, t, 0))
    row = pl.BlockSpec((1, w), lambda bi, t: (0, 0))
    return pl.pallas_call(
        _gdn_gates_kernel,
        grid=(b, s // tm),
        in_specs=[blk, row, row],
        out_specs=blk,
        out_shape=jax.ShapeDtypeStruct((b, s, w), F32),
        compiler_params=_cparams(("parallel", "parallel")),
        name="gdn_gates",
    )(zs, neg_a, dtb)


def _gdn_chunks(q, k, v2, beta2, gam2, revs, s_ref):
    nb, c, _ = q.shape
    n2 = 2 * c
    per_batch = lambda fn: jnp.stack([fn(r) for r in revs], axis=0)
    incl = per_batch(lambda r: _order_mask(n2, r, False, c))
    strict = per_batch(lambda r: _order_mask(n2, r, True, c))
    gam_c = jnp.broadcast_to(gam2, (nb, n2, n2))
    gam_r = jnp.swapaxes(gam_c, 1, 2)
    tot2 = jnp.stack([jnp.concatenate(
        [jnp.broadcast_to(gam2[j, h * c + (0 if r else c - 1)][None], (c, 1)) for h in range(2)], axis=0)
        for j, r in enumerate(revs)], axis=0)
    diff = gam_c - gam_r
    dec_s = jnp.where(strict, jnp.exp(jnp.where(strict, diff, 0.0)), 0.0)
    dec_i = jnp.where(incl, jnp.exp(jnp.where(incl, diff, 0.0)), 0.0)
    k2 = jnp.concatenate([k, k], axis=1)
    q2 = jnp.concatenate([q, q], axis=1)
    gram = _bdot_nt(jnp.concatenate([k2, q2], axis=1), k2)
    a_mat = gram[:, :n2] * beta2 * dec_s
    qk = gram[:, n2:] * dec_i
    t_inv = _neumann_inverse(-a_mat)
    e_gam = jnp.exp(gam2)
    uw = _bdot(t_inv, jnp.concatenate([v2 * beta2, k2 * (beta2 * e_gam)], axis=2))
    u2 = uw[:, :, :GDN_DV]
    w2 = uw[:, :, GDN_DV:]
    qd2 = q2 * e_gam
    ke2 = k2 * jnp.exp(tot2 - gam2)
    dl2 = jnp.exp(tot2)
    ws, qs = [], []
    for e in range(2):
        rows = slice(e * c, (e + 1) * c)
        both = _bdot(jnp.concatenate([w2[:, rows], qd2[:, rows]], axis=1), s_ref[e])
        ws.append(both[:, :c])
        qs.append(both[:, c:])
    vnew2 = u2 - jnp.concatenate(ws, axis=1)
    o2 = jnp.concatenate(qs, axis=1) + _bdot(qk, vnew2)
    for e in range(2):
        rows = slice(e * c, (e + 1) * c)
        s_ref[e] = s_ref[e] * dl2[:, e * c:e * c + 1, :] + _bdot_tn(ke2[:, rows], vnew2[:, rows])
    return o2


def _gdn_scan_kernel(qf, kf, vf, gf, qb, kb, vb, gb, of_ref, ob_ref, s_ref, *, heads):
    @pl.when(pl.program_id(2) == 0)
    def _():
        s_ref[...] = jnp.zeros_like(s_ref)

    c = qf.shape[1]
    lane = _iota2((c, LANES), 1)

    def column(gates, idx):
        return jnp.sum(jnp.where(lane == idx, gates, 0.0), axis=1, keepdims=True)

    qs, ks, vs, betas, gams, revs = [], [], [], [], [], []
    for d, (q_ref, k_ref, v_ref, g_ref) in enumerate(((qf, kf, vf, gf), (qb, kb, vb, gb))):
        gates = g_ref[0]
        for i in range(heads):
            vh = 2 * (pl.program_id(1) * heads + i)
            ksl = slice(i * GDN_DK, (i + 1) * GDN_DK)
            qs.append(q_ref[0, :, ksl])
            ks.append(k_ref[0, :, ksl])
            vs.append(jnp.concatenate([v_ref[0, :, (2 * i + e) * GDN_DV:(2 * i + e + 1) * GDN_DV] for e in range(2)],
                                      axis=0))
            betas.append(jnp.concatenate([column(gates, d * GDN_VHEADS + vh + e) for e in range(2)], axis=0))
            gams.append(jnp.concatenate([column(gates, (2 + d) * GDN_VHEADS + vh + e) for e in range(2)], axis=0))
            revs.append(d == 1)
    o2 = _gdn_chunks(jnp.stack(qs), jnp.stack(ks), jnp.stack(vs), jnp.stack(betas), jnp.stack(gams), revs, s_ref)
    for d, o_ref in enumerate((of_ref, ob_ref)):
        for i in range(heads):
            j = d * heads + i
            for e in range(2):
                o_ref[0, :, (2 * i + e) * GDN_DV:(2 * i + e + 1) * GDN_DV] = o2[j, e * c:(e + 1) * c]


def gdn_scan(qk, v, gates, heads):
    b, s, _ = v.shape
    n = s // CHUNK
    c = CHUNK
    ng = GDN_KHEADS // heads

    def spec(w, off, rev):
        if rev:
            return pl.BlockSpec((1, c, w), lambda bi, h, t: (bi, n - 1 - t, off + h))
        return pl.BlockSpec((1, c, w), lambda bi, h, t: (bi, t, off + h))

    def gspec(rev):
        if rev:
            return pl.BlockSpec((1, c, LANES), lambda bi, h, t: (bi, n - 1 - t, 0))
        return pl.BlockSpec((1, c, LANES), lambda bi, h, t: (bi, t, 0))

    kw = heads * GDN_DK
    vw = heads * 2 * GDN_DV
    in_specs = [spec(kw, 0, False), spec(kw, ng, False), spec(vw, 0, False), gspec(False),
                spec(kw, 0, True), spec(kw, ng, True), spec(vw, 0, True), gspec(True)]
    return pl.pallas_call(
        functools.partial(_gdn_scan_kernel, heads=heads),
        grid=(b, ng, n),
        in_specs=in_specs,
        out_specs=[spec(vw, 0, False), spec(vw, 0, True)],
        out_shape=[jax.ShapeDtypeStruct((b, s, GDN_VW), F32)] * 2,
        scratch_shapes=[pltpu.VMEM((2, 2 * heads, GDN_DK, GDN_DV), F32)],
        compiler_params=_cparams(("parallel", "parallel", "arbitrary")),
        name="gdn_scan",
    )(qk, qk, v, gates, qk, qk, v, gates)


def _gdn_post_kernel(of_ref, ob_ref, gate_ref, nw_ref, y_ref):
    nw = nw_ref[...]
    for h in range(of_ref.shape[2] // GDN_DV):
        sl = slice(h * GDN_DV, (h + 1) * GDN_DV)
        o = of_ref[0, :, sl] + ob_ref[0, :, sl]
        ms = jnp.mean(o * o, axis=-1, keepdims=True)
        y = o * lax.rsqrt(ms + EPS) * nw
        y_ref[0, :, sl] = (y * _silu(gate_ref[0, :, sl])).astype(y_ref.dtype)


def gdn_post(o_f, o_b, z, gate_off, norm_w, tm, cw):
    b, s, w = o_f.shape
    blk = pl.BlockSpec((1, tm, cw), lambda bi, t, c: (bi, t, c))
    gb = gate_off // cw
    return pl.pallas_call(
        _gdn_post_kernel,
        grid=(b, s // tm, w // cw),
        in_specs=[blk, blk, pl.BlockSpec((1, tm, cw), lambda bi, t, c: (bi, t, gb + c)),
                  pl.BlockSpec((1, GDN_DV), lambda bi, t, c: (0, 0))],
        out_specs=blk,
        out_shape=jax.ShapeDtypeStruct((b, s, w), BF16),
        compiler_params=_cparams(("parallel", "parallel", "parallel")),
        name="gdn_post",
    )(o_f, o_b, z, norm_w.reshape(1, GDN_DV))


def _moe_router_kernel(x_ref, g_ref, w_ref, b_ref, h_ref, route_ref):
    x = x_ref[...]
    ms = jnp.mean(x * x, axis=-1, keepdims=True)
    h = x * lax.rsqrt(ms + EPS) * g_ref[...]
    h_ref[...] = h.astype(h_ref.dtype)
    lane_i = _iota2((x.shape[0], LANES), 1)
    lane = lane_i.astype(F32)
    lane_grp = lax.shift_right_logical(lane_i, 3).astype(F32)
    neg = -jnp.inf
    logits = _dot(h, w_ref[...], HI) + b_ref[...]
    gl = jnp.where((lane_i >= N_EXPERTS) & (lane_i < N_EXPERTS + N_GROUPS), logits, neg)
    gmax = jnp.max(gl, axis=-1, keepdims=True)
    gidx = jnp.min(jnp.where(gl == gmax, lane, float(LANES)), axis=-1, keepdims=True) - float(N_EXPERTS)
    grp_w = 1.0 / jnp.sum(jnp.exp(gl - gmax), axis=-1, keepdims=True)
    sel = jnp.where((lane_i < N_EXPERTS) & (lane_grp == gidx), logits, neg)
    m1 = jnp.max(sel, axis=-1, keepdims=True)
    i1 = jnp.min(jnp.where(sel == m1, lane, float(LANES)), axis=-1, keepdims=True)
    sel2 = jnp.where(lane == i1, neg, sel)
    m2 = jnp.max(sel2, axis=-1, keepdims=True)
    i2 = jnp.min(jnp.where(sel2 == m2, lane, float(LANES)), axis=-1, keepdims=True)
    e2 = jnp.exp(m2 - m1)
    w1 = grp_w / (1.0 + e2)
    w2 = grp_w * e2 / (1.0 + e2)
    route_ref[...] = jnp.where(lane_i == 0, i1, jnp.where(lane_i == 1, i2, jnp.where(
        lane_i == 2, w1, jnp.where(lane_i == 3, w2, 0.0))))


def moe_router(x, g, w_group, b_group, w_router, b_router, tm):
    m, d = x.shape
    tm = min(tm, m)
    pad = lambda w: jnp.pad(w, ((0, 0), (0, LANES - w.shape[1])))
    full = lambda arr: pl.BlockSpec(arr.shape, lambda i: (0, 0))
    params = [g.reshape(1, d), pad(jnp.concatenate([w_router, w_group], axis=1)),
              pad(jnp.concatenate([b_router, b_group]).reshape(1, -1))]
    return pl.pallas_call(
        _moe_router_kernel,
        grid=(m // tm,),
        in_specs=[pl.BlockSpec((tm, d), lambda i: (i, 0))] + [full(p) for p in params],
        out_specs=[pl.BlockSpec((tm, d), lambda i: (i, 0)), pl.BlockSpec((tm, LANES), lambda i: (i, 0))],
        out_shape=[jax.ShapeDtypeStruct((m, d), F32), jax.ShapeDtypeStruct((m, LANES), F32)],
        compiler_params=_cparams(("parallel",)),
        name="moe_router",
    )(x, *params)


def _route_tables(route, tm, n_tiles):
    t = route.shape[0]
    e_flat = route[:, 0:2].astype(jnp.int32).reshape(-1)
    onehot = (e_flat[:, None] == jnp.arange(N_EXPERTS, dtype=jnp.int32)[None, :]).astype(jnp.int32)
    csum = jnp.cumsum(onehot, axis=0)
    counts = csum[-1]
    rank = jnp.take_along_axis(csum, e_flat[:, None], axis=1)[:, 0] - 1
    padded = ((counts + tm - 1) // tm) * tm
    ends = jnp.cumsum(padded)
    pos = (ends - padded)[e_flat] + rank
    tok = jnp.zeros((n_tiles * tm,), jnp.int32).at[pos].set(jnp.arange(2 * t, dtype=jnp.int32) // 2)
    tile_start = jnp.arange(n_tiles, dtype=jnp.int32) * tm
    tile_e = jnp.minimum(jnp.searchsorted(ends, tile_start, side="right"), N_EXPERTS - 1).astype(jnp.int32)
    tile_valid = (tile_start < ends[-1]).astype(jnp.int32)
    return tok, pos.astype(jnp.int32), tile_e, tile_valid


def _moe_ffn_kernel(tile_e_ref, tile_valid_ref, tok_ref, h_hbm, wg_ref, wu_ref, wd_ref, y_ref,
                    xbuf, wg_b, wu_b, wd_b, sem):
    i = pl.program_id(0)
    tm = xbuf.shape[0]
    valid = tile_valid_ref[i] != 0

    @pl.when(valid)
    def _():
        def issue(r, carry):
            pltpu.make_async_copy(h_hbm.at[pl.ds(tok_ref[i * tm + r], 1)], xbuf.at[pl.ds(r, 1)], sem).start()
            return carry

        lax.fori_loop(0, tm, issue, 0, unroll=8)

    @pl.when((i == 0) | (tile_e_ref[i] != tile_e_ref[jnp.maximum(i - 1, 0)]))
    def _():
        wg_b[...] = wg_ref[0].astype(BF16)
        wu_b[...] = wu_ref[0].astype(BF16)
        wd_b[...] = wd_ref[0].astype(BF16)

    @pl.when(valid)
    def _():
        pltpu.make_async_copy(h_hbm.at[pl.ds(0, tm)], xbuf, sem).wait()
        x = xbuf[...].astype(BF16)
        hid = _silu(_dot(x, wg_b[...])) * _dot(x, wu_b[...])
        y_ref[...] = _dot(hid.astype(BF16), wd_b[...])

    @pl.when(jnp.logical_not(valid))
    def _():
        y_ref[...] = jnp.zeros_like(y_ref)


def moe_ffn(h, tok, tile_e, tile_valid, w_gate, w_up, w_down, tm):
    _, d = h.shape
    n_tiles = tile_e.shape[0]
    _, _, ff = w_gate.shape
    wspec = lambda shape: pl.BlockSpec((1,) + shape, lambda i, te, tv, tk: (te[i], 0, 0))
    return pl.pallas_call(
        _moe_ffn_kernel,
        grid_spec=pltpu.PrefetchScalarGridSpec(
            num_scalar_prefetch=3,
            grid=(n_tiles,),
            in_specs=[pl.BlockSpec(memory_space=pl.ANY), wspec((d, ff)), wspec((d, ff)), wspec((ff, d))],
            out_specs=pl.BlockSpec((tm, d), lambda i, te, tv, tk: (i, 0)),
            scratch_shapes=[pltpu.VMEM((tm, d), F32), pltpu.VMEM((d, ff), BF16), pltpu.VMEM((d, ff), BF16),
                            pltpu.VMEM((ff, d), BF16), pltpu.SemaphoreType.DMA(())]),
        out_shape=jax.ShapeDtypeStruct((n_tiles * tm, d), F32),
        compiler_params=_cparams(("arbitrary",)),
        name="moe_ffn",
    )(tile_e, tile_valid, tok, h, w_gate, w_up, w_down)


def _moe_combine_kernel(pos_ref, x_ref, route_ref, y_hbm, g_ref, o_ref, ybuf, sem, *, final_norm):
    i = pl.program_id(0)
    tm = x_ref.shape[0]

    def issue(r, carry):
        base = 2 * (i * tm + r)
        for slot in range(2):
            pltpu.make_async_copy(y_hbm.at[pl.ds(pos_ref[base + slot], 1)], ybuf.at[slot, pl.ds(r, 1)], sem).start()
        return carry

    lax.fori_loop(0, tm, issue, 0, unroll=8)
    for slot in range(2):
        pltpu.make_async_copy(y_hbm.at[pl.ds(0, tm)], ybuf.at[slot], sem).wait()
    route = route_ref[...]
    o = x_ref[...] + route[:, 2:3] * ybuf[0] + route[:, 3:4] * ybuf[1]
    if final_norm:
        ms = jnp.mean(o * o, axis=-1, keepdims=True)
        o = o * lax.rsqrt(ms + EPS) * g_ref[...]
    o_ref[...] = o


def moe_combine(x, route, y_sorted, pos, norm_g, tm):
    m, d = x.shape
    tm = min(tm, m)
    final_norm = norm_g is not None
    g = (norm_g if final_norm else jnp.ones((d,), F32)).reshape(1, d)
    return pl.pallas_call(
        functools.partial(_moe_combine_kernel, final_norm=final_norm),
        grid_spec=pltpu.PrefetchScalarGridSpec(
            num_scalar_prefetch=1,
            grid=(m // tm,),
            in_specs=[pl.BlockSpec((tm, d), lambda i, p: (i, 0)), pl.BlockSpec((tm, LANES), lambda i, p: (i, 0)),
                      pl.BlockSpec(memory_space=pl.ANY), pl.BlockSpec((1, d), lambda i, p: (0, 0))],
            out_specs=pl.BlockSpec((tm, d), lambda i, p: (i, 0)),
            scratch_shapes=[pltpu.VMEM((2, tm, d), F32), pltpu.SemaphoreType.DMA(())]),
        out_shape=jax.ShapeDtypeStruct((m, d), F32),
        compiler_params=_cparams(("arbitrary",)),
        name="moe_combine",
    )(pos, x, route, y_sorted, g)


def _pad_rows(w, rows):
    return jnp.pad(w, ((0, rows - w.shape[0]), (0, 0)))


def _even_layer(x, norm_g, w_in, gla_w_alpha, gla_b_alpha, gla_norm, mu, w0, w2, a0, a2, g2,
                k_k, k_a, r_k, ln_w, ln_b, w_out):
    b, s, d = x.shape
    gla_cols, rw = w_in[:, :3104], w_in[:, 3104:]
    mu_g = lambda lo, hi, width: jnp.pad(mu[:, lo:hi], ((0, 0), (0, width - (hi - lo))))
    pad_c = lambda w, width: jnp.pad(w, ((0, 0), (0, width - w.shape[1])))
    w_cat = jnp.concatenate([
        gla_cols[:, 0:3072],
        rw[:, 0:3072], rw[:, 3360:3616],
        pad_c(rw[:, 3072:3168], LANES), pad_c(rw[:, 3168:3264], LANES), pad_c(rw[:, 3264:3360], LANES),
        pad_c(gla_cols[:, 3072:3104], LANES)], axis=1).astype(BF16)
    mu_cat = jnp.concatenate([
        mu[:, 0:3072], mu[:, 3360:3616], mu_g(3072, 3168, LANES), mu_g(3168, 3264, LANES),
        mu_g(3264, 3360, LANES)], axis=1)
    z = norm_matmul(x.reshape(b * s, d), norm_g, w_cat, 512, 768).reshape(b, s, EVEN_PAD)

    wa = jnp.stack([_pad_rows(gla_w_alpha[0], LANES),
                    jnp.pad(gla_w_alpha[1], ((GLA_LOWRANK, LANES - 2 * GLA_LOWRANK), (0, 0)))])
    gla_f, gla_b = gla_scan(z, wa, gla_b_alpha)
    y_gla = gla_post(gla_f, gla_b, z, gla_norm, 256)

    head_of_lane = jnp.arange(RWKV_WIDTH) // RWKV_HEAD
    seg_e = (head_of_lane[:, None] == jnp.arange(LANES)[None, :]).astype(F32)
    seg_et = seg_e.T
    row = lambda p: p.reshape(1, RWKV_WIDTH)
    r, k, v, kk, bb, cum_f, exc_f, cum_b, exc_b, g, bonus = rwkv_prep(
        z, mu_cat, w0, _pad_rows(w2[0], LANES), _pad_rows(w2[1], LANES), row(a0), _pad_rows(a2, LANES), g2,
        row(k_k), row(k_a), row(r_k), seg_e, seg_et, 128)
    rw_f, rw_b = rwkv_scan(r, k, v, kk, bb, cum_f, exc_f, cum_b, exc_b, RWKV_PAIRS_PER_STEP)
    y_rwkv = rwkv_post(rw_f, rw_b, bonus, g, ln_w, ln_b, seg_e, seg_et, 256)

    w_out = w_out.astype(BF16)
    out = proj_residual([y_gla.reshape(b * s, GLA_WIDTH), y_rwkv.reshape(b * s, RWKV_WIDTH)],
                        [w_out[:GLA_WIDTH], w_out[GLA_WIDTH:]], x.reshape(b * s, d), 512, 1024)
    return out.reshape(b, s, d)


def _odd_layer(x, norm_g, w_in, conv_w, a_log, dt_bias, norm_w, w_out):
    b, s, d = x.shape
    x2 = x.reshape(b * s, d)
    main = GDN_QKV + GDN_VW
    z = norm_matmul(x2, norm_g, w_in[:, :main].astype(BF16), 512, 1024).reshape(b, s, main)
    zs = norm_matmul(x2, norm_g, w_in[:, main:].astype(BF16), 512, LANES).reshape(b, s, LANES)
    scale = jnp.concatenate([jnp.full((1, GDN_KW), GDN_DK ** -0.5, F32), jnp.ones((1, GDN_KW), F32)], axis=1)
    qk = gdn_conv(z, conv_w, scale, 0, 2 * GDN_KW, True, 256, 1024)
    v = gdn_conv(z, conv_w, scale, 2 * GDN_KW, GDN_VW, False, 256, 1024)
    zero = jnp.zeros((2 * GDN_VHEADS,), F32)
    neg_a = jnp.concatenate([zero, -jnp.exp(a_log.reshape(-1))]).reshape(1, LANES)
    dtb = jnp.concatenate([zero, dt_bias.reshape(-1)]).reshape(1, LANES)
    gates = gdn_gates(zs, neg_a, dtb, 256)
    o_f, o_b = gdn_scan(qk, v, gates, GDN_HEADS_PER_STEP)
    y = gdn_post(o_f, o_b, z, GDN_QKV, norm_w, 256, 1024)
    out = proj_residual([y.reshape(b * s, GDN_VW)], [w_out.astype(BF16)], x2, 512, 1024)
    return out.reshape(b, s, d)


def _moe_layer(x, norm_g, w_group, b_group, w_router, b_router, w_gate, w_up, w_down, final_g):
    b, s, d = x.shape
    x2 = x.reshape(b * s, d)
    h, route = moe_router(x2, norm_g, w_group, b_group, w_router, b_router, 256)
    n_tiles = -(-2 * b * s // MOE_ROW_TILE) + N_EXPERTS
    tok, pos, tile_e, tile_valid = _route_tables(route, MOE_ROW_TILE, n_tiles)
    y_sorted = moe_ffn(h, tok, tile_e, tile_valid, w_gate, w_up, w_down, MOE_ROW_TILE)
    out = moe_combine(x2, route, y_sorted, pos, final_g, 256)
    return out.reshape(b, s, d)


def kernel(x, norm_mix, norm_ffn, norm_final, ev_w_in, ev_gla_w_alpha, ev_gla_b_alpha, ev_gla_norm, ev_rwkv_mu, ev_rwkv_w0, ev_rwkv_w2, ev_rwkv_a0, ev_rwkv_a2, ev_rwkv_g2, ev_rwkv_k_k, ev_rwkv_k_a, ev_rwkv_r_k, ev_rwkv_ln_w, ev_rwkv_ln_b, ev_w_out, od_w_in, od_conv, od_a_log, od_dt_bias, od_norm, od_w_out, moe_w_group, moe_b_group, moe_w_router, moe_b_router, moe_w_gate, moe_w_up, moe_w_down):
    depth = norm_mix.shape[0]
    for i in range(depth):
        j = i // 2
        if i % 2 == 0:
            x = _even_layer(x, norm_mix[i], ev_w_in[j], ev_gla_w_alpha[j], ev_gla_b_alpha[j], ev_gla_norm[j],
                            ev_rwkv_mu[j], ev_rwkv_w0[j], ev_rwkv_w2[j], ev_rwkv_a0[j], ev_rwkv_a2[j],
                            ev_rwkv_g2[j], ev_rwkv_k_k[j], ev_rwkv_k_a[j], ev_rwkv_r_k[j],
                            ev_rwkv_ln_w[j], ev_rwkv_ln_b[j], ev_w_out[j])
        else:
            x = _odd_layer(x, norm_mix[i], od_w_in[j], od_conv[j], od_a_log[j], od_dt_bias[j],
                           od_norm[j], od_w_out[j])
        x = _moe_layer(x, norm_ffn[i], moe_w_group[i], moe_b_group[i], moe_w_router[i], moe_b_router[i],
                       moe_w_gate[i], moe_w_up[i], moe_w_down[i], norm_final if i == depth - 1 else None)
    return x
```

```python
import functools
import math

import jax
import jax.numpy as jnp
from jax import lax
from jax.experimental import pallas as pl
from jax.experimental.pallas import tpu as pltpu

F32 = jnp.float32
BF16 = jnp.bfloat16
HI = lax.Precision.HIGHEST

EPS = 1e-6
CHUNK = 64
LANES = 128
VMEM_LIMIT = 56 * 1024 * 1024

D_MODEL = 2048
GLA_HEADS = 4
GLA_DK = 128
GLA_DV = 256
GLA_KW = 512
GLA_WIDTH = 1024
GLA_LOWRANK = 16
GLA_TAU = 16.0
RWKV_HEAD = 64
RWKV_WIDTH = 1024
RWKV_HEADS = 16
DECAY_LORA = 96
ICLR_LORA = 96
GATE_LORA = 256
RWKV_LN_EPS = 64e-5
GDN_DK = 128
GDN_DV = 128
GDN_KHEADS = 16
GDN_VHEADS = 32
GDN_KW = 2048
GDN_VW = 4096
GDN_QKV = 8192
SCAN_CHUNKS_PER_STEP = 2
RWKV_PAIRS_PER_STEP = 4
GDN_HEADS_PER_STEP = 4
MOE_ROW_TILE = 256
N_GROUPS = 4
EXPERTS_PER_GROUP = 8
N_EXPERTS = 32
EXPERT_FF = 512

GQ_OFF, GK_OFF, GV_OFF, GG_OFF = 0, 512, 1024, 2048
R_OFF, K_OFF, V_OFF = 3072, 4096, 5120
GLR_OFF, WLF_OFF, WLB_OFF, ALR_OFF, AL_OFF = 6144, 6400, 6528, 6656, 6784
RWKV_MAIN = 3072
RWKV_SMALL = 768
EVEN_PAD = 6912


def _cparams(sem):
    return pltpu.CompilerParams(dimension_semantics=sem, vmem_limit_bytes=VMEM_LIMIT)


def _mm(a, b, ca, cb, precision):
    if a.ndim == 3:
        dims = (((ca + 1,), (cb + 1,)), ((0,), (0,)))
    else:
        dims = (((ca,), (cb,)), ((), ()))
    return lax.dot_general(a, b, dims, preferred_element_type=F32, precision=precision)


def _dot(a, b, precision=None):
    return _mm(a, b, 1, 0, precision)


def _dot_nt(a, b, precision=None):
    return _mm(a, b, 1, 1, precision)


def _dot_tn(a, b, precision=None):
    return _mm(a, b, 0, 0, precision)


def _bf16_pieces(x, n):
    pieces = []
    for _ in range(n - 1):
        p = x.astype(BF16)
        pieces.append(p)
        x = x - p.astype(F32)
    pieces.append(x.astype(BF16))
    return pieces


def _mask_dot(mask, x, pieces=3):
    mb = mask.astype(BF16)
    return sum(_dot(mb, p) for p in _bf16_pieces(x, pieces))


def _dot_mask(x, mask, pieces=2):
    mb = mask.astype(BF16)
    return sum(_dot(p, mb) for p in _bf16_pieces(x, pieces))


def _dot3(a, b):
    ah, al = _bf16_pieces(a, 2)
    bh, bl = _bf16_pieces(b, 2)
    return _dot(ah, bh) + _dot(ah, bl) + _dot(al, bh)


def _bdot(a, b):
    return _dot(a.astype(BF16), b.astype(BF16))


def _bdot_nt(a, b):
    return _dot_nt(a.astype(BF16), b.astype(BF16))


def _bdot_tn(a, b):
    return _dot_tn(a.astype(BF16), b.astype(BF16))


def _sigmoid(x):
    return 1.0 / (1.0 + jnp.exp(-x))


def _silu(x):
    return x * _sigmoid(x)


def _softplus(x):
    return jnp.maximum(x, 0.0) + jnp.log(1.0 + jnp.exp(-jnp.abs(x)))


def _iota2(shape, dim):
    return lax.broadcasted_iota(jnp.int32, shape, dim)


def _order_mask(n, rev, strict, block):
    i = _iota2((n, n), 0)
    j = _iota2((n, n), 1)
    if rev:
        m = (j > i) if strict else (j >= i)
    else:
        m = (j < i) if strict else (j <= i)
    if block < n:
        sh = block.bit_length() - 1
        m = m & (lax.shift_right_logical(i, sh) == lax.shift_right_logical(j, sh))
    return m


def _neumann_inverse(nmat):
    n = nmat.shape[-1]
    eye = (_iota2((n, n), 0) == _iota2((n, n), 1)).astype(F32)
    t = eye + nmat
    nb = nmat.astype(BF16)
    p = _dot(nb, nb)
    for step in range(5):
        pb = p.astype(BF16)
        if step < 4:
            both = _dot(pb, jnp.concatenate([pb, t.astype(BF16)], axis=-1))
            p = both[..., :n]
            t = t + both[..., n:]
        else:
            t = t + _dot(pb, t.astype(BF16))
    return t


def _norm_matmul_kernel(x_ref, g_ref, w_ref, o_ref, h_ref):
    @pl.when(pl.program_id(1) == 0)
    def _():
        x = x_ref[...]
        ms = jnp.mean(x * x, axis=-1, keepdims=True)
        h_ref[...] = (x * lax.rsqrt(ms + EPS) * g_ref[...]).astype(BF16)

    o_ref[...] = _dot(h_ref[...], w_ref[...])


def norm_matmul(x, g, w, tm, tn):
    m, d = x.shape
    tm = min(tm, m)
    n = w.shape[1]
    return pl.pallas_call(
        _norm_matmul_kernel,
        grid=(m // tm, n // tn),
        in_specs=[pl.BlockSpec((tm, d), lambda i, j: (i, 0)),
                  pl.BlockSpec((1, d), lambda i, j: (0, 0)),
                  pl.BlockSpec((d, tn), lambda i, j: (0, j))],
        out_specs=pl.BlockSpec((tm, tn), lambda i, j: (i, j)),
        out_shape=jax.ShapeDtypeStruct((m, n), F32),
        scratch_shapes=[pltpu.VMEM((tm, d), BF16)],
        compiler_params=_cparams(("parallel", "arbitrary")),
        name="norm_matmul",
    )(x, g.reshape(1, d), w)


def _proj_residual_kernel(*refs, n_lhs):
    x_ref = refs[2 * n_lhs]
    o_ref = refs[2 * n_lhs + 1]
    acc = x_ref[...]
    for t in range(n_lhs):
        acc = acc + _dot(refs[t][...], refs[n_lhs + t][...])
    o_ref[...] = acc


def proj_residual(ys, ws, x, tm, tn):
    m, n = x.shape
    tm = min(tm, m)
    n_lhs = len(ys)
    in_specs = [pl.BlockSpec((tm, y.shape[1]), lambda i, j: (i, 0)) for y in ys]
    in_specs += [pl.BlockSpec((w.shape[0], tn), lambda i, j: (0, j)) for w in ws]
    in_specs += [pl.BlockSpec((tm, tn), lambda i, j: (i, j))]
    return pl.pallas_call(
        functools.partial(_proj_residual_kernel, n_lhs=n_lhs),
        grid=(m // tm, n // tn),
        in_specs=in_specs,
        out_specs=pl.BlockSpec((tm, tn), lambda i, j: (i, j)),
        out_shape=jax.ShapeDtypeStruct((m, n), F32),
        compiler_params=_cparams(("parallel", "arbitrary")),
        name="proj_residual",
    )(*ys, *ws, x)


def _final_norm_kernel(x_ref, g_ref, o_ref):
    x = x_ref[...]
    ms = jnp.mean(x * x, axis=-1, keepdims=True)
    o_ref[...] = x * lax.rsqrt(ms + EPS) * g_ref[...]


def final_norm(x, g, tm):
    m, d = x.shape
    tm = min(tm, m)
    return pl.pallas_call(
        _final_norm_kernel,
        grid=(m // tm,),
        in_specs=[pl.BlockSpec((tm, d), lambda i: (i, 0)), pl.BlockSpec((1, d), lambda i: (0, 0))],
        out_specs=pl.BlockSpec((tm, d), lambda i: (i, 0)),
        out_shape=jax.ShapeDtypeStruct((m, d), F32),
        compiler_params=_cparams(("parallel",)),
        name="final_norm",
    )(x, g.reshape(1, d))


def _gla_chunks(q, k, v, cum, revs, s_ref):
    c = q.shape[1]
    tot = jnp.stack([cum[j, (0 if rev else c - 1)][None] for j, rev in enumerate(revs)], axis=0)
    incl = jnp.stack([_order_mask(c, rev, False, c) for rev in revs], axis=0)
    q_dec = q * ((GLA_DK ** -0.5) * jnp.exp(cum))
    k_dec = k * jnp.exp(-cum)
    k_end = k * jnp.exp(tot - cum)
    scores = jnp.where(incl, _bdot_nt(q_dec, k_dec), 0.0)
    state = s_ref[...]
    o = _bdot(scores, v) + _bdot_nt(q_dec, state)
    s_ref[...] = jnp.exp(tot) * state + _bdot_tn(v, k_end)
    return o


def _gla_kernel(qf, kf, vf, alf, qb, kb, vb, alb, wa_ref, ba_ref, of_ref, ob_ref, s_ref):
    @pl.when(pl.program_id(1) == 0)
    def _():
        s_ref[...] = jnp.zeros_like(s_ref)

    c = qf.shape[1]
    qs, ks, vs, cums, revs = [], [], [], [], []
    for d, (q_ref, k_ref, v_ref, al_ref) in enumerate(((qf, kf, vf, alf), (qb, kb, vb, alb))):
        pre = _dot3(al_ref[0], wa_ref[d]) + ba_ref[d:d + 1]
        log_a = (jnp.minimum(pre, 0.0) - jnp.log(1.0 + jnp.exp(-jnp.abs(pre)))) * (1.0 / GLA_TAU)
        cum = _mask_dot(_order_mask(c, d == 1, False, c), log_a)
        for h in range(GLA_HEADS):
            ksl = slice(h * GLA_DK, (h + 1) * GLA_DK)
            qs.append(q_ref[0, :, ksl])
            ks.append(k_ref[0, :, ksl])
            vs.append(v_ref[0, :, h * GLA_DV:(h + 1) * GLA_DV])
            cums.append(cum[:, ksl])
            revs.append(d == 1)
    o = _gla_chunks(jnp.stack(qs), jnp.stack(ks), jnp.stack(vs), jnp.stack(cums), revs, s_ref)
    for d, o_ref in enumerate((of_ref, ob_ref)):
        for h in range(GLA_HEADS):
            o_ref[0, :, h * GLA_DV:(h + 1) * GLA_DV] = o[d * GLA_HEADS + h].astype(o_ref.dtype)


def gla_scan(z, wa, ba):
    b, s, _ = z.shape
    n = s // CHUNK
    c = CHUNK

    def fwd(off, w):
        return pl.BlockSpec((1, c, w), lambda bi, t: (bi, t, off // w))

    def bwd(off, w):
        return pl.BlockSpec((1, c, w), lambda bi, t: (bi, n - 1 - t, off // w))

    in_specs = [fwd(GQ_OFF, GLA_KW), fwd(GK_OFF, GLA_KW), fwd(GV_OFF, GLA_WIDTH), fwd(AL_OFF, LANES),
                bwd(GQ_OFF, GLA_KW), bwd(GK_OFF, GLA_KW), bwd(GV_OFF, GLA_WIDTH), bwd(AL_OFF, LANES),
                pl.BlockSpec(wa.shape, lambda bi, t: (0, 0, 0)), pl.BlockSpec(ba.shape, lambda bi, t: (0, 0))]
    out_specs = [pl.BlockSpec((1, c, GLA_WIDTH), lambda bi, t: (bi, t, 0)),
                 pl.BlockSpec((1, c, GLA_WIDTH), lambda bi, t: (bi, n - 1 - t, 0))]
    return pl.pallas_call(
        _gla_kernel,
        grid=(b, n),
        in_specs=in_specs,
        out_specs=out_specs,
        out_shape=[jax.ShapeDtypeStruct((b, s, GLA_WIDTH), BF16)] * 2,
        scratch_shapes=[pltpu.VMEM((2 * GLA_HEADS, GLA_DV, GLA_DK), F32)],
        compiler_params=_cparams(("parallel", "arbitrary")),
        name="gla_scan",
    )(z, z, z, z, z, z, z, z, wa, ba)


def _gla_post_kernel(of_ref, ob_ref, gate_ref, nw_ref, y_ref):
    o = of_ref[0].astype(F32) + ob_ref[0].astype(F32)
    ms = jnp.mean(o * o, axis=-1, keepdims=True)
    y = o * lax.rsqrt(ms + EPS) * nw_ref[...]
    y_ref[0] = (y * _silu(gate_ref[0])).astype(y_ref.dtype)


def gla_post(o_f, o_b, z, norm_w, tm):
    b, s, w = o_f.shape
    blk = pl.BlockSpec((1, tm, GLA_DV), lambda bi, t, h: (bi, t, h))
    return pl.pallas_call(
        _gla_post_kernel,
        grid=(b, s // tm, GLA_HEADS),
        in_specs=[blk, blk, pl.BlockSpec((1, tm, GLA_DV), lambda bi, t, h: (bi, t, GG_OFF // GLA_DV + h)),
                  pl.BlockSpec((1, GLA_DV), lambda bi, t, h: (0, 0))],
        out_specs=blk,
        out_shape=jax.ShapeDtypeStruct((b, s, w), BF16),
        compiler_params=_cparams(("parallel", "parallel", "parallel")),
        name="gla_post",
    )(o_f, o_b, z, norm_w.reshape(1, GLA_DV))


def _seg_sum(x, e, et):
    return _dot_mask(_dot_mask(x, e), et)


def _rwkv_prep_kernel(z_ref, zp_ref, zn_ref, y_ref, yp_ref, yn_ref, mu_ref, w0_ref, w2f_ref, w2b_ref, a0_ref, a2_ref, g2_ref,
                      kk_ref, ka_ref, rk_ref, e_ref, et_ref,
                      r_out, k_out, v_out, kk_out, b_out, cumf_out, excf_out, cumb_out, excb_out, g_out, bonus_out):
    t = pl.program_id(1)
    tm = z_ref.shape[1]
    row = _iota2((tm, 1), 0)
    first = t == 0
    last = t == pl.num_programs(1) - 1

    def shifted(lo, hi):
        cur, prv, nxt, base = (z_ref, zp_ref, zn_ref, R_OFF) if lo < GLR_OFF else (y_ref, yp_ref, yn_ref, GLR_OFF)
        z = cur[0, :, lo - base:hi - base]
        prev_row = jnp.where(first, 0.0, prv[0, 7:8, lo - base:hi - base])
        next_row = jnp.where(last, 0.0, nxt[0, 0:1, lo - base:hi - base])
        zprev = jnp.where(row == 0, prev_row, pltpu.roll(z, 1, axis=0))
        znext = jnp.where(row == tm - 1, next_row, pltpu.roll(z, tm - 1, axis=0))
        mu = mu_ref[:, lo - R_OFF:hi - R_OFF]
        return z + mu[0:1] * (zprev - z) + mu[1:2] * (znext - z)

    e = e_ref[...]
    et = et_ref[...]
    a = _sigmoid(a0_ref[...] + _bdot(shifted(ALR_OFF, ALR_OFF + LANES), a2_ref[...]))
    k = shifted(K_OFF, K_OFF + RWKV_WIDTH)
    kk_raw = k * kk_ref[...]
    kk = kk_raw * lax.rsqrt(_seg_sum(kk_raw * kk_raw, e, et) + EPS)
    kk_out[0] = kk
    b_out[0] = kk * a
    kmod = k * (1.0 + (a - 1.0) * ka_ref[...])
    k_out[0] = kmod
    r = shifted(R_OFF, R_OFF + RWKV_WIDTH)
    r_out[0] = r
    v = shifted(V_OFF, V_OFF + RWKV_WIDTH)
    v_out[0] = v
    bonus_out[0] = _seg_sum(r * kmod * rk_ref[...], e, et) * v
    g_out[0] = _bdot(_sigmoid(shifted(GLR_OFF, GLR_OFF + GATE_LORA)), g2_ref[...])
    decay_scale = -math.exp(-0.5)
    wf = w0_ref[0:1] + _dot3(jnp.tanh(shifted(WLF_OFF, WLF_OFF + LANES)), w2f_ref[...])
    lwf = decay_scale * _sigmoid(wf)
    cum_f = _mask_dot(_order_mask(tm, False, False, CHUNK), lwf)
    cumf_out[0] = cum_f
    excf_out[0] = cum_f - lwf
    wb = w0_ref[1:2] + _dot3(jnp.tanh(shifted(WLB_OFF, WLB_OFF + LANES)), w2b_ref[...])
    lwb = decay_scale * _sigmoid(wb)
    cum_b = _mask_dot(_order_mask(tm, True, False, CHUNK), lwb)
    cumb_out[0] = cum_b
    excb_out[0] = cum_b - lwb


def rwkv_prep(z, mu, w0, w2f, w2b, a0, a2, g2, k_k, k_a, r_k, seg_e, seg_et, tm):
    b, s, _ = z.shape
    nt = s // tm
    hb = tm // 8
    full = lambda arr: pl.BlockSpec(arr.shape, lambda bi, t: (0,) * arr.ndim)
    in_specs = []
    for width, off in ((RWKV_MAIN, R_OFF), (RWKV_SMALL, GLR_OFF)):
        cb = off // width
        in_specs += [pl.BlockSpec((1, tm, width), lambda bi, t, cb=cb: (bi, t, cb)),
                     pl.BlockSpec((1, 8, width), lambda bi, t, cb=cb: (bi, jnp.maximum(t * hb - 1, 0), cb)),
                     pl.BlockSpec((1, 8, width), lambda bi, t, cb=cb: (bi, jnp.minimum((t + 1) * hb, nt * hb - 1), cb))]
    params = [mu, w0, w2f, w2b, a0, a2, g2, k_k, k_a, r_k, seg_e, seg_et]
    in_specs += [full(p) for p in params]
    out_blk = pl.BlockSpec((1, tm, RWKV_WIDTH), lambda bi, t: (bi, t, 0))
    return pl.pallas_call(
        _rwkv_prep_kernel,
        grid=(b, nt),
        in_specs=in_specs,
        out_specs=[out_blk] * 11,
        out_shape=[jax.ShapeDtypeStruct((b, s, RWKV_WIDTH), F32)] * 11,
        compiler_params=_cparams(("parallel", "parallel")),
        name="rwkv_prep",
    )(z, z, z, z, z, z, *params)


def _stack_heads(x):
    lane = _iota2(x.shape, 2)
    return jnp.concatenate([jnp.where(lane < RWKV_HEAD, x, 0.0), jnp.where(lane < RWKV_HEAD, 0.0, x)], axis=1)


def _rwkv_chunks(r, k, v, kk, bb, cum, exc, revs, g_ref):
    nb, c, _ = r.shape
    n2 = 2 * c
    tot = jnp.stack([cum[j, (0 if rev else c - 1)][None] for j, rev in enumerate(revs)], axis=0)
    e_neg = jnp.exp(-cum)
    e_end = jnp.exp(tot - cum)
    al2 = _stack_heads(-kk * jnp.exp(exc))
    rb2 = _stack_heads(r * jnp.exp(cum))
    bt2 = _stack_heads(bb * e_neg)
    kt2 = _stack_heads(k * e_neg)
    be2 = _stack_heads(bb * e_end)
    ke2 = _stack_heads(k * e_end)
    v2 = _stack_heads(v)
    gram = _bdot_nt(jnp.concatenate([al2, rb2], axis=1), jnp.concatenate([bt2, kt2], axis=1))
    strict = jnp.stack([_order_mask(n2, rev, True, c) for rev in revs], axis=0)
    incl = jnp.stack([_order_mask(n2, rev, False, c) for rev in revs], axis=0)
    a_ab = jnp.where(strict, gram[:, :n2, :n2], 0.0)
    a_ak = jnp.where(strict, gram[:, :n2, n2:], 0.0)
    a_rb = jnp.where(incl, gram[:, n2:, :n2], 0.0)
    a_rk = jnp.where(incl, gram[:, n2:, n2:], 0.0)
    t_inv = _neumann_inverse(a_ab)
    wu = _bdot(t_inv, jnp.concatenate([al2, _bdot(a_ak, v2)], axis=2))
    g = g_ref[...]
    proj = _bdot_nt(jnp.concatenate([wu[:, :, :LANES], rb2], axis=1), g)
    u2 = proj[:, :n2] + wu[:, :, LANES:]
    uv = jnp.concatenate([u2, v2], axis=1)
    o2 = proj[:, n2:] + _bdot(jnp.concatenate([a_rb, a_rk], axis=2), uv)
    g_ref[...] = g * jnp.exp(tot) + _bdot_tn(uv, jnp.concatenate([be2, ke2], axis=1))
    return o2[:, :c] + o2[:, c:]


def _rwkv_scan_kernel(*refs, pairs):
    fwd_refs, bwd_refs = refs[0:7], refs[7:14]
    of_ref, ob_ref, g_ref = refs[14:17]

    @pl.when(pl.program_id(2) == 0)
    def _():
        g_ref[...] = jnp.zeros_like(g_ref)

    revs = [False] * pairs + [True] * pairs
    c = CHUNK
    sub_steps = of_ref.shape[1] // c
    for sub in range(sub_steps):
        rows_of = (slice(sub * c, (sub + 1) * c), slice((sub_steps - 1 - sub) * c, (sub_steps - sub) * c))
        operands = []
        for t in range(7):
            operands.append(jnp.stack([ref[0, rows_of[d], p * LANES:(p + 1) * LANES]
                                       for d, ref in enumerate((fwd_refs[t], bwd_refs[t])) for p in range(pairs)],
                                      axis=0))
        o = _rwkv_chunks(*operands, revs, g_ref)
        for d, o_ref in enumerate((of_ref, ob_ref)):
            for p in range(pairs):
                o_ref[0, rows_of[d], p * LANES:(p + 1) * LANES] = o[d * pairs + p].astype(o_ref.dtype)


def rwkv_scan(r, k, v, kk, bb, cum_f, exc_f, cum_b, exc_b, pairs):
    b, s, _ = r.shape
    c = CHUNK * SCAN_CHUNKS_PER_STEP
    n = s // c
    w = pairs * LANES
    fwd = pl.BlockSpec((1, c, w), lambda bi, p, t: (bi, t, p))
    bwd = pl.BlockSpec((1, c, w), lambda bi, p, t: (bi, n - 1 - t, p))
    return pl.pallas_call(
        functools.partial(_rwkv_scan_kernel, pairs=pairs),
        grid=(b, RWKV_WIDTH // w, n),
        in_specs=[fwd] * 7 + [bwd] * 7,
        out_specs=[fwd, bwd],
        out_shape=[jax.ShapeDtypeStruct((b, s, RWKV_WIDTH), BF16)] * 2,
        scratch_shapes=[pltpu.VMEM((2 * pairs, LANES, LANES), F32)],
        compiler_params=_cparams(("parallel", "parallel", "arbitrary")),
        name="rwkv_scan",
    )(r, k, v, kk, bb, cum_f, exc_f, r, k, v, kk, bb, cum_b, exc_b)


def _rwkv_post_kernel(of_ref, ob_ref, bonus_ref, g_ref, lnw_ref, lnb_ref, e_ref, et_ref, y_ref):
    e = e_ref[...]
    et = et_ref[...]
    o = of_ref[0].astype(F32) + ob_ref[0].astype(F32)
    mean = _seg_sum(o, e, et) * (1.0 / RWKV_HEAD)
    cen = o - mean
    var = _seg_sum(cen * cen, e, et) * (1.0 / RWKV_HEAD)
    y = cen * lax.rsqrt(var + RWKV_LN_EPS) * lnw_ref[...] + lnb_ref[...]
    y_ref[0] = ((y + bonus_ref[0]) * g_ref[0]).astype(y_ref.dtype)


def rwkv_post(o_f, o_b, bonus, g, ln_w, ln_b, seg_e, seg_et, tm):
    b, s, w = o_f.shape
    blk = pl.BlockSpec((1, tm, w), lambda bi, t: (bi, t, 0))
    full = lambda arr: pl.BlockSpec(arr.shape, lambda bi, t: (0,) * arr.ndim)
    params = [ln_w.reshape(1, w), ln_b.reshape(1, w), seg_e, seg_et]
    return pl.pallas_call(
        _rwkv_post_kernel,
        grid=(b, s // tm),
        in_specs=[blk] * 4 + [full(p) for p in params],
        out_specs=blk,
        out_shape=jax.ShapeDtypeStruct((b, s, w), BF16),
        compiler_params=_cparams(("parallel", "parallel")),
        name="rwkv_post",
    )(o_f, o_b, bonus, g, *params)


def _gdn_conv_kernel(z_ref, zp_ref, zn_ref, cw_ref, sc_ref, o_ref, ext_ref, *, l2norm):
    t = pl.program_id(1)
    tm = z_ref.shape[1]
    ext_ref[0:8] = jnp.where(t == 0, 0.0, zp_ref[0])
    ext_ref[8:8 + tm] = z_ref[0]
    ext_ref[8 + tm:16 + tm] = jnp.where(t == pl.num_programs(1) - 1, 0.0, zn_ref[0])
    y = sum(cw_ref[j:j + 1] * ext_ref[6 + j:6 + j + tm] for j in range(cw_ref.shape[0]))
    y = _silu(y)
    if l2norm:
        for h in range(y.shape[1] // GDN_DK):
            sl = slice(h * GDN_DK, (h + 1) * GDN_DK)
            yh = y[:, sl]
            ss = jnp.sum(yh * yh, axis=-1, keepdims=True)
            o_ref[0, :, sl] = yh * lax.rsqrt(ss + EPS) * sc_ref[0:1, sl]
    else:
        o_ref[0] = y


def gdn_conv(z, conv_w, scale, col_off, width, l2norm, tm, cw):
    b, s, _ = z.shape
    nt = s // tm
    hb = tm // 8
    cb = col_off // cw
    in_specs = [pl.BlockSpec((1, tm, cw), lambda bi, t, c: (bi, t, cb + c)),
                pl.BlockSpec((1, 8, cw), lambda bi, t, c: (bi, jnp.maximum(t * hb - 1, 0), cb + c)),
                pl.BlockSpec((1, 8, cw), lambda bi, t, c: (bi, jnp.minimum((t + 1) * hb, nt * hb - 1), cb + c)),
                pl.BlockSpec((conv_w.shape[0], cw), lambda bi, t, c: (0, cb + c)),
                pl.BlockSpec((1, cw), lambda bi, t, c: (0, c))]
    return pl.pallas_call(
        functools.partial(_gdn_conv_kernel, l2norm=l2norm),
        grid=(b, nt, width // cw),
        in_specs=in_specs,
        out_specs=pl.BlockSpec((1, tm, cw), lambda bi, t, c: (bi, t, c)),
        out_shape=jax.ShapeDtypeStruct((b, s, width), F32),
        scratch_shapes=[pltpu.VMEM((tm + 16, cw), F32)],
        compiler_params=_cparams(("parallel", "parallel", "parallel")),
        name="gdn_conv_norm" if l2norm else "gdn_conv",
    )(z, z, z, conv_w, scale)


def _gdn_gates_kernel(zs_ref, na_ref, dtb_ref, o_ref):
    zs = zs_ref[0]
    tm = zs.shape[0]
    gg = na_ref[...] * _softplus(zs + dtb_ref[...])
    gam_f = _mask_dot(_order_mask(tm, False, False, CHUNK), gg)
    gam_b = _mask_dot(_order_mask(tm, True, False, CHUNK), gg)
    lane = _iota2(zs.shape, 1)
    o_ref[0] = jnp.where(lane < 2 * GDN_VHEADS, _sigmoid(zs), jnp.where(lane < 3 * GDN_VHEADS, gam_f, gam_b))


def gdn_gates(zs, neg_a, dtb, tm):
    b, s, w = zs.shape
    blk = pl.BlockSpec((1, tm, w), lambda bi, t: (bi, t, 0))
    row = pl.BlockSpec((1, w), lambda bi, t: (0, 0))
    return pl.pallas_call(
        _gdn_gates_kernel,
        grid=(b, s // tm),
        in_specs=[blk, row, row],
        out_specs=blk,
        out_shape=jax.ShapeDtypeStruct((b, s, w), F32),
        compiler_params=_cparams(("parallel", "parallel")),
        name="gdn_gates",
    )(zs, neg_a, dtb)


def _gdn_chunks(q, k, v2, beta2, gam2, revs, s_ref):
    nb, c, _ = q.shape
    n2 = 2 * c
    per_batch = lambda fn: jnp.stack([fn(r) for r in revs], axis=0)
    incl = per_batch(lambda r: _order_mask(n2, r, False, c))
    strict = per_batch(lambda r: _order_mask(n2, r, True, c))
    gam_c = jnp.broadcast_to(gam2, (nb, n2, n2))
    gam_r = jnp.swapaxes(gam_c, 1, 2)
    tot2 = jnp.stack([jnp.concatenate(
        [jnp.broadcast_to(gam2[j, h * c + (0 if r else c - 1)][None], (c, 1)) for h in range(2)], axis=0)
        for j, r in enumerate(revs)], axis=0)
    diff = gam_c - gam_r
    dec_s = jnp.where(strict, jnp.exp(jnp.where(strict, diff, 0.0)), 0.0)
    dec_i = jnp.where(incl, jnp.exp(jnp.where(incl, diff, 0.0)), 0.0)
    k2 = jnp.concatenate([k, k], axis=1)
    q2 = jnp.concatenate([q, q], axis=1)
    gram = _bdot_nt(jnp.concatenate([k2, q2], axis=1), k2)
    a_mat = gram[:, :n2] * beta2 * dec_s
    qk = gram[:, n2:] * dec_i
    t_inv = _neumann_inverse(-a_mat)
    e_gam = jnp.exp(gam2)
    uw = _bdot(t_inv, jnp.concatenate([v2 * beta2, k2 * (beta2 * e_gam)], axis=2))
    u2 = uw[:, :, :GDN_DV]
    w2 = uw[:, :, GDN_DV:]
    qd2 = q2 * e_gam
    ke2 = k2 * jnp.exp(tot2 - gam2)
    dl2 = jnp.exp(tot2)
    ws, qs = [], []
    for e in range(2):
        rows = slice(e * c, (e + 1) * c)
        both = _bdot(jnp.concatenate([w2[:, rows], qd2[:, rows]], axis=1), s_ref[e])
        ws.append(both[:, :c])
        qs.append(both[:, c:])
    vnew2 = u2 - jnp.concatenate(ws, axis=1)
    o2 = jnp.concatenate(qs, axis=1) + _bdot(qk, vnew2)
    for e in range(2):
        rows = slice(e * c, (e + 1) * c)
        s_ref[e] = s_ref[e] * dl2[:, e * c:e * c + 1, :] + _bdot_tn(ke2[:, rows], vnew2[:, rows])
    return o2


def _gdn_scan_kernel(qf, kf, vf, gf, qb, kb, vb, gb, of_ref, ob_ref, s_ref, *, heads):
    @pl.when(pl.program_id(2) == 0)
    def _():
        s_ref[...] = jnp.zeros_like(s_ref)

    c = CHUNK
    sub_steps = qf.shape[1] // c
    lane = _iota2((c, LANES), 1)

    def column(gates, idx):
        return jnp.sum(jnp.where(lane == idx, gates, 0.0), axis=1, keepdims=True)

    for sub in range(sub_steps):
        rows_of = (slice(sub * c, (sub + 1) * c), slice((sub_steps - 1 - sub) * c, (sub_steps - sub) * c))
        qs, ks, vs, betas, gams, revs = [], [], [], [], [], []
        for d, (q_ref, k_ref, v_ref, g_ref) in enumerate(((qf, kf, vf, gf), (qb, kb, vb, gb))):
            rows = rows_of[d]
            gates = g_ref[0, rows, :]
            for i in range(heads):
                vh = 2 * (pl.program_id(1) * heads + i)
                ksl = slice(i * GDN_DK, (i + 1) * GDN_DK)
                qs.append(q_ref[0, rows, ksl])
                ks.append(k_ref[0, rows, ksl])
                vs.append(jnp.concatenate(
                    [v_ref[0, rows, (2 * i + e) * GDN_DV:(2 * i + e + 1) * GDN_DV] for e in range(2)], axis=0))
                betas.append(jnp.concatenate([column(gates, d * GDN_VHEADS + vh + e) for e in range(2)], axis=0))
                gams.append(jnp.concatenate([column(gates, (2 + d) * GDN_VHEADS + vh + e) for e in range(2)], axis=0))
                revs.append(d == 1)
        o2 = _gdn_chunks(jnp.stack(qs), jnp.stack(ks), jnp.stack(vs), jnp.stack(betas), jnp.stack(gams), revs, s_ref)
        for d, o_ref in enumerate((of_ref, ob_ref)):
            for i in range(heads):
                j = d * heads + i
                for e in range(2):
                    o_ref[0, rows_of[d], (2 * i + e) * GDN_DV:(2 * i + e + 1) * GDN_DV] = (
                        o2[j, e * c:(e + 1) * c].astype(o_ref.dtype))


def gdn_scan(qk, v, gates, heads):
    b, s, _ = v.shape
    c = CHUNK * SCAN_CHUNKS_PER_STEP
    n = s // c
    ng = GDN_KHEADS // heads

    def spec(w, off, rev):
        if rev:
            return pl.BlockSpec((1, c, w), lambda bi, h, t: (bi, n - 1 - t, off + h))
        return pl.BlockSpec((1, c, w), lambda bi, h, t: (bi, t, off + h))

    def gspec(rev):
        if rev:
            return pl.BlockSpec((1, c, LANES), lambda bi, h, t: (bi, n - 1 - t, 0))
        return pl.BlockSpec((1, c, LANES), lambda bi, h, t: (bi, t, 0))

    kw = heads * GDN_DK
    vw = heads * 2 * GDN_DV
    in_specs = [spec(kw, 0, False), spec(kw, ng, False), spec(vw, 0, False), gspec(False),
                spec(kw, 0, True), spec(kw, ng, True), spec(vw, 0, True), gspec(True)]
    return pl.pallas_call(
        functools.partial(_gdn_scan_kernel, heads=heads),
        grid=(b, ng, n),
        in_specs=in_specs,
        out_specs=[spec(vw, 0, False), spec(vw, 0, True)],
        out_shape=[jax.ShapeDtypeStruct((b, s, GDN_VW), BF16)] * 2,
        scratch_shapes=[pltpu.VMEM((2, 2 * heads, GDN_DK, GDN_DV), F32)],
        compiler_params=_cparams(("parallel", "parallel", "arbitrary")),
        name="gdn_scan",
    )(qk, qk, v, gates, qk, qk, v, gates)


def _gdn_post_kernel(of_ref, ob_ref, gate_ref, nw_ref, y_ref):
    nw = nw_ref[...]
    for h in range(of_ref.shape[2] // GDN_DV):
        sl = slice(h * GDN_DV, (h + 1) * GDN_DV)
        o = of_ref[0, :, sl].astype(F32) + ob_ref[0, :, sl].astype(F32)
        ms = jnp.mean(o * o, axis=-1, keepdims=True)
        y = o * lax.rsqrt(ms + EPS) * nw
        y_ref[0, :, sl] = (y * _silu(gate_ref[0, :, sl])).astype(y_ref.dtype)


def gdn_post(o_f, o_b, z, gate_off, norm_w, tm, cw):
    b, s, w = o_f.shape
    blk = pl.BlockSpec((1, tm, cw), lambda bi, t, c: (bi, t, c))
    gb = gate_off // cw
    return pl.pallas_call(
        _gdn_post_kernel,
        grid=(b, s // tm, w // cw),
        in_specs=[blk, blk, pl.BlockSpec((1, tm, cw), lambda bi, t, c: (bi, t, gb + c)),
                  pl.BlockSpec((1, GDN_DV), lambda bi, t, c: (0, 0))],
        out_specs=blk,
        out_shape=jax.ShapeDtypeStruct((b, s, w), BF16),
        compiler_params=_cparams(("parallel", "parallel", "parallel")),
        name="gdn_post",
    )(o_f, o_b, z, norm_w.reshape(1, GDN_DV))


def _moe_router_kernel(x_ref, g_ref, w_ref, b_ref, route_ref, counts_ref):
    @pl.when(pl.program_id(0) == 0)
    def _():
        counts_ref[...] = jnp.zeros_like(counts_ref)

    x = x_ref[...]
    tm = x.shape[0]
    ms = jnp.mean(x * x, axis=-1, keepdims=True)
    h = x * lax.rsqrt(ms + EPS) * g_ref[...]
    lane_i = _iota2((tm, LANES), 1)
    lane = lane_i.astype(F32)
    lane_grp = lax.shift_right_logical(lane_i, 3).astype(F32)
    neg = -jnp.inf
    logits = _dot3(h, w_ref[...]) + b_ref[...]
    gl = jnp.where((lane_i >= N_EXPERTS) & (lane_i < N_EXPERTS + N_GROUPS), logits, neg)
    gmax = jnp.max(gl, axis=-1, keepdims=True)
    gidx = jnp.min(jnp.where(gl == gmax, lane, float(LANES)), axis=-1, keepdims=True) - float(N_EXPERTS)
    grp_w = 1.0 / jnp.sum(jnp.exp(gl - gmax), axis=-1, keepdims=True)
    sel = jnp.where((lane_i < N_EXPERTS) & (lane_grp == gidx), logits, neg)
    m1 = jnp.max(sel, axis=-1, keepdims=True)
    i1 = jnp.min(jnp.where(sel == m1, lane, float(LANES)), axis=-1, keepdims=True)
    sel2 = jnp.where(lane == i1, neg, sel)
    m2 = jnp.max(sel2, axis=-1, keepdims=True)
    i2 = jnp.min(jnp.where(sel2 == m2, lane, float(LANES)), axis=-1, keepdims=True)
    e2 = jnp.exp(m2 - m1)
    w1 = grp_w / (1.0 + e2)
    w2 = grp_w * e2 / (1.0 + e2)
    hits = jnp.where((lane == i1) | (lane == i2), 1.0, 0.0)
    before = counts_ref[...] + _bdot(_order_mask(tm, False, True, tm), hits)
    r1 = jnp.sum(jnp.where(lane == i1, before, 0.0), axis=-1, keepdims=True)
    r2 = jnp.sum(jnp.where(lane == i2, before, 0.0), axis=-1, keepdims=True)
    counts_ref[...] += jnp.sum(hits, axis=0, keepdims=True)
    route_ref[...] = jnp.where(lane_i == 0, i1, jnp.where(lane_i == 1, i2, jnp.where(
        lane_i == 2, w1, jnp.where(lane_i == 3, w2, jnp.where(lane_i == 4, r1, jnp.where(lane_i == 5, r2, 0.0))))))


def moe_router(x, g, w_group, b_group, w_router, b_router, tm):
    m, d = x.shape
    tm = min(tm, m)
    pad = lambda w: jnp.pad(w, ((0, 0), (0, LANES - w.shape[1])))
    full = lambda arr: pl.BlockSpec(arr.shape, lambda i: (0, 0))
    params = [g.reshape(1, d), pad(jnp.concatenate([w_router, w_group], axis=1)),
              pad(jnp.concatenate([b_router, b_group]).reshape(1, -1))]
    return pl.pallas_call(
        _moe_router_kernel,
        grid=(m // tm,),
        in_specs=[pl.BlockSpec((tm, d), lambda i: (i, 0))] + [full(p) for p in params],
        out_specs=[pl.BlockSpec((tm, LANES), lambda i: (i, 0)), pl.BlockSpec((1, LANES), lambda i: (0, 0))],
        out_shape=[jax.ShapeDtypeStruct((m, LANES), F32), jax.ShapeDtypeStruct((1, LANES), F32)],
        compiler_params=_cparams(("arbitrary",)),
        name="moe_router",
    )(x, *params)


def _route_tables(route, counts, tm, n_tiles):
    counts = counts[0, :N_EXPERTS].astype(jnp.int32)
    padded = ((counts + tm - 1) // tm) * tm
    ends = jnp.cumsum(padded)
    starts = ends - padded
    experts = route[:, 0:2].astype(jnp.int32)
    pos = (starts[experts] + route[:, 4:6].astype(jnp.int32)).reshape(-1)
    tile_start = jnp.arange(n_tiles, dtype=jnp.int32) * tm
    tile_e = jnp.minimum(jnp.searchsorted(ends, tile_start, side="right"), N_EXPERTS - 1).astype(jnp.int32)
    n_valid = (ends[-1:] // tm).astype(jnp.int32)
    fill = jnp.stack([jnp.maximum(ends - tm, 0), (padded > 0).astype(jnp.int32)]).astype(jnp.int32)
    return pos, tile_e, n_valid, fill


def _moe_dispatch_kernel(pos_ref, nvalid_ref, fill_ref, x_ref, g_ref, hs_hbm, hbuf, zbuf, sem_rows, sem_fill, *,
                         n_tiles):
    i = pl.program_id(0)
    tm = x_ref.shape[0]
    row_tile = zbuf.shape[0]

    @pl.when(i == 0)
    def _():
        zbuf[...] = jnp.zeros_like(zbuf)
        fills = [(fill_ref[1, e] > 0, fill_ref[0, e]) for e in range(N_EXPERTS)]
        fills += [(nvalid_ref[0] + k < n_tiles, (nvalid_ref[0] + k) * row_tile) for k in range(N_EXPERTS)]
        fills = [(cond, pl.multiple_of(start, row_tile)) for cond, start in fills]
        for cond, start in fills:
            @pl.when(cond)
            def _(start=start):
                pltpu.make_async_copy(zbuf, hs_hbm.at[pl.ds(start, row_tile)], sem_fill).start()
        for cond, start in fills:
            @pl.when(cond)
            def _(start=start):
                pltpu.make_async_copy(zbuf, hs_hbm.at[pl.ds(start, row_tile)], sem_fill).wait()

    x = x_ref[...]
    ms = jnp.mean(x * x, axis=-1, keepdims=True)
    hbuf[...] = x * lax.rsqrt(ms + EPS) * g_ref[...]

    def issue(r, carry):
        base = 2 * (i * tm + r)
        for slot in range(2):
            pltpu.make_async_copy(hbuf.at[pl.ds(r, 1)], hs_hbm.at[pl.ds(pos_ref[base + slot], 1)], sem_rows).start()
        return carry

    lax.fori_loop(0, tm, issue, 0, unroll=8)
    for slot in range(2):
        pltpu.make_async_copy(hbuf, hs_hbm.at[pl.ds(0, tm)], sem_rows).wait()


def moe_dispatch(x, g, pos, n_valid, fill, n_tiles, tm):
    m, d = x.shape
    tm_x = min(256, m)
    return pl.pallas_call(
        functools.partial(_moe_dispatch_kernel, n_tiles=n_tiles),
        grid_spec=pltpu.PrefetchScalarGridSpec(
            num_scalar_prefetch=3,
            grid=(m // tm_x,),
            in_specs=[pl.BlockSpec((tm_x, d), lambda i, p, nv, fl: (i, 0)),
                      pl.BlockSpec((1, d), lambda i, p, nv, fl: (0, 0))],
            out_specs=pl.BlockSpec(memory_space=pl.ANY),
            scratch_shapes=[pltpu.VMEM((tm_x, d), F32), pltpu.VMEM((tm, d), F32),
                            pltpu.SemaphoreType.DMA(()), pltpu.SemaphoreType.DMA(())]),
        out_shape=jax.ShapeDtypeStruct((n_tiles * tm, d), F32),
        compiler_params=_cparams(("arbitrary",)),
        name="moe_dispatch",
    )(pos, n_valid, fill, x, g.reshape(1, d))


def _moe_ffn_kernel(tile_e_ref, nvalid_ref, h_ref, wg_ref, wu_ref, wd_ref, y_ref, wg_b, wu_b, wd_b):
    i = pl.program_id(0)

    @pl.when((i == 0) | (tile_e_ref[i] != tile_e_ref[jnp.maximum(i - 1, 0)]))
    def _():
        wg_b[...] = wg_ref[0, 0].astype(BF16)
        wu_b[...] = wu_ref[0, 0].astype(BF16)
        wd_b[...] = wd_ref[0, 0].astype(BF16)

    @pl.when(i < nvalid_ref[0])
    def _():
        x = h_ref[...].astype(BF16)
        hid = _silu(_dot(x, wg_b[...])) * _dot(x, wu_b[...])
        y_ref[...] = _dot(hid.astype(BF16), wd_b[...])

    @pl.when(i >= nvalid_ref[0])
    def _():
        y_ref[...] = jnp.zeros_like(y_ref)


def moe_ffn(h_sorted, tile_e, n_valid, w_gate, w_up, w_down, layer, tm):
    _, d = h_sorted.shape
    n_tiles = tile_e.shape[0]
    ff = w_gate.shape[-1]
    wspec = lambda shape: pl.BlockSpec((1, 1) + shape, lambda i, te, nv: (layer, te[i], 0, 0))
    return pl.pallas_call(
        _moe_ffn_kernel,
        grid_spec=pltpu.PrefetchScalarGridSpec(
            num_scalar_prefetch=2,
            grid=(n_tiles,),
            in_specs=[pl.BlockSpec((tm, d), lambda i, te, nv: (i, 0)), wspec((d, ff)), wspec((d, ff)), wspec((ff, d))],
            out_specs=pl.BlockSpec((tm, d), lambda i, te, nv: (i, 0)),
            scratch_shapes=[pltpu.VMEM((d, ff), BF16), pltpu.VMEM((d, ff), BF16), pltpu.VMEM((ff, d), BF16)]),
        out_shape=jax.ShapeDtypeStruct((n_tiles * tm, d), F32),
        compiler_params=_cparams(("arbitrary",)),
        name="moe_ffn",
    )(tile_e, n_valid, h_sorted, w_gate, w_up, w_down)


def _moe_combine_kernel(pos_ref, x_ref, route_ref, y_hbm, g_ref, o_ref, ybuf, sem, *, final_norm):
    i = pl.program_id(0)
    tm = x_ref.shape[0]

    def issue(r, carry):
        base = 2 * (i * tm + r)
        for slot in range(2):
            pltpu.make_async_copy(y_hbm.at[pl.ds(pos_ref[base + slot], 1)], ybuf.at[slot, pl.ds(r, 1)], sem).start()
        return carry

    lax.fori_loop(0, tm, issue, 0, unroll=8)
    for slot in range(2):
        pltpu.make_async_copy(y_hbm.at[pl.ds(0, tm)], ybuf.at[slot], sem).wait()
    route = route_ref[...]
    o = x_ref[...] + route[:, 2:3] * ybuf[0] + route[:, 3:4] * ybuf[1]
    if final_norm:
        ms = jnp.mean(o * o, axis=-1, keepdims=True)
        o = o * lax.rsqrt(ms + EPS) * g_ref[...]
    o_ref[...] = o


def moe_combine(x, route, y_sorted, pos, norm_g, tm):
    m, d = x.shape
    tm = min(tm, m)
    final_norm = norm_g is not None
    g = (norm_g if final_norm else jnp.ones((d,), F32)).reshape(1, d)
    return pl.pallas_call(
        functools.partial(_moe_combine_kernel, final_norm=final_norm),
        grid_spec=pltpu.PrefetchScalarGridSpec(
            num_scalar_prefetch=1,
            grid=(m // tm,),
            in_specs=[pl.BlockSpec((tm, d), lambda i, p: (i, 0)), pl.BlockSpec((tm, LANES), lambda i, p: (i, 0)),
                      pl.BlockSpec(memory_space=pl.ANY), pl.BlockSpec((1, d), lambda i, p: (0, 0))],
            out_specs=pl.BlockSpec((tm, d), lambda i, p: (i, 0)),
            scratch_shapes=[pltpu.VMEM((2, tm, d), F32), pltpu.SemaphoreType.DMA(())]),
        out_shape=jax.ShapeDtypeStruct((m, d), F32),
        compiler_params=_cparams(("arbitrary",)),
        name="moe_combine",
    )(pos, x, route, y_sorted, g)


def _pad_rows(w, rows):
    return jnp.pad(w, ((0, rows - w.shape[0]), (0, 0)))


def _even_layer(x, norm_g, w_in, gla_w_alpha, gla_b_alpha, gla_norm, mu, w0, w2, a0, a2, g2,
                k_k, k_a, r_k, ln_w, ln_b, w_out):
    b, s, d = x.shape
    gla_cols, rw = w_in[:, :3104], w_in[:, 3104:]
    mu_g = lambda lo, hi, width: jnp.pad(mu[:, lo:hi], ((0, 0), (0, width - (hi - lo))))
    pad_c = lambda w, width: jnp.pad(w, ((0, 0), (0, width - w.shape[1])))
    w_cat = jnp.concatenate([
        gla_cols[:, 0:3072],
        rw[:, 0:3072], rw[:, 3360:3616],
        pad_c(rw[:, 3072:3168], LANES), pad_c(rw[:, 3168:3264], LANES), pad_c(rw[:, 3264:3360], LANES),
        pad_c(gla_cols[:, 3072:3104], LANES)], axis=1).astype(BF16)
    mu_cat = jnp.concatenate([
        mu[:, 0:3072], mu[:, 3360:3616], mu_g(3072, 3168, LANES), mu_g(3168, 3264, LANES),
        mu_g(3264, 3360, LANES)], axis=1)
    z = norm_matmul(x.reshape(b * s, d), norm_g, w_cat, 512, 768).reshape(b, s, EVEN_PAD)

    wa = jnp.stack([_pad_rows(gla_w_alpha[0], LANES),
                    jnp.pad(gla_w_alpha[1], ((GLA_LOWRANK, LANES - 2 * GLA_LOWRANK), (0, 0)))])
    gla_f, gla_b = gla_scan(z, wa, gla_b_alpha)
    y_gla = gla_post(gla_f, gla_b, z, gla_norm, 256)

    head_of_lane = jnp.arange(RWKV_WIDTH) // RWKV_HEAD
    seg_e = (head_of_lane[:, None] == jnp.arange(LANES)[None, :]).astype(F32)
    seg_et = seg_e.T
    row = lambda p: p.reshape(1, RWKV_WIDTH)
    r, k, v, kk, bb, cum_f, exc_f, cum_b, exc_b, g, bonus = rwkv_prep(
        z, mu_cat, w0, _pad_rows(w2[0], LANES), _pad_rows(w2[1], LANES), row(a0), _pad_rows(a2, LANES), g2,
        row(k_k), row(k_a), row(r_k), seg_e, seg_et, 128)
    rw_f, rw_b = rwkv_scan(r, k, v, kk, bb, cum_f, exc_f, cum_b, exc_b, RWKV_PAIRS_PER_STEP)
    y_rwkv = rwkv_post(rw_f, rw_b, bonus, g, ln_w, ln_b, seg_e, seg_et, 256)

    w_out = w_out.astype(BF16)
    out = proj_residual([y_gla.reshape(b * s, GLA_WIDTH), y_rwkv.reshape(b * s, RWKV_WIDTH)],
                        [w_out[:GLA_WIDTH], w_out[GLA_WIDTH:]], x.reshape(b * s, d), 512, 1024)
    return out.reshape(b, s, d)


def _odd_layer(x, norm_g, w_in, conv_w, a_log, dt_bias, norm_w, w_out):
    b, s, d = x.shape
    x2 = x.reshape(b * s, d)
    main = GDN_QKV + GDN_VW
    z = norm_matmul(x2, norm_g, w_in[:, :main].astype(BF16), 512, 1024).reshape(b, s, main)
    zs = norm_matmul(x2, norm_g, w_in[:, main:].astype(BF16), 512, LANES).reshape(b, s, LANES)
    scale = jnp.concatenate([jnp.full((1, GDN_KW), GDN_DK ** -0.5, F32), jnp.ones((1, GDN_KW), F32)], axis=1)
    qk = gdn_conv(z, conv_w, scale, 0, 2 * GDN_KW, True, 256, 1024)
    v = gdn_conv(z, conv_w, scale, 2 * GDN_KW, GDN_VW, False, 256, 1024)
    zero = jnp.zeros((2 * GDN_VHEADS,), F32)
    neg_a = jnp.concatenate([zero, -jnp.exp(a_log.reshape(-1))]).reshape(1, LANES)
    dtb = jnp.concatenate([zero, dt_bias.reshape(-1)]).reshape(1, LANES)
    gates = gdn_gates(zs, neg_a, dtb, 256)
    o_f, o_b = gdn_scan(qk, v, gates, GDN_HEADS_PER_STEP)
    y = gdn_post(o_f, o_b, z, GDN_QKV, norm_w, 256, 1024)
    out = proj_residual([y.reshape(b * s, GDN_VW)], [w_out.astype(BF16)], x2, 512, 1024)
    return out.reshape(b, s, d)


def _moe_layer(x, norm_g, w_group, b_group, w_router, b_router, w_gate, w_up, w_down, layer, final_g):
    b, s, d = x.shape
    x2 = x.reshape(b * s, d)
    route, counts = moe_router(x2, norm_g, w_group, b_group, w_router, b_router, 256)
    n_tiles = -(-2 * b * s // MOE_ROW_TILE) + N_EXPERTS
    pos, tile_e, n_valid, fill = _route_tables(route, counts, MOE_ROW_TILE, n_tiles)
    h_sorted = moe_dispatch(x2, norm_g, pos, n_valid, fill, n_tiles, MOE_ROW_TILE)
    y_sorted = moe_ffn(h_sorted, tile_e, n_valid, w_gate, w_up, w_down, layer, MOE_ROW_TILE)
    out = moe_combine(x2, route, y_sorted, pos, final_g, 256)
    return out.reshape(b, s, d)


def kernel(x, norm_mix, norm_ffn, norm_final, ev_w_in, ev_gla_w_alpha, ev_gla_b_alpha, ev_gla_norm, ev_rwkv_mu, ev_rwkv_w0, ev_rwkv_w2, ev_rwkv_a0, ev_rwkv_a2, ev_rwkv_g2, ev_rwkv_k_k, ev_rwkv_k_a, ev_rwkv_r_k, ev_rwkv_ln_w, ev_rwkv_ln_b, ev_w_out, od_w_in, od_conv, od_a_log, od_dt_bias, od_norm, od_w_out, moe_w_group, moe_b_group, moe_w_router, moe_b_router, moe_w_gate, moe_w_up, moe_w_down):
    depth = norm_mix.shape[0]
    for i in range(depth):
        j = i // 2
        if i % 2 == 0:
            x = _even_layer(x, norm_mix[i], ev_w_in[j], ev_gla_w_alpha[j], ev_gla_b_alpha[j], ev_gla_norm[j],
                            ev_rwkv_mu[j], ev_rwkv_w0[j], ev_rwkv_w2[j], ev_rwkv_a0[j], ev_rwkv_a2[j],
                            ev_rwkv_g2[j], ev_rwkv_k_k[j], ev_rwkv_k_a[j], ev_rwkv_r_k[j],
                            ev_rwkv_ln_w[j], ev_rwkv_ln_b[j], ev_w_out[j])
        else:
            x = _odd_layer(x, norm_mix[i], od_w_in[j], od_conv[j], od_a_log[j], od_dt_bias[j],
                           od_norm[j], od_w_out[j])
        x = _moe_layer(x, norm_ffn[i], moe_w_group[i], moe_b_group[i], moe_w_router[i], moe_b_router[i],
                       moe_w_gate, moe_w_up, moe_w_down, i, norm_final if i == depth - 1 else None)
    return x
```

```python
import functools
import math

import jax
import jax.numpy as jnp
from jax import lax
from jax.experimental import pallas as pl
from jax.experimental.pallas import tpu as pltpu

F32 = jnp.float32
BF16 = jnp.bfloat16
HI = lax.Precision.HIGHEST

EPS = 1e-6
CHUNK = 64
LANES = 128
VMEM_LIMIT = 56 * 1024 * 1024

D_MODEL = 2048
GLA_HEADS = 4
GLA_DK = 128
GLA_DV = 256
GLA_KW = 512
GLA_WIDTH = 1024
GLA_LOWRANK = 16
GLA_TAU = 16.0
RWKV_HEAD = 64
RWKV_WIDTH = 1024
RWKV_HEADS = 16
DECAY_LORA = 96
ICLR_LORA = 96
GATE_LORA = 256
RWKV_LN_EPS = 64e-5
GDN_DK = 128
GDN_DV = 128
GDN_KHEADS = 16
GDN_VHEADS = 32
GDN_KW = 2048
GDN_VW = 4096
GDN_QKV = 8192
SCAN_CHUNKS_PER_STEP = 4
RWKV_PAIRS_PER_STEP = 4
GDN_HEADS_PER_STEP = 4
MOE_ROW_TILE = 256
N_GROUPS = 4
EXPERTS_PER_GROUP = 8
N_EXPERTS = 32
EXPERT_FF = 512

GQ_OFF, GK_OFF, GV_OFF, GG_OFF = 0, 512, 1024, 2048
R_OFF, K_OFF, V_OFF = 3072, 4096, 5120
GLR_OFF, WLF_OFF, WLB_OFF, ALR_OFF, AL_OFF = 6144, 6400, 6528, 6656, 6784
RWKV_MAIN = 3072
RWKV_SMALL = 768
EVEN_PAD = 6912


def _cparams(sem):
    return pltpu.CompilerParams(dimension_semantics=sem, vmem_limit_bytes=VMEM_LIMIT)


def _mm(a, b, ca, cb, precision):
    if a.ndim == 3:
        dims = (((ca + 1,), (cb + 1,)), ((0,), (0,)))
    else:
        dims = (((ca,), (cb,)), ((), ()))
    return lax.dot_general(a, b, dims, preferred_element_type=F32, precision=precision)


def _dot(a, b, precision=None):
    return _mm(a, b, 1, 0, precision)


def _dot_nt(a, b, precision=None):
    return _mm(a, b, 1, 1, precision)


def _dot_tn(a, b, precision=None):
    return _mm(a, b, 0, 0, precision)


def _bf16_pieces(x, n):
    pieces = []
    for _ in range(n - 1):
        p = x.astype(BF16)
        pieces.append(p)
        x = x - p.astype(F32)
    pieces.append(x.astype(BF16))
    return pieces


def _mask_dot(mask, x, pieces=3):
    mb = mask.astype(BF16)
    return sum(_dot(mb, p) for p in _bf16_pieces(x, pieces))


def _dot_mask(x, mask, pieces=2):
    mb = mask.astype(BF16)
    return sum(_dot(p, mb) for p in _bf16_pieces(x, pieces))


def _dot3(a, b):
    ah, al = _bf16_pieces(a, 2)
    bh, bl = _bf16_pieces(b, 2)
    return _dot(ah, bh) + _dot(ah, bl) + _dot(al, bh)


def _bdot(a, b):
    return _dot(a.astype(BF16), b.astype(BF16))


def _bdot_nt(a, b):
    return _dot_nt(a.astype(BF16), b.astype(BF16))


def _bdot_tn(a, b):
    return _dot_tn(a.astype(BF16), b.astype(BF16))


def _sigmoid(x):
    return 1.0 / (1.0 + jnp.exp(-x))


def _silu(x):
    return x * _sigmoid(x)


def _softplus(x):
    return jnp.maximum(x, 0.0) + jnp.log(1.0 + jnp.exp(-jnp.abs(x)))


def _iota2(shape, dim):
    return lax.broadcasted_iota(jnp.int32, shape, dim)


def _order_mask(n, rev, strict, block):
    i = _iota2((n, n), 0)
    j = _iota2((n, n), 1)
    if rev:
        m = (j > i) if strict else (j >= i)
    else:
        m = (j < i) if strict else (j <= i)
    if block < n:
        sh = block.bit_length() - 1
        m = m & (lax.shift_right_logical(i, sh) == lax.shift_right_logical(j, sh))
    return m


def _neumann_inverse(nmat):
    n = nmat.shape[-1]
    eye = (_iota2((n, n), 0) == _iota2((n, n), 1)).astype(F32)
    t = eye + nmat
    nb = nmat.astype(BF16)
    p = _dot(nb, nb)
    for step in range(5):
        pb = p.astype(BF16)
        if step < 4:
            both = _dot(pb, jnp.concatenate([pb, t.astype(BF16)], axis=-1))
            p = both[..., :n]
            t = t + both[..., n:]
        else:
            t = t + _dot(pb, t.astype(BF16))
    return t


def _norm_matmul_kernel(x_ref, g_ref, w_ref, o_ref, h_ref):
    @pl.when(pl.program_id(1) == 0)
    def _():
        x = x_ref[...]
        ms = jnp.mean(x * x, axis=-1, keepdims=True)
        h_ref[...] = (x * lax.rsqrt(ms + EPS) * g_ref[...]).astype(BF16)

    o_ref[...] = _dot(h_ref[...], w_ref[...]).astype(o_ref.dtype)


def norm_matmul(x, g, w, tm, tn, out_dtype=F32):
    m, d = x.shape
    tm = min(tm, m)
    n = w.shape[1]
    return pl.pallas_call(
        _norm_matmul_kernel,
        grid=(m // tm, n // tn),
        in_specs=[pl.BlockSpec((tm, d), lambda i, j: (i, 0)),
                  pl.BlockSpec((1, d), lambda i, j: (0, 0)),
                  pl.BlockSpec((d, tn), lambda i, j: (0, j))],
        out_specs=pl.BlockSpec((tm, tn), lambda i, j: (i, j)),
        out_shape=jax.ShapeDtypeStruct((m, n), out_dtype),
        scratch_shapes=[pltpu.VMEM((tm, d), BF16)],
        compiler_params=_cparams(("parallel", "arbitrary")),
        name="norm_matmul",
    )(x, g.reshape(1, d), w)


def _proj_residual_kernel(*refs, n_lhs):
    x_ref = refs[2 * n_lhs]
    o_ref = refs[2 * n_lhs + 1]
    acc = x_ref[...]
    for t in range(n_lhs):
        acc = acc + _dot(refs[t][...], refs[n_lhs + t][...])
    o_ref[...] = acc


def proj_residual(ys, ws, x, tm, tn):
    m, n = x.shape
    tm = min(tm, m)
    n_lhs = len(ys)
    in_specs = [pl.BlockSpec((tm, y.shape[1]), lambda i, j: (i, 0)) for y in ys]
    in_specs += [pl.BlockSpec((w.shape[0], tn), lambda i, j: (0, j)) for w in ws]
    in_specs += [pl.BlockSpec((tm, tn), lambda i, j: (i, j))]
    return pl.pallas_call(
        functools.partial(_proj_residual_kernel, n_lhs=n_lhs),
        grid=(m // tm, n // tn),
        in_specs=in_specs,
        out_specs=pl.BlockSpec((tm, tn), lambda i, j: (i, j)),
        out_shape=jax.ShapeDtypeStruct((m, n), F32),
        compiler_params=_cparams(("parallel", "arbitrary")),
        name="proj_residual",
    )(*ys, *ws, x)


def _final_norm_kernel(x_ref, g_ref, o_ref):
    x = x_ref[...]
    ms = jnp.mean(x * x, axis=-1, keepdims=True)
    o_ref[...] = x * lax.rsqrt(ms + EPS) * g_ref[...]


def final_norm(x, g, tm):
    m, d = x.shape
    tm = min(tm, m)
    return pl.pallas_call(
        _final_norm_kernel,
        grid=(m // tm,),
        in_specs=[pl.BlockSpec((tm, d), lambda i: (i, 0)), pl.BlockSpec((1, d), lambda i: (0, 0))],
        out_specs=pl.BlockSpec((tm, d), lambda i: (i, 0)),
        out_shape=jax.ShapeDtypeStruct((m, d), F32),
        compiler_params=_cparams(("parallel",)),
        name="final_norm",
    )(x, g.reshape(1, d))


def _gla_chunks(q, k, v, cum, revs, s_ref):
    c = q.shape[1]
    tot = jnp.stack([cum[j, (0 if rev else c - 1)][None] for j, rev in enumerate(revs)], axis=0)
    incl = jnp.stack([_order_mask(c, rev, False, c) for rev in revs], axis=0)
    q_dec = q * ((GLA_DK ** -0.5) * jnp.exp(cum))
    k_dec = k * jnp.exp(-cum)
    k_end = k * jnp.exp(tot - cum)
    scores = jnp.where(incl, _bdot_nt(q_dec, k_dec), 0.0)
    state = s_ref[...]
    o = _bdot(scores, v) + _bdot_nt(q_dec, state)
    s_ref[...] = jnp.exp(tot) * state + _bdot_tn(v, k_end)
    return o


def _gla_kernel(qf, kf, vf, alf, qb, kb, vb, alb, wa_ref, ba_ref, of_ref, ob_ref, s_ref):
    @pl.when(pl.program_id(1) == 0)
    def _():
        s_ref[...] = jnp.zeros_like(s_ref)

    c = qf.shape[1]
    qs, ks, vs, cums, revs = [], [], [], [], []
    for d, (q_ref, k_ref, v_ref, al_ref) in enumerate(((qf, kf, vf, alf), (qb, kb, vb, alb))):
        pre = _dot3(al_ref[0], wa_ref[d]) + ba_ref[d:d + 1]
        log_a = (jnp.minimum(pre, 0.0) - jnp.log(1.0 + jnp.exp(-jnp.abs(pre)))) * (1.0 / GLA_TAU)
        cum = _mask_dot(_order_mask(c, d == 1, False, c), log_a)
        for h in range(GLA_HEADS):
            ksl = slice(h * GLA_DK, (h + 1) * GLA_DK)
            qs.append(q_ref[0, :, ksl])
            ks.append(k_ref[0, :, ksl])
            vs.append(v_ref[0, :, h * GLA_DV:(h + 1) * GLA_DV])
            cums.append(cum[:, ksl])
            revs.append(d == 1)
    o = _gla_chunks(jnp.stack(qs), jnp.stack(ks), jnp.stack(vs), jnp.stack(cums), revs, s_ref)
    for d, o_ref in enumerate((of_ref, ob_ref)):
        for h in range(GLA_HEADS):
            o_ref[0, :, h * GLA_DV:(h + 1) * GLA_DV] = o[d * GLA_HEADS + h].astype(o_ref.dtype)


def gla_scan(z, wa, ba):
    b, s, _ = z.shape
    n = s // CHUNK
    c = CHUNK

    def fwd(off, w):
        return pl.BlockSpec((1, c, w), lambda bi, t: (bi, t, off // w))

    def bwd(off, w):
        return pl.BlockSpec((1, c, w), lambda bi, t: (bi, n - 1 - t, off // w))

    in_specs = [fwd(GQ_OFF, GLA_KW), fwd(GK_OFF, GLA_KW), fwd(GV_OFF, GLA_WIDTH), fwd(AL_OFF, LANES),
                bwd(GQ_OFF, GLA_KW), bwd(GK_OFF, GLA_KW), bwd(GV_OFF, GLA_WIDTH), bwd(AL_OFF, LANES),
                pl.BlockSpec(wa.shape, lambda bi, t: (0, 0, 0)), pl.BlockSpec(ba.shape, lambda bi, t: (0, 0))]
    out_specs = [pl.BlockSpec((1, c, GLA_WIDTH), lambda bi, t: (bi, t, 0)),
                 pl.BlockSpec((1, c, GLA_WIDTH), lambda bi, t: (bi, n - 1 - t, 0))]
    return pl.pallas_call(
        _gla_kernel,
        grid=(b, n),
        in_specs=in_specs,
        out_specs=out_specs,
        out_shape=[jax.ShapeDtypeStruct((b, s, GLA_WIDTH), BF16)] * 2,
        scratch_shapes=[pltpu.VMEM((2 * GLA_HEADS, GLA_DV, GLA_DK), F32)],
        compiler_params=_cparams(("parallel", "arbitrary")),
        name="gla_scan",
    )(z, z, z, z, z, z, z, z, wa, ba)


def _gla_post_kernel(of_ref, ob_ref, gate_ref, nw_ref, y_ref):
    o = of_ref[0].astype(F32) + ob_ref[0].astype(F32)
    ms = jnp.mean(o * o, axis=-1, keepdims=True)
    y = o * lax.rsqrt(ms + EPS) * nw_ref[...]
    y_ref[0] = (y * _silu(gate_ref[0])).astype(y_ref.dtype)


def gla_post(o_f, o_b, z, norm_w, tm):
    b, s, w = o_f.shape
    blk = pl.BlockSpec((1, tm, GLA_DV), lambda bi, t, h: (bi, t, h))
    return pl.pallas_call(
        _gla_post_kernel,
        grid=(b, s // tm, GLA_HEADS),
        in_specs=[blk, blk, pl.BlockSpec((1, tm, GLA_DV), lambda bi, t, h: (bi, t, GG_OFF // GLA_DV + h)),
                  pl.BlockSpec((1, GLA_DV), lambda bi, t, h: (0, 0))],
        out_specs=blk,
        out_shape=jax.ShapeDtypeStruct((b, s, w), BF16),
        compiler_params=_cparams(("parallel", "parallel", "parallel")),
        name="gla_post",
    )(o_f, o_b, z, norm_w.reshape(1, GLA_DV))


def _seg_sum(x, e, et):
    return _dot_mask(_dot_mask(x, e), et)


def _rwkv_prep_kernel(z_ref, zp_ref, zn_ref, y_ref, yp_ref, yn_ref, mu_ref, w0_ref, w2f_ref, w2b_ref, a0_ref, a2_ref, g2_ref,
                      kk_ref, ka_ref, rk_ref, e_ref, et_ref,
                      r_out, k_out, v_out, kk_out, b_out, cumf_out, excf_out, cumb_out, excb_out, g_out, bonus_out):
    t = pl.program_id(1)
    tm = z_ref.shape[1]
    row = _iota2((tm, 1), 0)
    first = t == 0
    last = t == pl.num_programs(1) - 1

    def shifted(lo, hi):
        cur, prv, nxt, base = (z_ref, zp_ref, zn_ref, R_OFF) if lo < GLR_OFF else (y_ref, yp_ref, yn_ref, GLR_OFF)
        z = cur[0, :, lo - base:hi - base]
        prev_row = jnp.where(first, 0.0, prv[0, 7:8, lo - base:hi - base])
        next_row = jnp.where(last, 0.0, nxt[0, 0:1, lo - base:hi - base])
        zprev = jnp.where(row == 0, prev_row, pltpu.roll(z, 1, axis=0))
        znext = jnp.where(row == tm - 1, next_row, pltpu.roll(z, tm - 1, axis=0))
        mu = mu_ref[:, lo - R_OFF:hi - R_OFF]
        return z + mu[0:1] * (zprev - z) + mu[1:2] * (znext - z)

    e = e_ref[...]
    et = et_ref[...]
    a = _sigmoid(a0_ref[...] + _bdot(shifted(ALR_OFF, ALR_OFF + LANES), a2_ref[...]))
    k = shifted(K_OFF, K_OFF + RWKV_WIDTH)
    kk_raw = k * kk_ref[...]
    kk = kk_raw * lax.rsqrt(_seg_sum(kk_raw * kk_raw, e, et) + EPS)
    kk_out[0] = kk.astype(kk_out.dtype)
    b_out[0] = (kk * a).astype(b_out.dtype)
    kmod = k * (1.0 + (a - 1.0) * ka_ref[...])
    k_out[0] = kmod.astype(k_out.dtype)
    r = shifted(R_OFF, R_OFF + RWKV_WIDTH)
    r_out[0] = r.astype(r_out.dtype)
    v = shifted(V_OFF, V_OFF + RWKV_WIDTH)
    v_out[0] = v.astype(v_out.dtype)
    bonus_out[0] = (_seg_sum(r * kmod * rk_ref[...], e, et) * v).astype(bonus_out.dtype)
    g_out[0] = _bdot(_sigmoid(shifted(GLR_OFF, GLR_OFF + GATE_LORA)), g2_ref[...]).astype(g_out.dtype)
    decay_scale = -math.exp(-0.5)
    wf = w0_ref[0:1] + _dot3(jnp.tanh(shifted(WLF_OFF, WLF_OFF + LANES)), w2f_ref[...])
    lwf = decay_scale * _sigmoid(wf)
    cum_f = _mask_dot(_order_mask(tm, False, False, CHUNK), lwf)
    cumf_out[0] = cum_f
    excf_out[0] = cum_f - lwf
    wb = w0_ref[1:2] + _dot3(jnp.tanh(shifted(WLB_OFF, WLB_OFF + LANES)), w2b_ref[...])
    lwb = decay_scale * _sigmoid(wb)
    cum_b = _mask_dot(_order_mask(tm, True, False, CHUNK), lwb)
    cumb_out[0] = cum_b
    excb_out[0] = cum_b - lwb


def rwkv_prep(z, mu, w0, w2f, w2b, a0, a2, g2, k_k, k_a, r_k, seg_e, seg_et, tm):
    b, s, _ = z.shape
    nt = s // tm
    hb = tm // 8
    full = lambda arr: pl.BlockSpec(arr.shape, lambda bi, t: (0,) * arr.ndim)
    in_specs = []
    for width, off in ((RWKV_MAIN, R_OFF), (RWKV_SMALL, GLR_OFF)):
        cb = off // width
        in_specs += [pl.BlockSpec((1, tm, width), lambda bi, t, cb=cb: (bi, t, cb)),
                     pl.BlockSpec((1, 8, width), lambda bi, t, cb=cb: (bi, jnp.maximum(t * hb - 1, 0), cb)),
                     pl.BlockSpec((1, 8, width), lambda bi, t, cb=cb: (bi, jnp.minimum((t + 1) * hb, nt * hb - 1), cb))]
    params = [mu, w0, w2f, w2b, a0, a2, g2, k_k, k_a, r_k, seg_e, seg_et]
    in_specs += [full(p) for p in params]
    out_blk = pl.BlockSpec((1, tm, RWKV_WIDTH), lambda bi, t: (bi, t, 0))
    return pl.pallas_call(
        _rwkv_prep_kernel,
        grid=(b, nt),
        in_specs=in_specs,
        out_specs=[out_blk] * 11,
        out_shape=[jax.ShapeDtypeStruct((b, s, RWKV_WIDTH), dt) for dt in [BF16] * 5 + [F32] * 4 + [BF16] * 2],
        compiler_params=_cparams(("parallel", "parallel")),
        name="rwkv_prep",
    )(z, z, z, z, z, z, *params)


def _stack_heads(x):
    lane = _iota2(x.shape, 2)
    return jnp.concatenate([jnp.where(lane < RWKV_HEAD, x, 0.0), jnp.where(lane < RWKV_HEAD, 0.0, x)], axis=1)


def _rwkv_chunks(r, k, v, kk, bb, cum, exc, revs, g_ref):
    nb, c, _ = r.shape
    n2 = 2 * c
    tot = jnp.stack([cum[j, (0 if rev else c - 1)][None] for j, rev in enumerate(revs)], axis=0)
    e_neg = jnp.exp(-cum)
    e_end = jnp.exp(tot - cum)
    al2 = _stack_heads(-kk * jnp.exp(exc))
    rb2 = _stack_heads(r * jnp.exp(cum))
    bt2 = _stack_heads(bb * e_neg)
    kt2 = _stack_heads(k * e_neg)
    be2 = _stack_heads(bb * e_end)
    ke2 = _stack_heads(k * e_end)
    v2 = _stack_heads(v)
    gram = _bdot_nt(jnp.concatenate([al2, rb2], axis=1), jnp.concatenate([bt2, kt2], axis=1))
    strict = jnp.stack([_order_mask(n2, rev, True, c) for rev in revs], axis=0)
    incl = jnp.stack([_order_mask(n2, rev, False, c) for rev in revs], axis=0)
    a_ab = jnp.where(strict, gram[:, :n2, :n2], 0.0)
    a_ak = jnp.where(strict, gram[:, :n2, n2:], 0.0)
    a_rb = jnp.where(incl, gram[:, n2:, :n2], 0.0)
    a_rk = jnp.where(incl, gram[:, n2:, n2:], 0.0)
    t_inv = _neumann_inverse(a_ab)
    wu = _bdot(t_inv, jnp.concatenate([al2, _bdot(a_ak, v2)], axis=2))
    g = g_ref[...]
    proj = _bdot_nt(jnp.concatenate([wu[:, :, :LANES], rb2], axis=1), g)
    u2 = proj[:, :n2] + wu[:, :, LANES:]
    uv = jnp.concatenate([u2, v2], axis=1)
    o2 = proj[:, n2:] + _bdot(jnp.concatenate([a_rb, a_rk], axis=2), uv)
    g_ref[...] = g * jnp.exp(tot) + _bdot_tn(uv, jnp.concatenate([be2, ke2], axis=1))
    return o2[:, :c] + o2[:, c:]


def _rwkv_scan_kernel(*refs, pairs):
    fwd_refs, bwd_refs = refs[0:7], refs[7:14]
    of_ref, ob_ref, g_ref = refs[14:17]

    @pl.when(pl.program_id(2) == 0)
    def _():
        g_ref[...] = jnp.zeros_like(g_ref)

    revs = [False] * pairs + [True] * pairs
    c = CHUNK
    sub_steps = of_ref.shape[1] // c
    for sub in range(sub_steps):
        rows_of = (slice(sub * c, (sub + 1) * c), slice((sub_steps - 1 - sub) * c, (sub_steps - sub) * c))
        operands = []
        for t in range(7):
            operands.append(jnp.stack([ref[0, rows_of[d], p * LANES:(p + 1) * LANES].astype(F32)
                                       for d, ref in enumerate((fwd_refs[t], bwd_refs[t])) for p in range(pairs)],
                                      axis=0))
        o = _rwkv_chunks(*operands, revs, g_ref)
        for d, o_ref in enumerate((of_ref, ob_ref)):
            for p in range(pairs):
                o_ref[0, rows_of[d], p * LANES:(p + 1) * LANES] = o[d * pairs + p].astype(o_ref.dtype)


def rwkv_scan(r, k, v, kk, bb, cum_f, exc_f, cum_b, exc_b, pairs):
    b, s, _ = r.shape
    c = CHUNK * SCAN_CHUNKS_PER_STEP
    n = s // c
    w = pairs * LANES
    fwd = pl.BlockSpec((1, c, w), lambda bi, p, t: (bi, t, p))
    bwd = pl.BlockSpec((1, c, w), lambda bi, p, t: (bi, n - 1 - t, p))
    return pl.pallas_call(
        functools.partial(_rwkv_scan_kernel, pairs=pairs),
        grid=(b, RWKV_WIDTH // w, n),
        in_specs=[fwd] * 7 + [bwd] * 7,
        out_specs=[fwd, bwd],
        out_shape=[jax.ShapeDtypeStruct((b, s, RWKV_WIDTH), BF16)] * 2,
        scratch_shapes=[pltpu.VMEM((2 * pairs, LANES, LANES), F32)],
        compiler_params=_cparams(("parallel", "parallel", "arbitrary")),
        name="rwkv_scan",
    )(r, k, v, kk, bb, cum_f, exc_f, r, k, v, kk, bb, cum_b, exc_b)


def _rwkv_post_kernel(of_ref, ob_ref, bonus_ref, g_ref, lnw_ref, lnb_ref, e_ref, et_ref, y_ref):
    e = e_ref[...]
    et = et_ref[...]
    o = of_ref[0].astype(F32) + ob_ref[0].astype(F32)
    mean = _seg_sum(o, e, et) * (1.0 / RWKV_HEAD)
    cen = o - mean
    var = _seg_sum(cen * cen, e, et) * (1.0 / RWKV_HEAD)
    y = cen * lax.rsqrt(var + RWKV_LN_EPS) * lnw_ref[...] + lnb_ref[...]
    y_ref[0] = ((y + bonus_ref[0].astype(F32)) * g_ref[0].astype(F32)).astype(y_ref.dtype)


def rwkv_post(o_f, o_b, bonus, g, ln_w, ln_b, seg_e, seg_et, tm):
    b, s, w = o_f.shape
    blk = pl.BlockSpec((1, tm, w), lambda bi, t: (bi, t, 0))
    full = lambda arr: pl.BlockSpec(arr.shape, lambda bi, t: (0,) * arr.ndim)
    params = [ln_w.reshape(1, w), ln_b.reshape(1, w), seg_e, seg_et]
    return pl.pallas_call(
        _rwkv_post_kernel,
        grid=(b, s // tm),
        in_specs=[blk] * 4 + [full(p) for p in params],
        out_specs=blk,
        out_shape=jax.ShapeDtypeStruct((b, s, w), BF16),
        compiler_params=_cparams(("parallel", "parallel")),
        name="rwkv_post",
    )(o_f, o_b, bonus, g, *params)


def _gdn_conv_kernel(z_ref, zp_ref, zn_ref, cw_ref, sc_ref, o_ref, *, l2norm):
    t = pl.program_id(1)
    tm = z_ref.shape[1]
    halo = zp_ref.shape[1]
    ext = jnp.concatenate([jnp.where(t == 0, 0.0, zp_ref[0].astype(F32)), z_ref[0].astype(F32),
                           jnp.where(t == pl.num_programs(1) - 1, 0.0, zn_ref[0].astype(F32))], axis=0)
    n = tm + 2 * halo
    pad = cw_ref.shape[0] // 2
    y = 0.0
    for j in range(cw_ref.shape[0]):
        shifted = ext if j == pad else pltpu.roll(ext, (pad - j) % n, axis=0)
        y = y + cw_ref[j:j + 1] * shifted[halo:halo + tm]
    y = _silu(y)
    if l2norm:
        for h in range(y.shape[1] // GDN_DK):
            sl = slice(h * GDN_DK, (h + 1) * GDN_DK)
            yh = y[:, sl]
            ss = jnp.sum(yh * yh, axis=-1, keepdims=True)
            o_ref[0, :, sl] = (yh * lax.rsqrt(ss + EPS) * sc_ref[0:1, sl]).astype(o_ref.dtype)
    else:
        o_ref[0] = y.astype(o_ref.dtype)


def gdn_conv(z, conv_w, scale, col_off, width, l2norm, tm, cw):
    b, s, _ = z.shape
    nt = s // tm
    halo = 8 * (4 // z.dtype.itemsize)
    hb = tm // halo
    cb = col_off // cw
    in_specs = [pl.BlockSpec((1, tm, cw), lambda bi, t, c: (bi, t, cb + c)),
                pl.BlockSpec((1, halo, cw), lambda bi, t, c: (bi, jnp.maximum(t * hb - 1, 0), cb + c)),
                pl.BlockSpec((1, halo, cw), lambda bi, t, c: (bi, jnp.minimum((t + 1) * hb, nt * hb - 1), cb + c)),
                pl.BlockSpec((conv_w.shape[0], cw), lambda bi, t, c: (0, cb + c)),
                pl.BlockSpec((1, cw), lambda bi, t, c: (0, c))]
    return pl.pallas_call(
        functools.partial(_gdn_conv_kernel, l2norm=l2norm),
        grid=(b, nt, width // cw),
        in_specs=in_specs,
        out_specs=pl.BlockSpec((1, tm, cw), lambda bi, t, c: (bi, t, c)),
        out_shape=jax.ShapeDtypeStruct((b, s, width), BF16),
        compiler_params=_cparams(("parallel", "parallel", "parallel")),
        name="gdn_conv_norm" if l2norm else "gdn_conv",
    )(z, z, z, conv_w, scale)


def _gdn_gates_kernel(zs_ref, na_ref, dtb_ref, o_ref):
    zs = zs_ref[0]
    tm = zs.shape[0]
    gg = na_ref[...] * _softplus(zs + dtb_ref[...])
    gam_f = _mask_dot(_order_mask(tm, False, False, CHUNK), gg)
    gam_b = _mask_dot(_order_mask(tm, True, False, CHUNK), gg)
    lane = _iota2(zs.shape, 1)
    o_ref[0] = jnp.where(lane < 2 * GDN_VHEADS, _sigmoid(zs), jnp.where(lane < 3 * GDN_VHEADS, gam_f, gam_b))


def gdn_gates(zs, neg_a, dtb, tm):
    b, s, w = zs.shape
    blk = pl.BlockSpec((1, tm, w), lambda bi, t: (bi, t, 0))
    row = pl.BlockSpec((1, w), lambda bi, t: (0, 0))
    return pl.pallas_call(
        _gdn_gates_kernel,
        grid=(b, s // tm),
        in_specs=[blk, row, row],
        out_specs=blk,
        out_shape=jax.ShapeDtypeStruct((b, s, w), F32),
        compiler_params=_cparams(("parallel", "parallel")),
        name="gdn_gates",
    )(zs, neg_a, dtb)


def _gdn_chunks(q, k, v2, beta2, gam2, revs, s_ref):
    nb, c, _ = q.shape
    n2 = 2 * c
    per_batch = lambda fn: jnp.stack([fn(r) for r in revs], axis=0)
    incl = per_batch(lambda r: _order_mask(n2, r, False, c))
    strict = per_batch(lambda r: _order_mask(n2, r, True, c))
    gam_c = jnp.broadcast_to(gam2, (nb, n2, n2))
    gam_r = jnp.swapaxes(gam_c, 1, 2)
    tot2 = jnp.stack([jnp.concatenate(
        [jnp.broadcast_to(gam2[j, h * c + (0 if r else c - 1)][None], (c, 1)) for h in range(2)], axis=0)
        for j, r in enumerate(revs)], axis=0)
    diff = gam_c - gam_r
    dec_s = jnp.where(strict, jnp.exp(jnp.where(strict, diff, 0.0)), 0.0)
    dec_i = jnp.where(incl, jnp.exp(jnp.where(incl, diff, 0.0)), 0.0)
    k2 = jnp.concatenate([k, k], axis=1)
    q2 = jnp.concatenate([q, q], axis=1)
    gram = _bdot_nt(jnp.concatenate([k2, q2], axis=1), k2)
    a_mat = gram[:, :n2] * beta2 * dec_s
    qk = gram[:, n2:] * dec_i
    t_inv = _neumann_inverse(-a_mat)
    e_gam = jnp.exp(gam2)
    uw = _bdot(t_inv, jnp.concatenate([v2 * beta2, k2 * (beta2 * e_gam)], axis=2))
    u2 = uw[:, :, :GDN_DV]
    w2 = uw[:, :, GDN_DV:]
    qd2 = q2 * e_gam
    ke2 = k2 * jnp.exp(tot2 - gam2)
    dl2 = jnp.exp(tot2)
    ws, qs = [], []
    for e in range(2):
        rows = slice(e * c, (e + 1) * c)
        both = _bdot(jnp.concatenate([w2[:, rows], qd2[:, rows]], axis=1), s_ref[e])
        ws.append(both[:, :c])
        qs.append(both[:, c:])
    vnew2 = u2 - jnp.concatenate(ws, axis=1)
    o2 = jnp.concatenate(qs, axis=1) + _bdot(qk, vnew2)
    for e in range(2):
        rows = slice(e * c, (e + 1) * c)
        s_ref[e] = s_ref[e] * dl2[:, e * c:e * c + 1, :] + _bdot_tn(ke2[:, rows], vnew2[:, rows])
    return o2


def _gdn_scan_kernel(qf, kf, vf, gf, qb, kb, vb, gb, of_ref, ob_ref, s_ref, *, heads):
    @pl.when(pl.program_id(2) == 0)
    def _():
        s_ref[...] = jnp.zeros_like(s_ref)

    c = CHUNK
    sub_steps = qf.shape[1] // c
    lane = _iota2((c, LANES), 1)

    def column(gates, idx):
        return jnp.sum(jnp.where(lane == idx, gates, 0.0), axis=1, keepdims=True)

    for sub in range(sub_steps):
        rows_of = (slice(sub * c, (sub + 1) * c), slice((sub_steps - 1 - sub) * c, (sub_steps - sub) * c))
        qs, ks, vs, betas, gams, revs = [], [], [], [], [], []
        for d, (q_ref, k_ref, v_ref, g_ref) in enumerate(((qf, kf, vf, gf), (qb, kb, vb, gb))):
            rows = rows_of[d]
            gates = g_ref[0, rows, :]
            for i in range(heads):
                vh = 2 * (pl.program_id(1) * heads + i)
                ksl = slice(i * GDN_DK, (i + 1) * GDN_DK)
                qs.append(q_ref[0, rows, ksl].astype(F32))
                ks.append(k_ref[0, rows, ksl].astype(F32))
                vs.append(jnp.concatenate(
                    [v_ref[0, rows, (2 * i + e) * GDN_DV:(2 * i + e + 1) * GDN_DV].astype(F32) for e in range(2)], axis=0))
                betas.append(jnp.concatenate([column(gates, d * GDN_VHEADS + vh + e) for e in range(2)], axis=0))
                gams.append(jnp.concatenate([column(gates, (2 + d) * GDN_VHEADS + vh + e) for e in range(2)], axis=0))
                revs.append(d == 1)
        o2 = _gdn_chunks(jnp.stack(qs), jnp.stack(ks), jnp.stack(vs), jnp.stack(betas), jnp.stack(gams), revs, s_ref)
        for d, o_ref in enumerate((of_ref, ob_ref)):
            for i in range(heads):
                j = d * heads + i
                for e in range(2):
                    o_ref[0, rows_of[d], (2 * i + e) * GDN_DV:(2 * i + e + 1) * GDN_DV] = (
                        o2[j, e * c:(e + 1) * c].astype(o_ref.dtype))


def gdn_scan(qk, v, gates, heads):
    b, s, _ = v.shape
    c = CHUNK * SCAN_CHUNKS_PER_STEP
    n = s // c
    ng = GDN_KHEADS // heads

    def spec(w, off, rev):
        if rev:
            return pl.BlockSpec((1, c, w), lambda bi, h, t: (bi, n - 1 - t, off + h))
        return pl.BlockSpec((1, c, w), lambda bi, h, t: (bi, t, off + h))

    def gspec(rev):
        if rev:
            return pl.BlockSpec((1, c, LANES), lambda bi, h, t: (bi, n - 1 - t, 0))
        return pl.BlockSpec((1, c, LANES), lambda bi, h, t: (bi, t, 0))

    kw = heads * GDN_DK
    vw = heads * 2 * GDN_DV
    in_specs = [spec(kw, 0, False), spec(kw, ng, False), spec(vw, 0, False), gspec(False),
                spec(kw, 0, True), spec(kw, ng, True), spec(vw, 0, True), gspec(True)]
    return pl.pallas_call(
        functools.partial(_gdn_scan_kernel, heads=heads),
        grid=(b, ng, n),
        in_specs=in_specs,
        out_specs=[spec(vw, 0, False), spec(vw, 0, True)],
        out_shape=[jax.ShapeDtypeStruct((b, s, GDN_VW), BF16)] * 2,
        scratch_shapes=[pltpu.VMEM((2, 2 * heads, GDN_DK, GDN_DV), F32)],
        compiler_params=_cparams(("parallel", "parallel", "arbitrary")),
        name="gdn_scan",
    )(qk, qk, v, gates, qk, qk, v, gates)


def _gdn_post_kernel(of_ref, ob_ref, gate_ref, nw_ref, y_ref):
    nw = nw_ref[...]
    for h in range(of_ref.shape[2] // GDN_DV):
        sl = slice(h * GDN_DV, (h + 1) * GDN_DV)
        o = of_ref[0, :, sl].astype(F32) + ob_ref[0, :, sl].astype(F32)
        ms = jnp.mean(o * o, axis=-1, keepdims=True)
        y = o * lax.rsqrt(ms + EPS) * nw
        y_ref[0, :, sl] = (y * _silu(gate_ref[0, :, sl].astype(F32))).astype(y_ref.dtype)


def gdn_post(o_f, o_b, z, gate_off, norm_w, tm, cw):
    b, s, w = o_f.shape
    blk = pl.BlockSpec((1, tm, cw), lambda bi, t, c: (bi, t, c))
    gb = gate_off // cw
    return pl.pallas_call(
        _gdn_post_kernel,
        grid=(b, s // tm, w // cw),
        in_specs=[blk, blk, pl.BlockSpec((1, tm, cw), lambda bi, t, c: (bi, t, gb + c)),
                  pl.BlockSpec((1, GDN_DV), lambda bi, t, c: (0, 0))],
        out_specs=blk,
        out_shape=jax.ShapeDtypeStruct((b, s, w), BF16),
        compiler_params=_cparams(("parallel", "parallel", "parallel")),
        name="gdn_post",
    )(o_f, o_b, z, norm_w.reshape(1, GDN_DV))


def _moe_router_kernel(x_ref, g_ref, w_ref, b_ref, route_ref, counts_ref):
    @pl.when(pl.program_id(0) == 0)
    def _():
        counts_ref[...] = jnp.zeros_like(counts_ref)

    x = x_ref[...]
    tm = x.shape[0]
    ms = jnp.mean(x * x, axis=-1, keepdims=True)
    h = x * lax.rsqrt(ms + EPS) * g_ref[...]
    lane_i = _iota2((tm, LANES), 1)
    lane = lane_i.astype(F32)
    lane_grp = lax.shift_right_logical(lane_i, 3).astype(F32)
    neg = -jnp.inf
    logits = _dot3(h, w_ref[...]) + b_ref[...]
    gl = jnp.where((lane_i >= N_EXPERTS) & (lane_i < N_EXPERTS + N_GROUPS), logits, neg)
    gmax = jnp.max(gl, axis=-1, keepdims=True)
    gidx = jnp.min(jnp.where(gl == gmax, lane, float(LANES)), axis=-1, keepdims=True) - float(N_EXPERTS)
    grp_w = 1.0 / jnp.sum(jnp.exp(gl - gmax), axis=-1, keepdims=True)
    sel = jnp.where((lane_i < N_EXPERTS) & (lane_grp == gidx), logits, neg)
    m1 = jnp.max(sel, axis=-1, keepdims=True)
    i1 = jnp.min(jnp.where(sel == m1, lane, float(LANES)), axis=-1, keepdims=True)
    sel2 = jnp.where(lane == i1, neg, sel)
    m2 = jnp.max(sel2, axis=-1, keepdims=True)
    i2 = jnp.min(jnp.where(sel2 == m2, lane, float(LANES)), axis=-1, keepdims=True)
    e2 = jnp.exp(m2 - m1)
    w1 = grp_w / (1.0 + e2)
    w2 = grp_w * e2 / (1.0 + e2)
    hits = jnp.where((lane == i1) | (lane == i2), 1.0, 0.0)
    before = counts_ref[...] + _bdot(_order_mask(tm, False, True, tm), hits)
    r1 = jnp.sum(jnp.where(lane == i1, before, 0.0), axis=-1, keepdims=True)
    r2 = jnp.sum(jnp.where(lane == i2, before, 0.0), axis=-1, keepdims=True)
    counts_ref[...] += jnp.sum(hits, axis=0, keepdims=True)
    route_ref[...] = jnp.where(lane_i == 0, i1, jnp.where(lane_i == 1, i2, jnp.where(
        lane_i == 2, w1, jnp.where(lane_i == 3, w2, jnp.where(lane_i == 4, r1, jnp.where(lane_i == 5, r2, 0.0))))))


def moe_router(x, g, w_group, b_group, w_router, b_router, tm):
    m, d = x.shape
    tm = min(tm, m)
    pad = lambda w: jnp.pad(w, ((0, 0), (0, LANES - w.shape[1])))
    full = lambda arr: pl.BlockSpec(arr.shape, lambda i: (0, 0))
    params = [g.reshape(1, d), pad(jnp.concatenate([w_router, w_group], axis=1)),
              pad(jnp.concatenate([b_router, b_group]).reshape(1, -1))]
    return pl.pallas_call(
        _moe_router_kernel,
        grid=(m // tm,),
        in_specs=[pl.BlockSpec((tm, d), lambda i: (i, 0))] + [full(p) for p in params],
        out_specs=[pl.BlockSpec((tm, LANES), lambda i: (i, 0)), pl.BlockSpec((1, LANES), lambda i: (0, 0))],
        out_shape=[jax.ShapeDtypeStruct((m, LANES), F32), jax.ShapeDtypeStruct((1, LANES), F32)],
        compiler_params=_cparams(("arbitrary",)),
        name="moe_router",
    )(x, *params)


def _route_tables(route, counts, tm, n_tiles):
    counts = counts[0, :N_EXPERTS].astype(jnp.int32)
    padded = ((counts + tm - 1) // tm) * tm
    ends = jnp.cumsum(padded)
    starts = ends - padded
    experts = route[:, 0:2].astype(jnp.int32)
    pos = (starts[experts] + route[:, 4:6].astype(jnp.int32)).reshape(-1)
    tile_start = jnp.arange(n_tiles, dtype=jnp.int32) * tm
    tile_e = jnp.minimum(jnp.sum(tile_start[:, None] >= ends[None, :], axis=1), N_EXPERTS - 1).astype(jnp.int32)
    n_valid = (ends[-1:] // tm).astype(jnp.int32)
    fill = jnp.stack([jnp.maximum(ends - tm, 0), (padded > 0).astype(jnp.int32)]).astype(jnp.int32)
    return pos, tile_e, n_valid, fill


def _moe_dispatch_kernel(pos_ref, nvalid_ref, fill_ref, x_ref, g_ref, hs_hbm, hbuf, zbuf, sem_rows, sem_fill, *,
                         n_tiles):
    i = pl.program_id(0)
    tm = x_ref.shape[0]
    row_tile = zbuf.shape[0]

    @pl.when(i == 0)
    def _():
        zbuf[...] = jnp.zeros_like(zbuf)
        fills = [(fill_ref[1, e] > 0, fill_ref[0, e]) for e in range(N_EXPERTS)]
        fills += [(nvalid_ref[0] + k < n_tiles, (nvalid_ref[0] + k) * row_tile) for k in range(N_EXPERTS)]
        fills = [(cond, pl.multiple_of(start, row_tile)) for cond, start in fills]
        for cond, start in fills:
            @pl.when(cond)
            def _(start=start):
                pltpu.make_async_copy(zbuf, hs_hbm.at[pl.ds(start, row_tile)], sem_fill).start()
        for cond, start in fills:
            @pl.when(cond)
            def _(start=start):
                pltpu.make_async_copy(zbuf, hs_hbm.at[pl.ds(start, row_tile)], sem_fill).wait()

    x = x_ref[...]
    ms = jnp.mean(x * x, axis=-1, keepdims=True)
    hbuf[...] = x * lax.rsqrt(ms + EPS) * g_ref[...]

    def issue(r, carry):
        base = 2 * (i * tm + r)
        for slot in range(2):
            pltpu.make_async_copy(hbuf.at[pl.ds(r, 1)], hs_hbm.at[pl.ds(pos_ref[base + slot], 1)], sem_rows).start()
        return carry

    lax.fori_loop(0, tm, issue, 0, unroll=8)
    for slot in range(2):
        pltpu.make_async_copy(hbuf, hs_hbm.at[pl.ds(0, tm)], sem_rows).wait()


def moe_dispatch(x, g, pos, n_valid, fill, n_tiles, tm):
    m, d = x.shape
    tm_x = min(256, m)
    return pl.pallas_call(
        functools.partial(_moe_dispatch_kernel, n_tiles=n_tiles),
        grid_spec=pltpu.PrefetchScalarGridSpec(
            num_scalar_prefetch=3,
            grid=(m // tm_x,),
            in_specs=[pl.BlockSpec((tm_x, d), lambda i, p, nv, fl: (i, 0)),
                      pl.BlockSpec((1, d), lambda i, p, nv, fl: (0, 0))],
            out_specs=pl.BlockSpec(memory_space=pl.ANY),
            scratch_shapes=[pltpu.VMEM((tm_x, d), F32), pltpu.VMEM((tm, d), F32),
                            pltpu.SemaphoreType.DMA(()), pltpu.SemaphoreType.DMA(())]),
        out_shape=jax.ShapeDtypeStruct((n_tiles * tm, d), F32),
        compiler_params=_cparams(("arbitrary",)),
        name="moe_dispatch",
    )(pos, n_valid, fill, x, g.reshape(1, d))


def _moe_ffn_kernel(tile_e_ref, nvalid_ref, h_ref, wg_ref, wu_ref, wd_ref, y_ref, wg_b, wu_b, wd_b):
    i = pl.program_id(0)

    @pl.when((i == 0) | (tile_e_ref[i] != tile_e_ref[jnp.maximum(i - 1, 0)]))
    def _():
        wg_b[...] = wg_ref[0, 0].astype(BF16)
        wu_b[...] = wu_ref[0, 0].astype(BF16)
        wd_b[...] = wd_ref[0, 0].astype(BF16)

    @pl.when(i < nvalid_ref[0])
    def _():
        x = h_ref[...].astype(BF16)
        hid = _silu(_dot(x, wg_b[...])) * _dot(x, wu_b[...])
        y_ref[...] = _dot(hid.astype(BF16), wd_b[...])

    @pl.when(i >= nvalid_ref[0])
    def _():
        y_ref[...] = jnp.zeros_like(y_ref)


def moe_ffn(h_sorted, tile_e, n_valid, w_gate, w_up, w_down, layer, tm):
    _, d = h_sorted.shape
    n_tiles = tile_e.shape[0]
    ff = w_gate.shape[-1]
    wspec = lambda shape: pl.BlockSpec((1, 1) + shape, lambda i, te, nv: (layer, te[i], 0, 0))
    return pl.pallas_call(
        _moe_ffn_kernel,
        grid_spec=pltpu.PrefetchScalarGridSpec(
            num_scalar_prefetch=2,
            grid=(n_tiles,),
            in_specs=[pl.BlockSpec((tm, d), lambda i, te, nv: (i, 0)), wspec((d, ff)), wspec((d, ff)), wspec((ff, d))],
            out_specs=pl.BlockSpec((tm, d), lambda i, te, nv: (i, 0)),
            scratch_shapes=[pltpu.VMEM((d, ff), BF16), pltpu.VMEM((d, ff), BF16), pltpu.VMEM((ff, d), BF16)]),
        out_shape=jax.ShapeDtypeStruct((n_tiles * tm, d), F32),
        compiler_params=_cparams(("arbitrary",)),
        name="moe_ffn",
    )(tile_e, n_valid, h_sorted, w_gate, w_up, w_down)


def _moe_combine_kernel(pos_ref, x_ref, route_ref, y_hbm, g_ref, o_ref, ybuf, sem, *, final_norm):
    i = pl.program_id(0)
    tm = x_ref.shape[0]

    def issue(r, carry):
        base = 2 * (i * tm + r)
        for slot in range(2):
            pltpu.make_async_copy(y_hbm.at[pl.ds(pos_ref[base + slot], 1)], ybuf.at[slot, pl.ds(r, 1)], sem).start()
        return carry

    lax.fori_loop(0, tm, issue, 0, unroll=8)
    for slot in range(2):
        pltpu.make_async_copy(y_hbm.at[pl.ds(0, tm)], ybuf.at[slot], sem).wait()
    route = route_ref[...]
    o = x_ref[...] + route[:, 2:3] * ybuf[0] + route[:, 3:4] * ybuf[1]
    if final_norm:
        ms = jnp.mean(o * o, axis=-1, keepdims=True)
        o = o * lax.rsqrt(ms + EPS) * g_ref[...]
    o_ref[...] = o


def moe_combine(x, route, y_sorted, pos, norm_g, tm):
    m, d = x.shape
    tm = min(tm, m)
    final_norm = norm_g is not None
    g = (norm_g if final_norm else jnp.ones((d,), F32)).reshape(1, d)
    return pl.pallas_call(
        functools.partial(_moe_combine_kernel, final_norm=final_norm),
        grid_spec=pltpu.PrefetchScalarGridSpec(
            num_scalar_prefetch=1,
            grid=(m // tm,),
            in_specs=[pl.BlockSpec((tm, d), lambda i, p: (i, 0)), pl.BlockSpec((tm, LANES), lambda i, p: (i, 0)),
                      pl.BlockSpec(memory_space=pl.ANY), pl.BlockSpec((1, d), lambda i, p: (0, 0))],
            out_specs=pl.BlockSpec((tm, d), lambda i, p: (i, 0)),
            scratch_shapes=[pltpu.VMEM((2, tm, d), F32), pltpu.SemaphoreType.DMA(())]),
        out_shape=jax.ShapeDtypeStruct((m, d), F32),
        compiler_params=_cparams(("arbitrary",)),
        name="moe_combine",
    )(pos, x, route, y_sorted, g)


def _pad_rows(w, rows):
    return jnp.pad(w, ((0, rows - w.shape[0]), (0, 0)))


def _even_layer(x, norm_g, w_in, gla_w_alpha, gla_b_alpha, gla_norm, mu, w0, w2, a0, a2, g2,
                k_k, k_a, r_k, ln_w, ln_b, w_out):
    b, s, d = x.shape
    gla_cols, rw = w_in[:, :3104], w_in[:, 3104:]
    mu_g = lambda lo, hi, width: jnp.pad(mu[:, lo:hi], ((0, 0), (0, width - (hi - lo))))
    pad_c = lambda w, width: jnp.pad(w, ((0, 0), (0, width - w.shape[1])))
    w_cat = jnp.concatenate([
        gla_cols[:, 0:3072],
        rw[:, 0:3072], rw[:, 3360:3616],
        pad_c(rw[:, 3072:3168], LANES), pad_c(rw[:, 3168:3264], LANES), pad_c(rw[:, 3264:3360], LANES),
        pad_c(gla_cols[:, 3072:3104], LANES)], axis=1).astype(BF16)
    mu_cat = jnp.concatenate([
        mu[:, 0:3072], mu[:, 3360:3616], mu_g(3072, 3168, LANES), mu_g(3168, 3264, LANES),
        mu_g(3264, 3360, LANES)], axis=1)
    z = norm_matmul(x.reshape(b * s, d), norm_g, w_cat, 512, 768).reshape(b, s, EVEN_PAD)

    wa = jnp.stack([_pad_rows(gla_w_alpha[0], LANES),
                    jnp.pad(gla_w_alpha[1], ((GLA_LOWRANK, LANES - 2 * GLA_LOWRANK), (0, 0)))])
    gla_f, gla_b = gla_scan(z, wa, gla_b_alpha)
    y_gla = gla_post(gla_f, gla_b, z, gla_norm, 256)

    head_of_lane = jnp.arange(RWKV_WIDTH) // RWKV_HEAD
    seg_e = (head_of_lane[:, None] == jnp.arange(LANES)[None, :]).astype(F32)
    seg_et = seg_e.T
    row = lambda p: p.reshape(1, RWKV_WIDTH)
    r, k, v, kk, bb, cum_f, exc_f, cum_b, exc_b, g, bonus = rwkv_prep(
        z, mu_cat, w0, _pad_rows(w2[0], LANES), _pad_rows(w2[1], LANES), row(a0), _pad_rows(a2, LANES), g2,
        row(k_k), row(k_a), row(r_k), seg_e, seg_et, 128)
    rw_f, rw_b = rwkv_scan(r, k, v, kk, bb, cum_f, exc_f, cum_b, exc_b, RWKV_PAIRS_PER_STEP)
    y_rwkv = rwkv_post(rw_f, rw_b, bonus, g, ln_w, ln_b, seg_e, seg_et, 256)

    w_out = w_out.astype(BF16)
    out = proj_residual([y_gla.reshape(b * s, GLA_WIDTH), y_rwkv.reshape(b * s, RWKV_WIDTH)],
                        [w_out[:GLA_WIDTH], w_out[GLA_WIDTH:]], x.reshape(b * s, d), 512, 1024)
    return out.reshape(b, s, d)


def _odd_layer(x, norm_g, w_in, conv_w, a_log, dt_bias, norm_w, w_out):
    b, s, d = x.shape
    x2 = x.reshape(b * s, d)
    main = GDN_QKV + GDN_VW
    z = norm_matmul(x2, norm_g, w_in[:, :main].astype(BF16), 512, 1024, BF16).reshape(b, s, main)
    zs = norm_matmul(x2, norm_g, w_in[:, main:].astype(BF16), 512, LANES).reshape(b, s, LANES)
    scale = jnp.concatenate([jnp.full((1, GDN_KW), GDN_DK ** -0.5, F32), jnp.ones((1, GDN_KW), F32)], axis=1)
    qk = gdn_conv(z, conv_w, scale, 0, 2 * GDN_KW, True, 256, 1024)
    v = gdn_conv(z, conv_w, scale, 2 * GDN_KW, GDN_VW, False, 256, 1024)
    zero = jnp.zeros((2 * GDN_VHEADS,), F32)
    neg_a = jnp.concatenate([zero, -jnp.exp(a_log.reshape(-1))]).reshape(1, LANES)
    dtb = jnp.concatenate([zero, dt_bias.reshape(-1)]).reshape(1, LANES)
    gates = gdn_gates(zs, neg_a, dtb, 256)
    o_f, o_b = gdn_scan(qk, v, gates, GDN_HEADS_PER_STEP)
    y = gdn_post(o_f, o_b, z, GDN_QKV, norm_w, 256, 1024)
    out = proj_residual([y.reshape(b * s, GDN_VW)], [w_out.astype(BF16)], x2, 512, 1024)
    return out.reshape(b, s, d)


def _moe_layer(x, norm_g, w_group, b_group, w_router, b_router, w_gate, w_up, w_down, layer, final_g):
    b, s, d = x.shape
    x2 = x.reshape(b * s, d)
    route, counts = moe_router(x2, norm_g, w_group, b_group, w_router, b_router, 256)
    n_tiles = -(-2 * b * s // MOE_ROW_TILE) + N_EXPERTS
    pos, tile_e, n_valid, fill = _route_tables(route, counts, MOE_ROW_TILE, n_tiles)
    h_sorted = moe_dispatch(x2, norm_g, pos, n_valid, fill, n_tiles, MOE_ROW_TILE)
    y_sorted = moe_ffn(h_sorted, tile_e, n_valid, w_gate, w_up, w_down, layer, MOE_ROW_TILE)
    out = moe_combine(x2, route, y_sorted, pos, final_g, 256)
    return out.reshape(b, s, d)


def kernel(x, norm_mix, norm_ffn, norm_final, ev_w_in, ev_gla_w_alpha, ev_gla_b_alpha, ev_gla_norm, ev_rwkv_mu, ev_rwkv_w0, ev_rwkv_w2, ev_rwkv_a0, ev_rwkv_a2, ev_rwkv_g2, ev_rwkv_k_k, ev_rwkv_k_a, ev_rwkv_r_k, ev_rwkv_ln_w, ev_rwkv_ln_b, ev_w_out, od_w_in, od_conv, od_a_log, od_dt_bias, od_norm, od_w_out, moe_w_group, moe_b_group, moe_w_router, moe_b_router, moe_w_gate, moe_w_up, moe_w_down):
    depth = norm_mix.shape[0]
    for i in range(depth):
        j = i // 2
        if i % 2 == 0:
            x = _even_layer(x, norm_mix[i], ev_w_in[j], ev_gla_w_alpha[j], ev_gla_b_alpha[j], ev_gla_norm[j],
                            ev_rwkv_mu[j], ev_rwkv_w0[j], ev_rwkv_w2[j], ev_rwkv_a0[j], ev_rwkv_a2[j],
                            ev_rwkv_g2[j], ev_rwkv_k_k[j], ev_rwkv_k_a[j], ev_rwkv_r_k[j],
                            ev_rwkv_ln_w[j], ev_rwkv_ln_b[j], ev_w_out[j])
        else:
            x = _odd_layer(x, norm_mix[i], od_w_in[j], od_conv[j], od_a_log[j], od_dt_bias[j],
                           od_norm[j], od_w_out[j])
        x = _moe_layer(x, norm_ffn[i], moe_w_group[i], moe_b_group[i], moe_w_router[i], moe_b_router[i],
                       moe_w_gate, moe_w_up, moe_w_down, i, norm_final if i == depth - 1 else None)
    return x
```

```python
import functools
import math

import jax
import jax.numpy as jnp
from jax import lax
from jax.experimental import pallas as pl
from jax.experimental.pallas import tpu as pltpu

F32 = jnp.float32
BF16 = jnp.bfloat16
HI = lax.Precision.HIGHEST

EPS = 1e-6
CHUNK = 64
LANES = 128
VMEM_LIMIT = 56 * 1024 * 1024

D_MODEL = 2048
GLA_HEADS = 4
GLA_DK = 128
GLA_DV = 256
GLA_KW = 512
GLA_WIDTH = 1024
GLA_LOWRANK = 16
GLA_TAU = 16.0
RWKV_HEAD = 64
RWKV_WIDTH = 1024
RWKV_HEADS = 16
DECAY_LORA = 96
ICLR_LORA = 96
GATE_LORA = 256
RWKV_LN_EPS = 64e-5
GDN_DK = 128
GDN_DV = 128
GDN_KHEADS = 16
GDN_VHEADS = 32
GDN_KW = 2048
GDN_VW = 4096
GDN_QKV = 8192
SCAN_CHUNKS_PER_STEP = 4
RWKV_PAIRS_PER_STEP = 4
GDN_HEADS_PER_STEP = 4
MOE_ROW_TILE = 256
N_GROUPS = 4
EXPERTS_PER_GROUP = 8
N_EXPERTS = 32
EXPERT_FF = 512

GQ_OFF, GK_OFF, GV_OFF, GG_OFF = 0, 512, 1024, 2048
R_OFF, K_OFF, V_OFF = 3072, 4096, 5120
GLR_OFF, WLF_OFF, WLB_OFF, ALR_OFF, AL_OFF = 6144, 6400, 6528, 6656, 6784
RWKV_MAIN = 3072
RWKV_SMALL = 768
EVEN_PAD = 6912


def _cparams(sem):
    return pltpu.CompilerParams(dimension_semantics=sem, vmem_limit_bytes=VMEM_LIMIT)


def _mm(a, b, ca, cb, precision):
    if a.ndim == 3:
        dims = (((ca + 1,), (cb + 1,)), ((0,), (0,)))
    else:
        dims = (((ca,), (cb,)), ((), ()))
    return lax.dot_general(a, b, dims, preferred_element_type=F32, precision=precision)


def _dot(a, b, precision=None):
    return _mm(a, b, 1, 0, precision)


def _dot_nt(a, b, precision=None):
    return _mm(a, b, 1, 1, precision)


def _dot_tn(a, b, precision=None):
    return _mm(a, b, 0, 0, precision)


def _bf16_pieces(x, n):
    pieces = []
    for _ in range(n - 1):
        p = x.astype(BF16)
        pieces.append(p)
        x = x - p.astype(F32)
    pieces.append(x.astype(BF16))
    return pieces


def _mask_dot(mask, x, pieces=3):
    mb = mask.astype(BF16)
    return sum(_dot(mb, p) for p in _bf16_pieces(x, pieces))


def _dot_mask(x, mask, pieces=2):
    mb = mask.astype(BF16)
    return sum(_dot(p, mb) for p in _bf16_pieces(x, pieces))


def _dot3(a, b):
    ah, al = _bf16_pieces(a, 2)
    bh, bl = _bf16_pieces(b, 2)
    return _dot(ah, bh) + _dot(ah, bl) + _dot(al, bh)


def _bdot(a, b):
    return _dot(a.astype(BF16), b.astype(BF16))


def _bdot_nt(a, b):
    return _dot_nt(a.astype(BF16), b.astype(BF16))


def _bdot_tn(a, b):
    return _dot_tn(a.astype(BF16), b.astype(BF16))


def _sigmoid(x):
    return 1.0 / (1.0 + jnp.exp(-x))


def _silu(x):
    return x * _sigmoid(x)


def _softplus(x):
    return jnp.maximum(x, 0.0) + jnp.log(1.0 + jnp.exp(-jnp.abs(x)))


def _iota2(shape, dim):
    return lax.broadcasted_iota(jnp.int32, shape, dim)


def _order_mask(n, rev, strict, block):
    i = _iota2((n, n), 0)
    j = _iota2((n, n), 1)
    if rev:
        m = (j > i) if strict else (j >= i)
    else:
        m = (j < i) if strict else (j <= i)
    if block < n:
        sh = block.bit_length() - 1
        m = m & (lax.shift_right_logical(i, sh) == lax.shift_right_logical(j, sh))
    return m


def _neumann_inverse(nmat):
    n = nmat.shape[-1]
    eye = (_iota2((n, n), 0) == _iota2((n, n), 1)).astype(F32)
    t = eye + nmat
    nb = nmat.astype(BF16)
    p = _dot(nb, nb)
    for step in range(5):
        pb = p.astype(BF16)
        if step < 4:
            both = _dot(pb, jnp.concatenate([pb, t.astype(BF16)], axis=-1))
            p = both[..., :n]
            t = t + both[..., n:]
        else:
            t = t + _dot(pb, t.astype(BF16))
    return t


def _norm_matmul_kernel(x_ref, g_ref, w_ref, o_ref, h_ref):
    @pl.when(pl.program_id(1) == 0)
    def _():
        x = x_ref[...]
        ms = jnp.mean(x * x, axis=-1, keepdims=True)
        h_ref[...] = (x * lax.rsqrt(ms + EPS) * g_ref[...]).astype(BF16)

    o_ref[...] = _dot(h_ref[...], w_ref[...]).astype(o_ref.dtype)


def norm_matmul(x, g, w, tm, tn, out_dtype=F32):
    m, d = x.shape
    tm = min(tm, m)
    n = w.shape[1]
    return pl.pallas_call(
        _norm_matmul_kernel,
        grid=(m // tm, n // tn),
        in_specs=[pl.BlockSpec((tm, d), lambda i, j: (i, 0)),
                  pl.BlockSpec((1, d), lambda i, j: (0, 0)),
                  pl.BlockSpec((d, tn), lambda i, j: (0, j))],
        out_specs=pl.BlockSpec((tm, tn), lambda i, j: (i, j)),
        out_shape=jax.ShapeDtypeStruct((m, n), out_dtype),
        scratch_shapes=[pltpu.VMEM((tm, d), BF16)],
        compiler_params=_cparams(("parallel", "arbitrary")),
        name="norm_matmul",
    )(x, g.reshape(1, d), w)


def _proj_residual_kernel(*refs, n_lhs):
    x_ref = refs[2 * n_lhs]
    o_ref = refs[2 * n_lhs + 1]
    acc = x_ref[...]
    for t in range(n_lhs):
        acc = acc + _dot(refs[t][...], refs[n_lhs + t][...])
    o_ref[...] = acc


def proj_residual(ys, ws, x, tm, tn):
    m, n = x.shape
    tm = min(tm, m)
    n_lhs = len(ys)
    in_specs = [pl.BlockSpec((tm, y.shape[1]), lambda i, j: (i, 0)) for y in ys]
    in_specs += [pl.BlockSpec((w.shape[0], tn), lambda i, j: (0, j)) for w in ws]
    in_specs += [pl.BlockSpec((tm, tn), lambda i, j: (i, j))]
    return pl.pallas_call(
        functools.partial(_proj_residual_kernel, n_lhs=n_lhs),
        grid=(m // tm, n // tn),
        in_specs=in_specs,
        out_specs=pl.BlockSpec((tm, tn), lambda i, j: (i, j)),
        out_shape=jax.ShapeDtypeStruct((m, n), F32),
        compiler_params=_cparams(("parallel", "arbitrary")),
        name="proj_residual",
    )(*ys, *ws, x)


def _final_norm_kernel(x_ref, g_ref, o_ref):
    x = x_ref[...]
    ms = jnp.mean(x * x, axis=-1, keepdims=True)
    o_ref[...] = x * lax.rsqrt(ms + EPS) * g_ref[...]


def final_norm(x, g, tm):
    m, d = x.shape
    tm = min(tm, m)
    return pl.pallas_call(
        _final_norm_kernel,
        grid=(m // tm,),
        in_specs=[pl.BlockSpec((tm, d), lambda i: (i, 0)), pl.BlockSpec((1, d), lambda i: (0, 0))],
        out_specs=pl.BlockSpec((tm, d), lambda i: (i, 0)),
        out_shape=jax.ShapeDtypeStruct((m, d), F32),
        compiler_params=_cparams(("parallel",)),
        name="final_norm",
    )(x, g.reshape(1, d))


def _gla_chunks(q, k, v, cum, revs, s_ref):
    c = q.shape[1]
    tot = jnp.stack([cum[j, (0 if rev else c - 1)][None] for j, rev in enumerate(revs)], axis=0)
    incl = jnp.stack([_order_mask(c, rev, False, c) for rev in revs], axis=0)
    q_dec = q * ((GLA_DK ** -0.5) * jnp.exp(cum))
    k_dec = k * jnp.exp(-cum)
    k_end = k * jnp.exp(tot - cum)
    scores = jnp.where(incl, _bdot_nt(q_dec, k_dec), 0.0)
    state = s_ref[...]
    o = _bdot(scores, v) + _bdot_nt(q_dec, state)
    s_ref[...] = jnp.exp(tot) * state + _bdot_tn(v, k_end)
    return o


def _gla_kernel(qf, kf, vf, alf, qb, kb, vb, alb, wa_ref, ba_ref, of_ref, ob_ref, s_ref):
    @pl.when(pl.program_id(1) == 0)
    def _():
        s_ref[...] = jnp.zeros_like(s_ref)

    c = qf.shape[1]
    qs, ks, vs, cums, revs = [], [], [], [], []
    for d, (q_ref, k_ref, v_ref, al_ref) in enumerate(((qf, kf, vf, alf), (qb, kb, vb, alb))):
        pre = _dot3(al_ref[0], wa_ref[d]) + ba_ref[d:d + 1]
        log_a = (jnp.minimum(pre, 0.0) - jnp.log(1.0 + jnp.exp(-jnp.abs(pre)))) * (1.0 / GLA_TAU)
        cum = _mask_dot(_order_mask(c, d == 1, False, c), log_a)
        for h in range(GLA_HEADS):
            ksl = slice(h * GLA_DK, (h + 1) * GLA_DK)
            qs.append(q_ref[0, :, ksl])
            ks.append(k_ref[0, :, ksl])
            vs.append(v_ref[0, :, h * GLA_DV:(h + 1) * GLA_DV])
            cums.append(cum[:, ksl])
            revs.append(d == 1)
    o = _gla_chunks(jnp.stack(qs), jnp.stack(ks), jnp.stack(vs), jnp.stack(cums), revs, s_ref)
    for d, o_ref in enumerate((of_ref, ob_ref)):
        for h in range(GLA_HEADS):
            o_ref[0, :, h * GLA_DV:(h + 1) * GLA_DV] = o[d * GLA_HEADS + h].astype(o_ref.dtype)


def gla_scan(z, wa, ba):
    b, s, _ = z.shape
    n = s // CHUNK
    c = CHUNK

    def fwd(off, w):
        return pl.BlockSpec((1, c, w), lambda bi, t: (bi, t, off // w))

    def bwd(off, w):
        return pl.BlockSpec((1, c, w), lambda bi, t: (bi, n - 1 - t, off // w))

    in_specs = [fwd(GQ_OFF, GLA_KW), fwd(GK_OFF, GLA_KW), fwd(GV_OFF, GLA_WIDTH), fwd(AL_OFF, LANES),
                bwd(GQ_OFF, GLA_KW), bwd(GK_OFF, GLA_KW), bwd(GV_OFF, GLA_WIDTH), bwd(AL_OFF, LANES),
                pl.BlockSpec(wa.shape, lambda bi, t: (0, 0, 0)), pl.BlockSpec(ba.shape, lambda bi, t: (0, 0))]
    out_specs = [pl.BlockSpec((1, c, GLA_WIDTH), lambda bi, t: (bi, t, 0)),
                 pl.BlockSpec((1, c, GLA_WIDTH), lambda bi, t: (bi, n - 1 - t, 0))]
    return pl.pallas_call(
        _gla_kernel,
        grid=(b, n),
        in_specs=in_specs,
        out_specs=out_specs,
        out_shape=[jax.ShapeDtypeStruct((b, s, GLA_WIDTH), BF16)] * 2,
        scratch_shapes=[pltpu.VMEM((2 * GLA_HEADS, GLA_DV, GLA_DK), F32)],
        compiler_params=_cparams(("parallel", "arbitrary")),
        name="gla_scan",
    )(z, z, z, z, z, z, z, z, wa, ba)


def _gla_post_kernel(of_ref, ob_ref, gate_ref, nw_ref, y_ref):
    o = of_ref[0].astype(F32) + ob_ref[0].astype(F32)
    ms = jnp.mean(o * o, axis=-1, keepdims=True)
    y = o * lax.rsqrt(ms + EPS) * nw_ref[...]
    y_ref[0] = (y * _silu(gate_ref[0])).astype(y_ref.dtype)


def gla_post(o_f, o_b, z, norm_w, tm):
    b, s, w = o_f.shape
    blk = pl.BlockSpec((1, tm, GLA_DV), lambda bi, t, h: (bi, t, h))
    return pl.pallas_call(
        _gla_post_kernel,
        grid=(b, s // tm, GLA_HEADS),
        in_specs=[blk, blk, pl.BlockSpec((1, tm, GLA_DV), lambda bi, t, h: (bi, t, GG_OFF // GLA_DV + h)),
                  pl.BlockSpec((1, GLA_DV), lambda bi, t, h: (0, 0))],
        out_specs=blk,
        out_shape=jax.ShapeDtypeStruct((b, s, w), BF16),
        compiler_params=_cparams(("parallel", "parallel", "parallel")),
        name="gla_post",
    )(o_f, o_b, z, norm_w.reshape(1, GLA_DV))


def _seg_sum(x, e, et):
    return _dot_mask(_dot_mask(x, e), et)


def _rwkv_prep_kernel(z_ref, zp_ref, zn_ref, y_ref, yp_ref, yn_ref, mu_ref, w0_ref, w2f_ref, w2b_ref, a0_ref, a2_ref, g2_ref,
                      kk_ref, ka_ref, rk_ref, e_ref, et_ref,
                      r_out, k_out, v_out, kk_out, b_out, cumf_out, excf_out, cumb_out, excb_out, g_out, bonus_out):
    t = pl.program_id(1)
    tm = z_ref.shape[1]
    row = _iota2((tm, 1), 0)
    first = t == 0
    last = t == pl.num_programs(1) - 1

    def shifted(lo, hi):
        cur, prv, nxt, base = (z_ref, zp_ref, zn_ref, R_OFF) if lo < GLR_OFF else (y_ref, yp_ref, yn_ref, GLR_OFF)
        z = cur[0, :, lo - base:hi - base]
        prev_row = jnp.where(first, 0.0, prv[0, 7:8, lo - base:hi - base])
        next_row = jnp.where(last, 0.0, nxt[0, 0:1, lo - base:hi - base])
        zprev = jnp.where(row == 0, prev_row, pltpu.roll(z, 1, axis=0))
        znext = jnp.where(row == tm - 1, next_row, pltpu.roll(z, tm - 1, axis=0))
        mu = mu_ref[:, lo - R_OFF:hi - R_OFF]
        return z + mu[0:1] * (zprev - z) + mu[1:2] * (znext - z)

    e = e_ref[...]
    et = et_ref[...]
    a = _sigmoid(a0_ref[...] + _bdot(shifted(ALR_OFF, ALR_OFF + LANES), a2_ref[...]))
    k = shifted(K_OFF, K_OFF + RWKV_WIDTH)
    kk_raw = k * kk_ref[...]
    kk = kk_raw * lax.rsqrt(_seg_sum(kk_raw * kk_raw, e, et) + EPS)
    kk_out[0] = kk.astype(kk_out.dtype)
    b_out[0] = (kk * a).astype(b_out.dtype)
    kmod = k * (1.0 + (a - 1.0) * ka_ref[...])
    k_out[0] = kmod.astype(k_out.dtype)
    r = shifted(R_OFF, R_OFF + RWKV_WIDTH)
    r_out[0] = r.astype(r_out.dtype)
    v = shifted(V_OFF, V_OFF + RWKV_WIDTH)
    v_out[0] = v.astype(v_out.dtype)
    bonus_out[0] = (_seg_sum(r * kmod * rk_ref[...], e, et) * v).astype(bonus_out.dtype)
    g_out[0] = _bdot(_sigmoid(shifted(GLR_OFF, GLR_OFF + GATE_LORA)), g2_ref[...]).astype(g_out.dtype)
    decay_scale = -math.exp(-0.5)
    wf = w0_ref[0:1] + _dot3(jnp.tanh(shifted(WLF_OFF, WLF_OFF + LANES)), w2f_ref[...])
    lwf = decay_scale * _sigmoid(wf)
    cum_f = _mask_dot(_order_mask(tm, False, False, CHUNK), lwf)
    cumf_out[0] = cum_f
    excf_out[0] = cum_f - lwf
    wb = w0_ref[1:2] + _dot3(jnp.tanh(shifted(WLB_OFF, WLB_OFF + LANES)), w2b_ref[...])
    lwb = decay_scale * _sigmoid(wb)
    cum_b = _mask_dot(_order_mask(tm, True, False, CHUNK), lwb)
    cumb_out[0] = cum_b
    excb_out[0] = cum_b - lwb


def rwkv_prep(z, mu, w0, w2f, w2b, a0, a2, g2, k_k, k_a, r_k, seg_e, seg_et, tm):
    b, s, _ = z.shape
    nt = s // tm
    hb = tm // 8
    full = lambda arr: pl.BlockSpec(arr.shape, lambda bi, t: (0,) * arr.ndim)
    in_specs = []
    for width, off in ((RWKV_MAIN, R_OFF), (RWKV_SMALL, GLR_OFF)):
        cb = off // width
        in_specs += [pl.BlockSpec((1, tm, width), lambda bi, t, cb=cb: (bi, t, cb)),
                     pl.BlockSpec((1, 8, width), lambda bi, t, cb=cb: (bi, jnp.maximum(t * hb - 1, 0), cb)),
                     pl.BlockSpec((1, 8, width), lambda bi, t, cb=cb: (bi, jnp.minimum((t + 1) * hb, nt * hb - 1), cb))]
    params = [mu, w0, w2f, w2b, a0, a2, g2, k_k, k_a, r_k, seg_e, seg_et]
    in_specs += [full(p) for p in params]
    out_blk = pl.BlockSpec((1, tm, RWKV_WIDTH), lambda bi, t: (bi, t, 0))
    return pl.pallas_call(
        _rwkv_prep_kernel,
        grid=(b, nt),
        in_specs=in_specs,
        out_specs=[out_blk] * 11,
        out_shape=[jax.ShapeDtypeStruct((b, s, RWKV_WIDTH), dt) for dt in [BF16] * 5 + [F32] * 4 + [BF16] * 2],
        compiler_params=_cparams(("parallel", "parallel")),
        name="rwkv_prep",
    )(z, z, z, z, z, z, *params)


def _stack_heads(x):
    lane = _iota2(x.shape, 2)
    return jnp.concatenate([jnp.where(lane < RWKV_HEAD, x, 0.0), jnp.where(lane < RWKV_HEAD, 0.0, x)], axis=1)


def _rwkv_chunks(r, k, v, kk, bb, cum, exc, revs, g_ref):
    nb, c, _ = r.shape
    n2 = 2 * c
    tot = jnp.stack([cum[j, (0 if rev else c - 1)][None] for j, rev in enumerate(revs)], axis=0)
    e_neg = jnp.exp(-cum)
    e_end = jnp.exp(tot - cum)
    al2 = _stack_heads(-kk * jnp.exp(exc))
    rb2 = _stack_heads(r * jnp.exp(cum))
    bt2 = _stack_heads(bb * e_neg)
    kt2 = _stack_heads(k * e_neg)
    be2 = _stack_heads(bb * e_end)
    ke2 = _stack_heads(k * e_end)
    v2 = _stack_heads(v)
    gram = _bdot_nt(jnp.concatenate([al2, rb2], axis=1), jnp.concatenate([bt2, kt2], axis=1))
    strict = jnp.stack([_order_mask(n2, rev, True, c) for rev in revs], axis=0)
    incl = jnp.stack([_order_mask(n2, rev, False, c) for rev in revs], axis=0)
    a_ab = jnp.where(strict, gram[:, :n2, :n2], 0.0)
    a_ak = jnp.where(strict, gram[:, :n2, n2:], 0.0)
    a_rb = jnp.where(incl, gram[:, n2:, :n2], 0.0)
    a_rk = jnp.where(incl, gram[:, n2:, n2:], 0.0)
    t_inv = _neumann_inverse(a_ab)
    wu = _bdot(t_inv, jnp.concatenate([al2, _bdot(a_ak, v2)], axis=2))
    g = g_ref[...]
    proj = _bdot_nt(jnp.concatenate([wu[:, :, :LANES], rb2], axis=1), g)
    u2 = proj[:, :n2] + wu[:, :, LANES:]
    uv = jnp.concatenate([u2, v2], axis=1)
    o2 = proj[:, n2:] + _bdot(jnp.concatenate([a_rb, a_rk], axis=2), uv)
    g_ref[...] = g * jnp.exp(tot) + _bdot_tn(uv, jnp.concatenate([be2, ke2], axis=1))
    return o2[:, :c] + o2[:, c:]


def _rwkv_scan_kernel(*refs, pairs):
    fwd_refs, bwd_refs = refs[0:7], refs[7:14]
    of_ref, ob_ref, g_ref = refs[14:17]

    @pl.when(pl.program_id(2) == 0)
    def _():
        g_ref[...] = jnp.zeros_like(g_ref)

    revs = [False] * pairs + [True] * pairs
    c = CHUNK
    sub_steps = of_ref.shape[1] // c
    for sub in range(sub_steps):
        rows_of = (slice(sub * c, (sub + 1) * c), slice((sub_steps - 1 - sub) * c, (sub_steps - sub) * c))
        operands = []
        for t in range(7):
            operands.append(jnp.stack([ref[0, rows_of[d], p * LANES:(p + 1) * LANES].astype(F32)
                                       for d, ref in enumerate((fwd_refs[t], bwd_refs[t])) for p in range(pairs)],
                                      axis=0))
        o = _rwkv_chunks(*operands, revs, g_ref)
        for d, o_ref in enumerate((of_ref, ob_ref)):
            for p in range(pairs):
                o_ref[0, rows_of[d], p * LANES:(p + 1) * LANES] = o[d * pairs + p].astype(o_ref.dtype)


def rwkv_scan(r, k, v, kk, bb, cum_f, exc_f, cum_b, exc_b, pairs):
    b, s, _ = r.shape
    c = CHUNK * SCAN_CHUNKS_PER_STEP
    n = s // c
    w = pairs * LANES
    fwd = pl.BlockSpec((1, c, w), lambda bi, p, t: (bi, t, p))
    bwd = pl.BlockSpec((1, c, w), lambda bi, p, t: (bi, n - 1 - t, p))
    return pl.pallas_call(
        functools.partial(_rwkv_scan_kernel, pairs=pairs),
        grid=(b, RWKV_WIDTH // w, n),
        in_specs=[fwd] * 7 + [bwd] * 7,
        out_specs=[fwd, bwd],
        out_shape=[jax.ShapeDtypeStruct((b, s, RWKV_WIDTH), BF16)] * 2,
        scratch_shapes=[pltpu.VMEM((2 * pairs, LANES, LANES), F32)],
        compiler_params=_cparams(("parallel", "parallel", "arbitrary")),
        name="rwkv_scan",
    )(r, k, v, kk, bb, cum_f, exc_f, r, k, v, kk, bb, cum_b, exc_b)


def _rwkv_post_kernel(of_ref, ob_ref, bonus_ref, g_ref, lnw_ref, lnb_ref, e_ref, et_ref, y_ref):
    e = e_ref[...]
    et = et_ref[...]
    o = of_ref[0].astype(F32) + ob_ref[0].astype(F32)
    mean = _seg_sum(o, e, et) * (1.0 / RWKV_HEAD)
    cen = o - mean
    var = _seg_sum(cen * cen, e, et) * (1.0 / RWKV_HEAD)
    y = cen * lax.rsqrt(var + RWKV_LN_EPS) * lnw_ref[...] + lnb_ref[...]
    y_ref[0] = ((y + bonus_ref[0].astype(F32)) * g_ref[0].astype(F32)).astype(y_ref.dtype)


def rwkv_post(o_f, o_b, bonus, g, ln_w, ln_b, seg_e, seg_et, tm):
    b, s, w = o_f.shape
    blk = pl.BlockSpec((1, tm, w), lambda bi, t: (bi, t, 0))
    full = lambda arr: pl.BlockSpec(arr.shape, lambda bi, t: (0,) * arr.ndim)
    params = [ln_w.reshape(1, w), ln_b.reshape(1, w), seg_e, seg_et]
    return pl.pallas_call(
        _rwkv_post_kernel,
        grid=(b, s // tm),
        in_specs=[blk] * 4 + [full(p) for p in params],
        out_specs=blk,
        out_shape=jax.ShapeDtypeStruct((b, s, w), BF16),
        compiler_params=_cparams(("parallel", "parallel")),
        name="rwkv_post",
    )(o_f, o_b, bonus, g, *params)


def _gdn_conv_kernel(z_ref, zp_ref, zn_ref, cw_ref, sc_ref, o_ref, *, l2norm):
    t = pl.program_id(1)
    tm = z_ref.shape[1]
    halo = zp_ref.shape[1]
    ext = jnp.concatenate([jnp.where(t == 0, 0.0, zp_ref[0].astype(F32)), z_ref[0].astype(F32),
                           jnp.where(t == pl.num_programs(1) - 1, 0.0, zn_ref[0].astype(F32))], axis=0)
    n = tm + 2 * halo
    pad = cw_ref.shape[0] // 2
    y = 0.0
    for j in range(cw_ref.shape[0]):
        shifted = ext if j == pad else pltpu.roll(ext, (pad - j) % n, axis=0)
        y = y + cw_ref[j:j + 1] * shifted[halo:halo + tm]
    y = _silu(y)
    if l2norm:
        for h in range(y.shape[1] // GDN_DK):
            sl = slice(h * GDN_DK, (h + 1) * GDN_DK)
            yh = y[:, sl]
            ss = jnp.sum(yh * yh, axis=-1, keepdims=True)
            o_ref[0, :, sl] = (yh * lax.rsqrt(ss + EPS) * sc_ref[0:1, sl]).astype(o_ref.dtype)
    else:
        o_ref[0] = y.astype(o_ref.dtype)


def gdn_conv(z, conv_w, scale, col_off, width, l2norm, tm, cw):
    b, s, _ = z.shape
    nt = s // tm
    halo = 8 * (4 // z.dtype.itemsize)
    hb = tm // halo
    cb = col_off // cw
    in_specs = [pl.BlockSpec((1, tm, cw), lambda bi, t, c: (bi, t, cb + c)),
                pl.BlockSpec((1, halo, cw), lambda bi, t, c: (bi, jnp.maximum(t * hb - 1, 0), cb + c)),
                pl.BlockSpec((1, halo, cw), lambda bi, t, c: (bi, jnp.minimum((t + 1) * hb, nt * hb - 1), cb + c)),
                pl.BlockSpec((conv_w.shape[0], cw), lambda bi, t, c: (0, cb + c)),
                pl.BlockSpec((1, cw), lambda bi, t, c: (0, c))]
    return pl.pallas_call(
        functools.partial(_gdn_conv_kernel, l2norm=l2norm),
        grid=(b, nt, width // cw),
        in_specs=in_specs,
        out_specs=pl.BlockSpec((1, tm, cw), lambda bi, t, c: (bi, t, c)),
        out_shape=jax.ShapeDtypeStruct((b, s, width), BF16),
        compiler_params=_cparams(("parallel", "parallel", "parallel")),
        name="gdn_conv_norm" if l2norm else "gdn_conv",
    )(z, z, z, conv_w, scale)


def _gdn_gates_kernel(zs_ref, na_ref, dtb_ref, o_ref):
    zs = zs_ref[0]
    tm = zs.shape[0]
    gg = na_ref[...] * _softplus(zs + dtb_ref[...])
    gam_f = _mask_dot(_order_mask(tm, False, False, CHUNK), gg)
    gam_b = _mask_dot(_order_mask(tm, True, False, CHUNK), gg)
    lane = _iota2(zs.shape, 1)
    o_ref[0] = jnp.where(lane < 2 * GDN_VHEADS, _sigmoid(zs), jnp.where(lane < 3 * GDN_VHEADS, gam_f, gam_b))


def gdn_gates(zs, neg_a, dtb, tm):
    b, s, w = zs.shape
    blk = pl.BlockSpec((1, tm, w), lambda bi, t: (bi, t, 0))
    row = pl.BlockSpec((1, w), lambda bi, t: (0, 0))
    return pl.pallas_call(
        _gdn_gates_kernel,
        grid=(b, s // tm),
        in_specs=[blk, row, row],
        out_specs=blk,
        out_shape=jax.ShapeDtypeStruct((b, s, w), F32),
        compiler_params=_cparams(("parallel", "parallel")),
        name="gdn_gates",
    )(zs, neg_a, dtb)


def _gdn_chunks(q, k, v2, beta2, gam2, revs, s_ref):
    nb, c, _ = q.shape
    n2 = 2 * c
    per_batch = lambda fn: jnp.stack([fn(r) for r in revs], axis=0)
    incl = per_batch(lambda r: _order_mask(n2, r, False, c))
    strict = per_batch(lambda r: _order_mask(n2, r, True, c))
    gam_c = jnp.broadcast_to(gam2, (nb, n2, n2))
    beta_b = jnp.broadcast_to(beta2, (nb, n2, n2))
    gam_r = jnp.swapaxes(gam_c, 1, 2)
    tot_b = jnp.stack([jnp.concatenate(
        [jnp.broadcast_to(gam_c[j, h * c + (0 if r else c - 1)][None], (c, n2)) for h in range(2)], axis=0)
        for j, r in enumerate(revs)], axis=0)
    diff = gam_c - gam_r
    decay = jnp.exp(jnp.where(incl, diff, 0.0))
    dec_s = jnp.where(strict, decay, 0.0)
    dec_i = jnp.where(incl, decay, 0.0)
    k2 = jnp.concatenate([k, k], axis=1)
    q2 = jnp.concatenate([q, q], axis=1)
    gram = _bdot_nt(jnp.concatenate([k, q], axis=1), k2)
    a_mat = jnp.concatenate([gram[:, :c], gram[:, :c]], axis=1) * (beta_b * dec_s)
    qk = jnp.concatenate([gram[:, c:], gram[:, c:]], axis=1) * dec_i
    t_inv = _neumann_inverse(-a_mat)
    e_gam = jnp.exp(gam_c)
    uw = _bdot(t_inv, jnp.concatenate([v2 * beta_b, k2 * (beta_b * e_gam)], axis=2))
    u2 = uw[:, :, :GDN_DV]
    w2 = uw[:, :, GDN_DV:]
    qd2 = q2 * e_gam
    ke2 = k2 * jnp.exp(tot_b - gam_c)
    dl2 = jnp.exp(tot_b)
    ws, qs = [], []
    for e in range(2):
        rows = slice(e * c, (e + 1) * c)
        both = _bdot(jnp.concatenate([w2[:, rows], qd2[:, rows]], axis=1), s_ref[e])
        ws.append(both[:, :c])
        qs.append(both[:, c:])
    vnew2 = u2 - jnp.concatenate(ws, axis=1)
    o2 = jnp.concatenate(qs, axis=1) + _bdot(qk, vnew2)
    for e in range(2):
        rows = slice(e * c, (e + 1) * c)
        s_ref[e] = s_ref[e] * dl2[:, e * c:e * c + 1, :] + _bdot_tn(ke2[:, rows], vnew2[:, rows])
    return o2


def _gdn_scan_kernel(qf, kf, vf, gf, qb, kb, vb, gb, of_ref, ob_ref, s_ref, *, heads):
    @pl.when(pl.program_id(2) == 0)
    def _():
        s_ref[...] = jnp.zeros_like(s_ref)

    c = CHUNK
    sub_steps = qf.shape[1] // c
    lane = _iota2((c, LANES), 1)

    def column(gates, idx):
        return jnp.sum(jnp.where(lane == idx, gates, 0.0), axis=1, keepdims=True)

    for sub in range(sub_steps):
        rows_of = (slice(sub * c, (sub + 1) * c), slice((sub_steps - 1 - sub) * c, (sub_steps - sub) * c))
        qs, ks, vs, betas, gams, revs = [], [], [], [], [], []
        for d, (q_ref, k_ref, v_ref, g_ref) in enumerate(((qf, kf, vf, gf), (qb, kb, vb, gb))):
            rows = rows_of[d]
            gates = g_ref[0, rows, :]
            for i in range(heads):
                vh = 2 * (pl.program_id(1) * heads + i)
                ksl = slice(i * GDN_DK, (i + 1) * GDN_DK)
                qs.append(q_ref[0, rows, ksl].astype(F32))
                ks.append(k_ref[0, rows, ksl].astype(F32))
                vs.append(jnp.concatenate(
                    [v_ref[0, rows, (2 * i + e) * GDN_DV:(2 * i + e + 1) * GDN_DV].astype(F32) for e in range(2)], axis=0))
                betas.append(jnp.concatenate([column(gates, d * GDN_VHEADS + vh + e) for e in range(2)], axis=0))
                gams.append(jnp.concatenate([column(gates, (2 + d) * GDN_VHEADS + vh + e) for e in range(2)], axis=0))
                revs.append(d == 1)
        o2 = _gdn_chunks(jnp.stack(qs), jnp.stack(ks), jnp.stack(vs), jnp.stack(betas), jnp.stack(gams), revs, s_ref)
        for d, o_ref in enumerate((of_ref, ob_ref)):
            for i in range(heads):
                j = d * heads + i
                for e in range(2):
                    o_ref[0, rows_of[d], (2 * i + e) * GDN_DV:(2 * i + e + 1) * GDN_DV] = (
                        o2[j, e * c:(e + 1) * c].astype(o_ref.dtype))


def gdn_scan(qk, v, gates, heads):
    b, s, _ = v.shape
    c = CHUNK * SCAN_CHUNKS_PER_STEP
    n = s // c
    ng = GDN_KHEADS // heads

    def spec(w, off, rev):
        if rev:
            return pl.BlockSpec((1, c, w), lambda bi, h, t: (bi, n - 1 - t, off + h))
        return pl.BlockSpec((1, c, w), lambda bi, h, t: (bi, t, off + h))

    def gspec(rev):
        if rev:
            return pl.BlockSpec((1, c, LANES), lambda bi, h, t: (bi, n - 1 - t, 0))
        return pl.BlockSpec((1, c, LANES), lambda bi, h, t: (bi, t, 0))

    kw = heads * GDN_DK
    vw = heads * 2 * GDN_DV
    in_specs = [spec(kw, 0, False), spec(kw, ng, False), spec(vw, 0, False), gspec(False),
                spec(kw, 0, True), spec(kw, ng, True), spec(vw, 0, True), gspec(True)]
    return pl.pallas_call(
        functools.partial(_gdn_scan_kernel, heads=heads),
        grid=(b, ng, n),
        in_specs=in_specs,
        out_specs=[spec(vw, 0, False), spec(vw, 0, True)],
        out_shape=[jax.ShapeDtypeStruct((b, s, GDN_VW), BF16)] * 2,
        scratch_shapes=[pltpu.VMEM((2, 2 * heads, GDN_DK, GDN_DV), F32)],
        compiler_params=_cparams(("parallel", "parallel", "arbitrary")),
        name="gdn_scan",
    )(qk, qk, v, gates, qk, qk, v, gates)


def _gdn_post_kernel(of_ref, ob_ref, gate_ref, nw_ref, y_ref):
    nw = nw_ref[...]
    for h in range(of_ref.shape[2] // GDN_DV):
        sl = slice(h * GDN_DV, (h + 1) * GDN_DV)
        o = of_ref[0, :, sl].astype(F32) + ob_ref[0, :, sl].astype(F32)
        ms = jnp.mean(o * o, axis=-1, keepdims=True)
        y = o * lax.rsqrt(ms + EPS) * nw
        y_ref[0, :, sl] = (y * _silu(gate_ref[0, :, sl].astype(F32))).astype(y_ref.dtype)


def gdn_post(o_f, o_b, z, gate_off, norm_w, tm, cw):
    b, s, w = o_f.shape
    blk = pl.BlockSpec((1, tm, cw), lambda bi, t, c: (bi, t, c))
    gb = gate_off // cw
    return pl.pallas_call(
        _gdn_post_kernel,
        grid=(b, s // tm, w // cw),
        in_specs=[blk, blk, pl.BlockSpec((1, tm, cw), lambda bi, t, c: (bi, t, gb + c)),
                  pl.BlockSpec((1, GDN_DV), lambda bi, t, c: (0, 0))],
        out_specs=blk,
        out_shape=jax.ShapeDtypeStruct((b, s, w), BF16),
        compiler_params=_cparams(("parallel", "parallel", "parallel")),
        name="gdn_post",
    )(o_f, o_b, z, norm_w.reshape(1, GDN_DV))


def _moe_router_kernel(x_ref, g_ref, w_ref, b_ref, route_ref, counts_ref):
    @pl.when(pl.program_id(0) == 0)
    def _():
        counts_ref[...] = jnp.zeros_like(counts_ref)

    x = x_ref[...]
    tm = x.shape[0]
    ms = jnp.mean(x * x, axis=-1, keepdims=True)
    h = x * lax.rsqrt(ms + EPS) * g_ref[...]
    lane_i = _iota2((tm, LANES), 1)
    lane = lane_i.astype(F32)
    lane_grp = lax.shift_right_logical(lane_i, 3).astype(F32)
    neg = -jnp.inf
    logits = _dot3(h, w_ref[...]) + b_ref[...]
    gl = jnp.where((lane_i >= N_EXPERTS) & (lane_i < N_EXPERTS + N_GROUPS), logits, neg)
    gmax = jnp.max(gl, axis=-1, keepdims=True)
    gidx = jnp.min(jnp.where(gl == gmax, lane, float(LANES)), axis=-1, keepdims=True) - float(N_EXPERTS)
    grp_w = 1.0 / jnp.sum(jnp.exp(gl - gmax), axis=-1, keepdims=True)
    sel = jnp.where((lane_i < N_EXPERTS) & (lane_grp == gidx), logits, neg)
    m1 = jnp.max(sel, axis=-1, keepdims=True)
    i1 = jnp.min(jnp.where(sel == m1, lane, float(LANES)), axis=-1, keepdims=True)
    sel2 = jnp.where(lane == i1, neg, sel)
    m2 = jnp.max(sel2, axis=-1, keepdims=True)
    i2 = jnp.min(jnp.where(sel2 == m2, lane, float(LANES)), axis=-1, keepdims=True)
    e2 = jnp.exp(m2 - m1)
    w1 = grp_w / (1.0 + e2)
    w2 = grp_w * e2 / (1.0 + e2)
    hits = jnp.where((lane == i1) | (lane == i2), 1.0, 0.0)
    before = counts_ref[...] + _bdot(_order_mask(tm, False, True, tm), hits)
    r1 = jnp.sum(jnp.where(lane == i1, before, 0.0), axis=-1, keepdims=True)
    r2 = jnp.sum(jnp.where(lane == i2, before, 0.0), axis=-1, keepdims=True)
    counts_ref[...] += jnp.sum(hits, axis=0, keepdims=True)
    route_ref[...] = jnp.where(lane_i == 0, i1, jnp.where(lane_i == 1, i2, jnp.where(
        lane_i == 2, w1, jnp.where(lane_i == 3, w2, jnp.where(lane_i == 4, r1, jnp.where(lane_i == 5, r2, 0.0))))))


def moe_router(x, g, w_group, b_group, w_router, b_router, tm):
    m, d = x.shape
    tm = min(tm, m)
    pad = lambda w: jnp.pad(w, ((0, 0), (0, LANES - w.shape[1])))
    full = lambda arr: pl.BlockSpec(arr.shape, lambda i: (0, 0))
    params = [g.reshape(1, d), pad(jnp.concatenate([w_router, w_group], axis=1)),
              pad(jnp.concatenate([b_router, b_group]).reshape(1, -1))]
    return pl.pallas_call(
        _moe_router_kernel,
        grid=(m // tm,),
        in_specs=[pl.BlockSpec((tm, d), lambda i: (i, 0))] + [full(p) for p in params],
        out_specs=[pl.BlockSpec((tm, LANES), lambda i: (i, 0)), pl.BlockSpec((1, LANES), lambda i: (0, 0))],
        out_shape=[jax.ShapeDtypeStruct((m, LANES), F32), jax.ShapeDtypeStruct((1, LANES), F32)],
        compiler_params=_cparams(("arbitrary",)),
        name="moe_router",
    )(x, *params)


def _route_tables(route, counts, tm, n_tiles):
    counts = counts[0, :N_EXPERTS].astype(jnp.int32)
    padded = ((counts + tm - 1) // tm) * tm
    ends = jnp.cumsum(padded)
    starts = ends - padded
    experts = route[:, 0:2].astype(jnp.int32)
    pos = (starts[experts] + route[:, 4:6].astype(jnp.int32)).reshape(-1)
    tile_start = jnp.arange(n_tiles, dtype=jnp.int32) * tm
    tile_e = jnp.minimum(jnp.sum(tile_start[:, None] >= ends[None, :], axis=1), N_EXPERTS - 1).astype(jnp.int32)
    n_valid = (ends[-1:] // tm).astype(jnp.int32)
    fill = jnp.stack([jnp.maximum(ends - tm, 0), (padded > 0).astype(jnp.int32)]).astype(jnp.int32)
    return pos, tile_e, n_valid, fill


def _moe_dispatch_kernel(pos_ref, nvalid_ref, fill_ref, x_ref, g_ref, hs_hbm, hbuf, zbuf, sem_rows, sem_fill, *,
                         n_tiles):
    i = pl.program_id(0)
    tm = x_ref.shape[0]
    row_tile = zbuf.shape[0]

    @pl.when(i == 0)
    def _():
        zbuf[...] = jnp.zeros_like(zbuf)
        fills = [(fill_ref[1, e] > 0, fill_ref[0, e]) for e in range(N_EXPERTS)]
        fills += [(nvalid_ref[0] + k < n_tiles, (nvalid_ref[0] + k) * row_tile) for k in range(N_EXPERTS)]
        fills = [(cond, pl.multiple_of(start, row_tile)) for cond, start in fills]
        for cond, start in fills:
            @pl.when(cond)
            def _(start=start):
                pltpu.make_async_copy(zbuf, hs_hbm.at[pl.ds(start, row_tile)], sem_fill).start()
        for cond, start in fills:
            @pl.when(cond)
            def _(start=start):
                pltpu.make_async_copy(zbuf, hs_hbm.at[pl.ds(start, row_tile)], sem_fill).wait()

    buf = i % 2
    x = x_ref[...]
    ms = jnp.mean(x * x, axis=-1, keepdims=True)
    hbuf[buf] = x * lax.rsqrt(ms + EPS) * g_ref[...]

    def issue(r, carry):
        base = 2 * (i * tm + r)
        for slot in range(2):
            pltpu.make_async_copy(hbuf.at[buf, pl.ds(r, 1)], hs_hbm.at[pl.ds(pos_ref[base + slot], 1)],
                                  sem_rows.at[buf]).start()
        return carry

    lax.fori_loop(0, tm, issue, 0, unroll=8)

    def drain(b):
        for _ in range(2):
            pltpu.make_async_copy(hbuf.at[b], hs_hbm.at[pl.ds(0, tm)], sem_rows.at[b]).wait()

    @pl.when(i > 0)
    def _():
        drain(1 - buf)

    @pl.when(i == pl.num_programs(0) - 1)
    def _():
        drain(buf)


def moe_dispatch(x, g, pos, n_valid, fill, n_tiles, tm):
    m, d = x.shape
    tm_x = min(256, m)
    return pl.pallas_call(
        functools.partial(_moe_dispatch_kernel, n_tiles=n_tiles),
        grid_spec=pltpu.PrefetchScalarGridSpec(
            num_scalar_prefetch=3,
            grid=(m // tm_x,),
            in_specs=[pl.BlockSpec((tm_x, d), lambda i, p, nv, fl: (i, 0)),
                      pl.BlockSpec((1, d), lambda i, p, nv, fl: (0, 0))],
            out_specs=pl.BlockSpec(memory_space=pl.ANY),
            scratch_shapes=[pltpu.VMEM((2, tm_x, d), F32), pltpu.VMEM((tm, d), F32),
                            pltpu.SemaphoreType.DMA((2,)), pltpu.SemaphoreType.DMA(())]),
        out_shape=jax.ShapeDtypeStruct((n_tiles * tm, d), F32),
        compiler_params=_cparams(("arbitrary",)),
        name="moe_dispatch",
    )(pos, n_valid, fill, x, g.reshape(1, d))


def _moe_ffn_kernel(tile_e_ref, nvalid_ref, h_ref, wg_ref, wu_ref, wd_ref, y_ref, wg_b, wu_b, wd_b):
    i = pl.program_id(0)

    @pl.when((i == 0) | (tile_e_ref[i] != tile_e_ref[jnp.maximum(i - 1, 0)]))
    def _():
        wg_b[...] = wg_ref[0, 0].astype(BF16)
        wu_b[...] = wu_ref[0, 0].astype(BF16)
        wd_b[...] = wd_ref[0, 0].astype(BF16)

    @pl.when(i < nvalid_ref[0])
    def _():
        x = h_ref[...].astype(BF16)
        hid = _silu(_dot(x, wg_b[...])) * _dot(x, wu_b[...])
        y_ref[...] = _dot(hid.astype(BF16), wd_b[...])

    @pl.when(i >= nvalid_ref[0])
    def _():
        y_ref[...] = jnp.zeros_like(y_ref)


def moe_ffn(h_sorted, tile_e, n_valid, w_gate, w_up, w_down, layer, tm):
    _, d = h_sorted.shape
    n_tiles = tile_e.shape[0]
    ff = w_gate.shape[-1]
    wspec = lambda shape: pl.BlockSpec((1, 1) + shape, lambda i, te, nv: (layer, te[i], 0, 0))
    return pl.pallas_call(
        _moe_ffn_kernel,
        grid_spec=pltpu.PrefetchScalarGridSpec(
            num_scalar_prefetch=2,
            grid=(n_tiles,),
            in_specs=[pl.BlockSpec((tm, d), lambda i, te, nv: (i, 0)), wspec((d, ff)), wspec((d, ff)), wspec((ff, d))],
            out_specs=pl.BlockSpec((tm, d), lambda i, te, nv: (i, 0)),
            scratch_shapes=[pltpu.VMEM((d, ff), BF16), pltpu.VMEM((d, ff), BF16), pltpu.VMEM((ff, d), BF16)]),
        out_shape=jax.ShapeDtypeStruct((n_tiles * tm, d), F32),
        compiler_params=_cparams(("arbitrary",)),
        name="moe_ffn",
    )(tile_e, n_valid, h_sorted, w_gate, w_up, w_down)


def _moe_combine_kernel(pos_ref, x_ref, route_ref, y_hbm, g_ref, o_ref, ybuf, sems, *, final_norm):
    i = pl.program_id(0)
    tm = x_ref.shape[0]

    def gather(tile, buf):
        def issue(r, carry):
            base = 2 * (tile * tm + r)
            for slot in range(2):
                pltpu.make_async_copy(y_hbm.at[pl.ds(pos_ref[base + slot], 1)], ybuf.at[buf, slot, pl.ds(r, 1)],
                                      sems.at[buf]).start()
            return carry

        lax.fori_loop(0, tm, issue, 0, unroll=8)

    @pl.when(i == 0)
    def _():
        gather(0, 0)

    buf = i % 2

    @pl.when(i + 1 < pl.num_programs(0))
    def _():
        gather(i + 1, 1 - buf)

    for slot in range(2):
        pltpu.make_async_copy(y_hbm.at[pl.ds(0, tm)], ybuf.at[buf, slot], sems.at[buf]).wait()
    route = route_ref[...]
    o = x_ref[...] + route[:, 2:3] * ybuf[buf, 0] + route[:, 3:4] * ybuf[buf, 1]
    if final_norm:
        ms = jnp.mean(o * o, axis=-1, keepdims=True)
        o = o * lax.rsqrt(ms + EPS) * g_ref[...]
    o_ref[...] = o


def moe_combine(x, route, y_sorted, pos, norm_g, tm):
    m, d = x.shape
    tm = min(tm, m)
    final_norm = norm_g is not None
    g = (norm_g if final_norm else jnp.ones((d,), F32)).reshape(1, d)
    return pl.pallas_call(
        functools.partial(_moe_combine_kernel, final_norm=final_norm),
        grid_spec=pltpu.PrefetchScalarGridSpec(
            num_scalar_prefetch=1,
            grid=(m // tm,),
            in_specs=[pl.BlockSpec((tm, d), lambda i, p: (i, 0)), pl.BlockSpec((tm, LANES), lambda i, p: (i, 0)),
                      pl.BlockSpec(memory_space=pl.ANY), pl.BlockSpec((1, d), lambda i, p: (0, 0))],
            out_specs=pl.BlockSpec((tm, d), lambda i, p: (i, 0)),
            scratch_shapes=[pltpu.VMEM((2, 2, tm, d), F32), pltpu.SemaphoreType.DMA((2,))]),
        out_shape=jax.ShapeDtypeStruct((m, d), F32),
        compiler_params=_cparams(("arbitrary",)),
        name="moe_combine",
    )(pos, x, route, y_sorted, g)


def _pad_rows(w, rows):
    return jnp.pad(w, ((0, rows - w.shape[0]), (0, 0)))


def _even_layer(x, norm_g, w_in, gla_w_alpha, gla_b_alpha, gla_norm, mu, w0, w2, a0, a2, g2,
                k_k, k_a, r_k, ln_w, ln_b, w_out):
    b, s, d = x.shape
    gla_cols, rw = w_in[:, :3104], w_in[:, 3104:]
    mu_g = lambda lo, hi, width: jnp.pad(mu[:, lo:hi], ((0, 0), (0, width - (hi - lo))))
    pad_c = lambda w, width: jnp.pad(w, ((0, 0), (0, width - w.shape[1])))
    w_cat = jnp.concatenate([
        gla_cols[:, 0:3072],
        rw[:, 0:3072], rw[:, 3360:3616],
        pad_c(rw[:, 3072:3168], LANES), pad_c(rw[:, 3168:3264], LANES), pad_c(rw[:, 3264:3360], LANES),
        pad_c(gla_cols[:, 3072:3104], LANES)], axis=1).astype(BF16)
    mu_cat = jnp.concatenate([
        mu[:, 0:3072], mu[:, 3360:3616], mu_g(3072, 3168, LANES), mu_g(3168, 3264, LANES),
        mu_g(3264, 3360, LANES)], axis=1)
    z = norm_matmul(x.reshape(b * s, d), norm_g, w_cat, 1024, 768).reshape(b, s, EVEN_PAD)

    wa = jnp.stack([_pad_rows(gla_w_alpha[0], LANES),
                    jnp.pad(gla_w_alpha[1], ((GLA_LOWRANK, LANES - 2 * GLA_LOWRANK), (0, 0)))])
    gla_f, gla_b = gla_scan(z, wa, gla_b_alpha)
    y_gla = gla_post(gla_f, gla_b, z, gla_norm, 256)

    head_of_lane = jnp.arange(RWKV_WIDTH) // RWKV_HEAD
    seg_e = (head_of_lane[:, None] == jnp.arange(LANES)[None, :]).astype(F32)
    seg_et = seg_e.T
    row = lambda p: p.reshape(1, RWKV_WIDTH)
    r, k, v, kk, bb, cum_f, exc_f, cum_b, exc_b, g, bonus = rwkv_prep(
        z, mu_cat, w0, _pad_rows(w2[0], LANES), _pad_rows(w2[1], LANES), row(a0), _pad_rows(a2, LANES), g2,
        row(k_k), row(k_a), row(r_k), seg_e, seg_et, 128)
    rw_f, rw_b = rwkv_scan(r, k, v, kk, bb, cum_f, exc_f, cum_b, exc_b, RWKV_PAIRS_PER_STEP)
    y_rwkv = rwkv_post(rw_f, rw_b, bonus, g, ln_w, ln_b, seg_e, seg_et, 256)

    w_out = w_out.astype(BF16)
    out = proj_residual([y_gla.reshape(b * s, GLA_WIDTH), y_rwkv.reshape(b * s, RWKV_WIDTH)],
                        [w_out[:GLA_WIDTH], w_out[GLA_WIDTH:]], x.reshape(b * s, d), 512, 1024)
    return out.reshape(b, s, d)


def _odd_layer(x, norm_g, w_in, conv_w, a_log, dt_bias, norm_w, w_out):
    b, s, d = x.shape
    x2 = x.reshape(b * s, d)
    main = GDN_QKV + GDN_VW
    z = norm_matmul(x2, norm_g, w_in[:, :main].astype(BF16), 1024, 1024, BF16).reshape(b, s, main)
    zs = norm_matmul(x2, norm_g, w_in[:, main:].astype(BF16), 512, LANES).reshape(b, s, LANES)
    scale = jnp.concatenate([jnp.full((1, GDN_KW), GDN_DK ** -0.5, F32), jnp.ones((1, GDN_KW), F32)], axis=1)
    qk = gdn_conv(z, conv_w, scale, 0, 2 * GDN_KW, True, 256, 1024)
    v = gdn_conv(z, conv_w, scale, 2 * GDN_KW, GDN_VW, False, 256, 1024)
    zero = jnp.zeros((2 * GDN_VHEADS,), F32)
    neg_a = jnp.concatenate([zero, -jnp.exp(a_log.reshape(-1))]).reshape(1, LANES)
    dtb = jnp.concatenate([zero, dt_bias.reshape(-1)]).reshape(1, LANES)
    gates = gdn_gates(zs, neg_a, dtb, 256)
    o_f, o_b = gdn_scan(qk, v, gates, GDN_HEADS_PER_STEP)
    y = gdn_post(o_f, o_b, z, GDN_QKV, norm_w, 256, 1024)
    out = proj_residual([y.reshape(b * s, GDN_VW)], [w_out.astype(BF16)], x2, 512, 1024)
    return out.reshape(b, s, d)


def _moe_layer(x, norm_g, w_group, b_group, w_router, b_router, w_gate, w_up, w_down, layer, final_g):
    b, s, d = x.shape
    x2 = x.reshape(b * s, d)
    route, counts = moe_router(x2, norm_g, w_group, b_group, w_router, b_router, 256)
    n_tiles = -(-2 * b * s // MOE_ROW_TILE) + N_EXPERTS
    pos, tile_e, n_valid, fill = _route_tables(route, counts, MOE_ROW_TILE, n_tiles)
    h_sorted = moe_dispatch(x2, norm_g, pos, n_valid, fill, n_tiles, MOE_ROW_TILE)
    y_sorted = moe_ffn(h_sorted, tile_e, n_valid, w_gate, w_up, w_down, layer, MOE_ROW_TILE)
    out = moe_combine(x2, route, y_sorted, pos, final_g, 256)
    return out.reshape(b, s, d)


def kernel(x, norm_mix, norm_ffn, norm_final, ev_w_in, ev_gla_w_alpha, ev_gla_b_alpha, ev_gla_norm, ev_rwkv_mu, ev_rwkv_w0, ev_rwkv_w2, ev_rwkv_a0, ev_rwkv_a2, ev_rwkv_g2, ev_rwkv_k_k, ev_rwkv_k_a, ev_rwkv_r_k, ev_rwkv_ln_w, ev_rwkv_ln_b, ev_w_out, od_w_in, od_conv, od_a_log, od_dt_bias, od_norm, od_w_out, moe_w_group, moe_b_group, moe_w_router, moe_b_router, moe_w_gate, moe_w_up, moe_w_down):
    depth = norm_mix.shape[0]
    for i in range(depth):
        j = i // 2
        if i % 2 == 0:
            x = _even_layer(x, norm_mix[i], ev_w_in[j], ev_gla_w_alpha[j], ev_gla_b_alpha[j], ev_gla_norm[j],
                            ev_rwkv_mu[j], ev_rwkv_w0[j], ev_rwkv_w2[j], ev_rwkv_a0[j], ev_rwkv_a2[j],
                            ev_rwkv_g2[j], ev_rwkv_k_k[j], ev_rwkv_k_a[j], ev_rwkv_r_k[j],
                            ev_rwkv_ln_w[j], ev_rwkv_ln_b[j], ev_w_out[j])
        else:
            x = _odd_layer(x, norm_mix[i], od_w_in[j], od_conv[j], od_a_log[j], od_dt_bias[j],
                           od_norm[j], od_w_out[j])
        x = _moe_layer(x, norm_ffn[i], moe_w_group[i], moe_b_group[i], moe_w_router[i], moe_b_router[i],
                       moe_w_gate, moe_w_up, moe_w_down, i, norm_final if i == depth - 1 else None)
    return x
```

```python
import functools
import math

import jax
import jax.numpy as jnp
from jax import lax
from jax.experimental import pallas as pl
from jax.experimental.pallas import tpu as pltpu

F32 = jnp.float32
BF16 = jnp.bfloat16
HI = lax.Precision.HIGHEST

EPS = 1e-6
CHUNK = 64
LANES = 128
VMEM_LIMIT = 56 * 1024 * 1024

D_MODEL = 2048
GLA_HEADS = 4
GLA_DK = 128
GLA_DV = 256
GLA_KW = 512
GLA_WIDTH = 1024
GLA_LOWRANK = 16
GLA_TAU = 16.0
RWKV_HEAD = 64
RWKV_WIDTH = 1024
RWKV_HEADS = 16
DECAY_LORA = 96
ICLR_LORA = 96
GATE_LORA = 256
RWKV_LN_EPS = 64e-5
GDN_DK = 128
GDN_DV = 128
GDN_KHEADS = 16
GDN_VHEADS = 32
GDN_KW = 2048
GDN_VW = 4096
GDN_QKV = 8192
SCAN_CHUNKS_PER_STEP = 4
RWKV_PAIRS_PER_STEP = 4
GDN_HEADS_PER_STEP = 4
MOE_ROW_TILE = 256
N_GROUPS = 4
EXPERTS_PER_GROUP = 8
N_EXPERTS = 32
EXPERT_FF = 512

GQ_OFF, GK_OFF, GV_OFF, GG_OFF = 0, 512, 1024, 2048
R_OFF, K_OFF, V_OFF = 3072, 4096, 5120
GLR_OFF, WLF_OFF, WLB_OFF, ALR_OFF, AL_OFF = 6144, 6400, 6528, 6656, 6784
RWKV_MAIN = 3072
RWKV_SMALL = 768
EVEN_PAD = 6912


def _cparams(sem):
    return pltpu.CompilerParams(dimension_semantics=sem, vmem_limit_bytes=VMEM_LIMIT)


def _mm(a, b, ca, cb, precision):
    if a.ndim == 3:
        dims = (((ca + 1,), (cb + 1,)), ((0,), (0,)))
    else:
        dims = (((ca,), (cb,)), ((), ()))
    return lax.dot_general(a, b, dims, preferred_element_type=F32, precision=precision)


def _dot(a, b, precision=None):
    return _mm(a, b, 1, 0, precision)


def _dot_nt(a, b, precision=None):
    return _mm(a, b, 1, 1, precision)


def _dot_tn(a, b, precision=None):
    return _mm(a, b, 0, 0, precision)


def _bf16_pieces(x, n):
    pieces = []
    for _ in range(n - 1):
        p = x.astype(BF16)
        pieces.append(p)
        x = x - p.astype(F32)
    pieces.append(x.astype(BF16))
    return pieces


def _mask_dot(mask, x, pieces=3):
    mb = mask.astype(BF16)
    return sum(_dot(mb, p) for p in _bf16_pieces(x, pieces))


def _dot_mask(x, mask, pieces=2):
    mb = mask.astype(BF16)
    return sum(_dot(p, mb) for p in _bf16_pieces(x, pieces))


def _dot3(a, b):
    ah, al = _bf16_pieces(a, 2)
    bh, bl = _bf16_pieces(b, 2)
    return _dot(ah, bh) + _dot(ah, bl) + _dot(al, bh)


def _bdot(a, b):
    return _dot(a.astype(BF16), b.astype(BF16))


def _bdot_nt(a, b):
    return _dot_nt(a.astype(BF16), b.astype(BF16))


def _bdot_tn(a, b):
    return _dot_tn(a.astype(BF16), b.astype(BF16))


def _sigmoid(x):
    return 1.0 / (1.0 + jnp.exp(-x))


def _silu(x):
    return x * _sigmoid(x)


def _softplus(x):
    return jnp.maximum(x, 0.0) + jnp.log(1.0 + jnp.exp(-jnp.abs(x)))


def _iota2(shape, dim):
    return lax.broadcasted_iota(jnp.int32, shape, dim)


def _order_mask(n, rev, strict, block):
    i = _iota2((n, n), 0)
    j = _iota2((n, n), 1)
    if rev:
        m = (j > i) if strict else (j >= i)
    else:
        m = (j < i) if strict else (j <= i)
    if block < n:
        sh = block.bit_length() - 1
        m = m & (lax.shift_right_logical(i, sh) == lax.shift_right_logical(j, sh))
    return m


def _neumann_inverse(nmat):
    n = nmat.shape[-1]
    eye = (_iota2((n, n), 0) == _iota2((n, n), 1)).astype(F32)
    t = eye + nmat
    nb = nmat.astype(BF16)
    p = _dot(nb, nb)
    for step in range(5):
        pb = p.astype(BF16)
        if step < 4:
            both = _dot(pb, jnp.concatenate([pb, t.astype(BF16)], axis=-1))
            p = both[..., :n]
            t = t + both[..., n:]
        else:
            t = t + _dot(pb, t.astype(BF16))
    return t


def _norm_matmul_kernel(x_ref, g_ref, w_ref, o_ref, h_ref):
    @pl.when(pl.program_id(1) == 0)
    def _():
        x = x_ref[...]
        ms = jnp.mean(x * x, axis=-1, keepdims=True)
        h_ref[...] = (x * lax.rsqrt(ms + EPS) * g_ref[...]).astype(BF16)

    o_ref[...] = _dot(h_ref[...], w_ref[...]).astype(o_ref.dtype)


def norm_matmul(x, g, w, tm, tn, out_dtype=F32):
    m, d = x.shape
    tm = min(tm, m)
    n = w.shape[1]
    return pl.pallas_call(
        _norm_matmul_kernel,
        grid=(m // tm, n // tn),
        in_specs=[pl.BlockSpec((tm, d), lambda i, j: (i, 0)),
                  pl.BlockSpec((1, d), lambda i, j: (0, 0)),
                  pl.BlockSpec((d, tn), lambda i, j: (0, j))],
        out_specs=pl.BlockSpec((tm, tn), lambda i, j: (i, j)),
        out_shape=jax.ShapeDtypeStruct((m, n), out_dtype),
        scratch_shapes=[pltpu.VMEM((tm, d), BF16)],
        compiler_params=_cparams(("parallel", "arbitrary")),
        name="norm_matmul",
    )(x, g.reshape(1, d), w)


def _proj_residual_kernel(*refs, n_lhs):
    x_ref = refs[2 * n_lhs]
    o_ref = refs[2 * n_lhs + 1]
    acc = x_ref[...]
    for t in range(n_lhs):
        acc = acc + _dot(refs[t][...], refs[n_lhs + t][...])
    o_ref[...] = acc


def proj_residual(ys, ws, x, tm, tn):
    m, n = x.shape
    tm = min(tm, m)
    n_lhs = len(ys)
    in_specs = [pl.BlockSpec((tm, y.shape[1]), lambda i, j: (i, 0)) for y in ys]
    in_specs += [pl.BlockSpec((w.shape[0], tn), lambda i, j: (0, j)) for w in ws]
    in_specs += [pl.BlockSpec((tm, tn), lambda i, j: (i, j))]
    return pl.pallas_call(
        functools.partial(_proj_residual_kernel, n_lhs=n_lhs),
        grid=(m // tm, n // tn),
        in_specs=in_specs,
        out_specs=pl.BlockSpec((tm, tn), lambda i, j: (i, j)),
        out_shape=jax.ShapeDtypeStruct((m, n), F32),
        compiler_params=_cparams(("parallel", "arbitrary")),
        name="proj_residual",
    )(*ys, *ws, x)


def _final_norm_kernel(x_ref, g_ref, o_ref):
    x = x_ref[...]
    ms = jnp.mean(x * x, axis=-1, keepdims=True)
    o_ref[...] = x * lax.rsqrt(ms + EPS) * g_ref[...]


def final_norm(x, g, tm):
    m, d = x.shape
    tm = min(tm, m)
    return pl.pallas_call(
        _final_norm_kernel,
        grid=(m // tm,),
        in_specs=[pl.BlockSpec((tm, d), lambda i: (i, 0)), pl.BlockSpec((1, d), lambda i: (0, 0))],
        out_specs=pl.BlockSpec((tm, d), lambda i: (i, 0)),
        out_shape=jax.ShapeDtypeStruct((m, d), F32),
        compiler_params=_cparams(("parallel",)),
        name="final_norm",
    )(x, g.reshape(1, d))


def _gla_chunks(q, k, v, cum, revs, s_ref):
    c = q.shape[1]
    tot = jnp.stack([cum[j, (0 if rev else c - 1)][None] for j, rev in enumerate(revs)], axis=0)
    incl = jnp.stack([_order_mask(c, rev, False, c) for rev in revs], axis=0)
    q_dec = q * ((GLA_DK ** -0.5) * jnp.exp(cum))
    k_dec = k * jnp.exp(-cum)
    k_end = k * jnp.exp(tot - cum)
    scores = jnp.where(incl, _bdot_nt(q_dec, k_dec), 0.0)
    state = s_ref[...]
    o = _bdot(scores, v) + _bdot_nt(q_dec, state)
    s_ref[...] = jnp.exp(tot) * state + _bdot_tn(v, k_end)
    return o


def _gla_kernel(qf, kf, vf, alf, qb, kb, vb, alb, wa_ref, ba_ref, of_ref, ob_ref, s_ref):
    @pl.when(pl.program_id(1) == 0)
    def _():
        s_ref[...] = jnp.zeros_like(s_ref)

    c = qf.shape[1]
    qs, ks, vs, cums, revs = [], [], [], [], []
    for d, (q_ref, k_ref, v_ref, al_ref) in enumerate(((qf, kf, vf, alf), (qb, kb, vb, alb))):
        pre = _dot3(al_ref[0], wa_ref[d]) + ba_ref[d:d + 1]
        log_a = (jnp.minimum(pre, 0.0) - jnp.log(1.0 + jnp.exp(-jnp.abs(pre)))) * (1.0 / GLA_TAU)
        cum = _mask_dot(_order_mask(c, d == 1, False, c), log_a)
        for h in range(GLA_HEADS):
            ksl = slice(h * GLA_DK, (h + 1) * GLA_DK)
            qs.append(q_ref[0, :, ksl])
            ks.append(k_ref[0, :, ksl])
            vs.append(v_ref[0, :, h * GLA_DV:(h + 1) * GLA_DV])
            cums.append(cum[:, ksl])
            revs.append(d == 1)
    o = _gla_chunks(jnp.stack(qs), jnp.stack(ks), jnp.stack(vs), jnp.stack(cums), revs, s_ref)
    for d, o_ref in enumerate((of_ref, ob_ref)):
        for h in range(GLA_HEADS):
            o_ref[0, :, h * GLA_DV:(h + 1) * GLA_DV] = o[d * GLA_HEADS + h].astype(o_ref.dtype)


def gla_scan(z, wa, ba):
    b, s, _ = z.shape
    n = s // CHUNK
    c = CHUNK

    def fwd(off, w):
        return pl.BlockSpec((1, c, w), lambda bi, t: (bi, t, off // w))

    def bwd(off, w):
        return pl.BlockSpec((1, c, w), lambda bi, t: (bi, n - 1 - t, off // w))

    in_specs = [fwd(GQ_OFF, GLA_KW), fwd(GK_OFF, GLA_KW), fwd(GV_OFF, GLA_WIDTH), fwd(AL_OFF, LANES),
                bwd(GQ_OFF, GLA_KW), bwd(GK_OFF, GLA_KW), bwd(GV_OFF, GLA_WIDTH), bwd(AL_OFF, LANES),
                pl.BlockSpec(wa.shape, lambda bi, t: (0, 0, 0)), pl.BlockSpec(ba.shape, lambda bi, t: (0, 0))]
    out_specs = [pl.BlockSpec((1, c, GLA_WIDTH), lambda bi, t: (bi, t, 0)),
                 pl.BlockSpec((1, c, GLA_WIDTH), lambda bi, t: (bi, n - 1 - t, 0))]
    return pl.pallas_call(
        _gla_kernel,
        grid=(b, n),
        in_specs=in_specs,
        out_specs=out_specs,
        out_shape=[jax.ShapeDtypeStruct((b, s, GLA_WIDTH), BF16)] * 2,
        scratch_shapes=[pltpu.VMEM((2 * GLA_HEADS, GLA_DV, GLA_DK), F32)],
        compiler_params=_cparams(("parallel", "arbitrary")),
        name="gla_scan",
    )(z, z, z, z, z, z, z, z, wa, ba)


def _gla_post_kernel(of_ref, ob_ref, gate_ref, nw_ref, y_ref):
    o = of_ref[0].astype(F32) + ob_ref[0].astype(F32)
    ms = jnp.mean(o * o, axis=-1, keepdims=True)
    y = o * lax.rsqrt(ms + EPS) * nw_ref[...]
    y_ref[0] = (y * _silu(gate_ref[0])).astype(y_ref.dtype)


def gla_post(o_f, o_b, z, norm_w, tm):
    b, s, w = o_f.shape
    blk = pl.BlockSpec((1, tm, GLA_DV), lambda bi, t, h: (bi, t, h))
    return pl.pallas_call(
        _gla_post_kernel,
        grid=(b, s // tm, GLA_HEADS),
        in_specs=[blk, blk, pl.BlockSpec((1, tm, GLA_DV), lambda bi, t, h: (bi, t, GG_OFF // GLA_DV + h)),
                  pl.BlockSpec((1, GLA_DV), lambda bi, t, h: (0, 0))],
        out_specs=blk,
        out_shape=jax.ShapeDtypeStruct((b, s, w), BF16),
        compiler_params=_cparams(("parallel", "parallel", "parallel")),
        name="gla_post",
    )(o_f, o_b, z, norm_w.reshape(1, GLA_DV))


def _seg_sum(x, e, et):
    return _dot_mask(_dot_mask(x, e), et)


def _rwkv_prep_kernel(z_ref, zp_ref, zn_ref, y_ref, yp_ref, yn_ref, mu_ref, w0_ref, w2f_ref, w2b_ref, a0_ref, a2_ref, g2_ref,
                      kk_ref, ka_ref, rk_ref, e_ref, et_ref,
                      r_out, k_out, v_out, kk_out, b_out, cumf_out, excf_out, cumb_out, excb_out, g_out, bonus_out):
    t = pl.program_id(1)
    tm = z_ref.shape[1]
    row = _iota2((tm, 1), 0)
    first = t == 0
    last = t == pl.num_programs(1) - 1

    def shifted(lo, hi):
        cur, prv, nxt, base = (z_ref, zp_ref, zn_ref, R_OFF) if lo < GLR_OFF else (y_ref, yp_ref, yn_ref, GLR_OFF)
        z = cur[0, :, lo - base:hi - base]
        prev_row = jnp.where(first, 0.0, prv[0, 7:8, lo - base:hi - base])
        next_row = jnp.where(last, 0.0, nxt[0, 0:1, lo - base:hi - base])
        zprev = jnp.where(row == 0, prev_row, pltpu.roll(z, 1, axis=0))
        znext = jnp.where(row == tm - 1, next_row, pltpu.roll(z, tm - 1, axis=0))
        mu = mu_ref[:, lo - R_OFF:hi - R_OFF]
        return z + mu[0:1] * (zprev - z) + mu[1:2] * (znext - z)

    e = e_ref[...]
    et = et_ref[...]
    a = _sigmoid(a0_ref[...] + _bdot(shifted(ALR_OFF, ALR_OFF + LANES), a2_ref[...]))
    k = shifted(K_OFF, K_OFF + RWKV_WIDTH)
    kk_raw = k * kk_ref[...]
    kk = kk_raw * lax.rsqrt(_seg_sum(kk_raw * kk_raw, e, et) + EPS)
    kk_out[0] = kk.astype(kk_out.dtype)
    b_out[0] = (kk * a).astype(b_out.dtype)
    kmod = k * (1.0 + (a - 1.0) * ka_ref[...])
    k_out[0] = kmod.astype(k_out.dtype)
    r = shifted(R_OFF, R_OFF + RWKV_WIDTH)
    r_out[0] = r.astype(r_out.dtype)
    v = shifted(V_OFF, V_OFF + RWKV_WIDTH)
    v_out[0] = v.astype(v_out.dtype)
    bonus_out[0] = (_seg_sum(r * kmod * rk_ref[...], e, et) * v).astype(bonus_out.dtype)
    g_out[0] = _bdot(_sigmoid(shifted(GLR_OFF, GLR_OFF + GATE_LORA)), g2_ref[...]).astype(g_out.dtype)
    decay_scale = -math.exp(-0.5)
    wf = w0_ref[0:1] + _dot3(jnp.tanh(shifted(WLF_OFF, WLF_OFF + LANES)), w2f_ref[...])
    lwf = decay_scale * _sigmoid(wf)
    cum_f = _mask_dot(_order_mask(tm, False, False, CHUNK), lwf)
    cumf_out[0] = cum_f
    excf_out[0] = cum_f - lwf
    wb = w0_ref[1:2] + _dot3(jnp.tanh(shifted(WLB_OFF, WLB_OFF + LANES)), w2b_ref[...])
    lwb = decay_scale * _sigmoid(wb)
    cum_b = _mask_dot(_order_mask(tm, True, False, CHUNK), lwb)
    cumb_out[0] = cum_b
    excb_out[0] = cum_b - lwb


def rwkv_prep(z, mu, w0, w2f, w2b, a0, a2, g2, k_k, k_a, r_k, seg_e, seg_et, tm):
    b, s, _ = z.shape
    nt = s // tm
    hb = tm // 8
    full = lambda arr: pl.BlockSpec(arr.shape, lambda bi, t: (0,) * arr.ndim)
    in_specs = []
    for width, off in ((RWKV_MAIN, R_OFF), (RWKV_SMALL, GLR_OFF)):
        cb = off // width
        in_specs += [pl.BlockSpec((1, tm, width), lambda bi, t, cb=cb: (bi, t, cb)),
                     pl.BlockSpec((1, 8, width), lambda bi, t, cb=cb: (bi, jnp.maximum(t * hb - 1, 0), cb)),
                     pl.BlockSpec((1, 8, width), lambda bi, t, cb=cb: (bi, jnp.minimum((t + 1) * hb, nt * hb - 1), cb))]
    params = [mu, w0, w2f, w2b, a0, a2, g2, k_k, k_a, r_k, seg_e, seg_et]
    in_specs += [full(p) for p in params]
    out_blk = pl.BlockSpec((1, tm, RWKV_WIDTH), lambda bi, t: (bi, t, 0))
    return pl.pallas_call(
        _rwkv_prep_kernel,
        grid=(b, nt),
        in_specs=in_specs,
        out_specs=[out_blk] * 11,
        out_shape=[jax.ShapeDtypeStruct((b, s, RWKV_WIDTH), dt) for dt in [BF16] * 5 + [F32] * 4 + [BF16] * 2],
        compiler_params=_cparams(("parallel", "parallel")),
        name="rwkv_prep",
    )(z, z, z, z, z, z, *params)


def _stack_heads(x):
    lane = _iota2(x.shape, 2)
    return jnp.concatenate([jnp.where(lane < RWKV_HEAD, x, 0.0), jnp.where(lane < RWKV_HEAD, 0.0, x)], axis=1)


def _rwkv_chunks(r, k, v, kk, bb, cum, exc, revs, g_ref):
    nb, c, _ = r.shape
    n2 = 2 * c
    tot = jnp.stack([cum[j, (0 if rev else c - 1)][None] for j, rev in enumerate(revs)], axis=0)
    e_neg = jnp.exp(-cum)
    e_end = jnp.exp(tot - cum)
    al2 = _stack_heads(-kk * jnp.exp(exc))
    rb2 = _stack_heads(r * jnp.exp(cum))
    bt2 = _stack_heads(bb * e_neg)
    kt2 = _stack_heads(k * e_neg)
    be2 = _stack_heads(bb * e_end)
    ke2 = _stack_heads(k * e_end)
    v2 = _stack_heads(v)
    gram = _bdot_nt(jnp.concatenate([al2, rb2], axis=1), jnp.concatenate([bt2, kt2], axis=1))
    strict = jnp.stack([_order_mask(n2, rev, True, c) for rev in revs], axis=0)
    incl = jnp.stack([_order_mask(n2, rev, False, c) for rev in revs], axis=0)
    a_ab = jnp.where(strict, gram[:, :n2, :n2], 0.0)
    a_ak = jnp.where(strict, gram[:, :n2, n2:], 0.0)
    a_rb = jnp.where(incl, gram[:, n2:, :n2], 0.0)
    a_rk = jnp.where(incl, gram[:, n2:, n2:], 0.0)
    t_inv = _neumann_inverse(a_ab)
    wu = _bdot(t_inv, jnp.concatenate([al2, _bdot(a_ak, v2)], axis=2))
    g = g_ref[...]
    proj = _bdot_nt(jnp.concatenate([wu[:, :, :LANES], rb2], axis=1), g)
    u2 = proj[:, :n2] + wu[:, :, LANES:]
    uv = jnp.concatenate([u2, v2], axis=1)
    o2 = proj[:, n2:] + _bdot(jnp.concatenate([a_rb, a_rk], axis=2), uv)
    g_ref[...] = g * jnp.exp(tot) + _bdot_tn(uv, jnp.concatenate([be2, ke2], axis=1))
    return o2[:, :c] + o2[:, c:]


def _rwkv_scan_kernel(*refs, pairs):
    fwd_refs, bwd_refs = refs[0:7], refs[7:14]
    of_ref, ob_ref, g_ref = refs[14:17]

    @pl.when(pl.program_id(2) == 0)
    def _():
        g_ref[...] = jnp.zeros_like(g_ref)

    revs = [False] * pairs + [True] * pairs
    c = CHUNK
    sub_steps = of_ref.shape[1] // c
    for sub in range(sub_steps):
        rows_of = (slice(sub * c, (sub + 1) * c), slice((sub_steps - 1 - sub) * c, (sub_steps - sub) * c))
        operands = []
        for t in range(7):
            operands.append(jnp.stack([ref[0, rows_of[d], p * LANES:(p + 1) * LANES].astype(F32)
                                       for d, ref in enumerate((fwd_refs[t], bwd_refs[t])) for p in range(pairs)],
                                      axis=0))
        o = _rwkv_chunks(*operands, revs, g_ref)
        for d, o_ref in enumerate((of_ref, ob_ref)):
            for p in range(pairs):
                o_ref[0, rows_of[d], p * LANES:(p + 1) * LANES] = o[d * pairs + p].astype(o_ref.dtype)


def rwkv_scan(r, k, v, kk, bb, cum_f, exc_f, cum_b, exc_b, pairs):
    b, s, _ = r.shape
    c = CHUNK * SCAN_CHUNKS_PER_STEP
    n = s // c
    w = pairs * LANES
    fwd = pl.BlockSpec((1, c, w), lambda bi, p, t: (bi, t, p))
    bwd = pl.BlockSpec((1, c, w), lambda bi, p, t: (bi, n - 1 - t, p))
    return pl.pallas_call(
        functools.partial(_rwkv_scan_kernel, pairs=pairs),
        grid=(b, RWKV_WIDTH // w, n),
        in_specs=[fwd] * 7 + [bwd] * 7,
        out_specs=[fwd, bwd],
        out_shape=[jax.ShapeDtypeStruct((b, s, RWKV_WIDTH), BF16)] * 2,
        scratch_shapes=[pltpu.VMEM((2 * pairs, LANES, LANES), F32)],
        compiler_params=_cparams(("parallel", "parallel", "arbitrary")),
        name="rwkv_scan",
    )(r, k, v, kk, bb, cum_f, exc_f, r, k, v, kk, bb, cum_b, exc_b)


def _rwkv_post_kernel(of_ref, ob_ref, bonus_ref, g_ref, lnw_ref, lnb_ref, e_ref, et_ref, y_ref):
    e = e_ref[...]
    et = et_ref[...]
    o = of_ref[0].astype(F32) + ob_ref[0].astype(F32)
    mean = _seg_sum(o, e, et) * (1.0 / RWKV_HEAD)
    cen = o - mean
    var = _seg_sum(cen * cen, e, et) * (1.0 / RWKV_HEAD)
    y = cen * lax.rsqrt(var + RWKV_LN_EPS) * lnw_ref[...] + lnb_ref[...]
    y_ref[0] = ((y + bonus_ref[0].astype(F32)) * g_ref[0].astype(F32)).astype(y_ref.dtype)


def rwkv_post(o_f, o_b, bonus, g, ln_w, ln_b, seg_e, seg_et, tm):
    b, s, w = o_f.shape
    blk = pl.BlockSpec((1, tm, w), lambda bi, t: (bi, t, 0))
    full = lambda arr: pl.BlockSpec(arr.shape, lambda bi, t: (0,) * arr.ndim)
    params = [ln_w.reshape(1, w), ln_b.reshape(1, w), seg_e, seg_et]
    return pl.pallas_call(
        _rwkv_post_kernel,
        grid=(b, s // tm),
        in_specs=[blk] * 4 + [full(p) for p in params],
        out_specs=blk,
        out_shape=jax.ShapeDtypeStruct((b, s, w), BF16),
        compiler_params=_cparams(("parallel", "parallel")),
        name="rwkv_post",
    )(o_f, o_b, bonus, g, *params)


def _gdn_conv_kernel(z_ref, zp_ref, zn_ref, cw_ref, sc_ref, o_ref, *, l2norm):
    t = pl.program_id(1)
    tm = z_ref.shape[1]
    halo = zp_ref.shape[1]
    ext = jnp.concatenate([jnp.where(t == 0, 0.0, zp_ref[0].astype(F32)), z_ref[0].astype(F32),
                           jnp.where(t == pl.num_programs(1) - 1, 0.0, zn_ref[0].astype(F32))], axis=0)
    n = tm + 2 * halo
    pad = cw_ref.shape[0] // 2
    y = 0.0
    for j in range(cw_ref.shape[0]):
        shifted = ext if j == pad else pltpu.roll(ext, (pad - j) % n, axis=0)
        y = y + cw_ref[j:j + 1] * shifted[halo:halo + tm]
    y = _silu(y)
    if l2norm:
        for h in range(y.shape[1] // GDN_DK):
            sl = slice(h * GDN_DK, (h + 1) * GDN_DK)
            yh = y[:, sl]
            ss = jnp.sum(yh * yh, axis=-1, keepdims=True)
            o_ref[0, :, sl] = (yh * lax.rsqrt(ss + EPS) * sc_ref[0:1, sl]).astype(o_ref.dtype)
    else:
        o_ref[0] = y.astype(o_ref.dtype)


def gdn_conv(z, conv_w, scale, col_off, width, l2norm, tm, cw):
    b, s, _ = z.shape
    nt = s // tm
    halo = 8 * (4 // z.dtype.itemsize)
    hb = tm // halo
    cb = col_off // cw
    in_specs = [pl.BlockSpec((1, tm, cw), lambda bi, t, c: (bi, t, cb + c)),
                pl.BlockSpec((1, halo, cw), lambda bi, t, c: (bi, jnp.maximum(t * hb - 1, 0), cb + c)),
                pl.BlockSpec((1, halo, cw), lambda bi, t, c: (bi, jnp.minimum((t + 1) * hb, nt * hb - 1), cb + c)),
                pl.BlockSpec((conv_w.shape[0], cw), lambda bi, t, c: (0, cb + c)),
                pl.BlockSpec((1, cw), lambda bi, t, c: (0, c))]
    return pl.pallas_call(
        functools.partial(_gdn_conv_kernel, l2norm=l2norm),
        grid=(b, nt, width // cw),
        in_specs=in_specs,
        out_specs=pl.BlockSpec((1, tm, cw), lambda bi, t, c: (bi, t, c)),
        out_shape=jax.ShapeDtypeStruct((b, s, width), BF16),
        compiler_params=_cparams(("parallel", "parallel", "parallel")),
        name="gdn_conv_norm" if l2norm else "gdn_conv",
    )(z, z, z, conv_w, scale)


def _gdn_gates_kernel(zs_ref, na_ref, dtb_ref, o_ref):
    zs = zs_ref[0]
    tm = zs.shape[0]
    gg = na_ref[...] * _softplus(zs + dtb_ref[...])
    gam_f = _mask_dot(_order_mask(tm, False, False, CHUNK), gg)
    gam_b = _mask_dot(_order_mask(tm, True, False, CHUNK), gg)
    lane = _iota2(zs.shape, 1)
    o_ref[0] = jnp.where(lane < 2 * GDN_VHEADS, _sigmoid(zs), jnp.where(lane < 3 * GDN_VHEADS, gam_f, gam_b))


def gdn_gates(zs, neg_a, dtb, tm):
    b, s, w = zs.shape
    blk = pl.BlockSpec((1, tm, w), lambda bi, t: (bi, t, 0))
    row = pl.BlockSpec((1, w), lambda bi, t: (0, 0))
    return pl.pallas_call(
        _gdn_gates_kernel,
        grid=(b, s // tm),
        in_specs=[blk, row, row],
        out_specs=blk,
        out_shape=jax.ShapeDtypeStruct((b, s, w), F32),
        compiler_params=_cparams(("parallel", "parallel")),
        name="gdn_gates",
    )(zs, neg_a, dtb)


def _gdn_chunks(q, k, v2, beta2, gam2, revs, s_ref):
    nb, c, _ = q.shape
    n2 = 2 * c
    per_batch = lambda fn: jnp.stack([fn(r) for r in revs], axis=0)
    incl = per_batch(lambda r: _order_mask(n2, r, False, c))
    strict = per_batch(lambda r: _order_mask(n2, r, True, c))
    gam_c = jnp.broadcast_to(gam2, (nb, n2, n2))
    gam_r = jnp.swapaxes(gam_c, 1, 2)
    tot2 = jnp.stack([jnp.concatenate(
        [jnp.broadcast_to(gam2[j, h * c + (0 if r else c - 1)][None], (c, 1)) for h in range(2)], axis=0)
        for j, r in enumerate(revs)], axis=0)
    diff = gam_c - gam_r
    decay = jnp.exp(jnp.where(incl, diff, 0.0))
    dec_s = jnp.where(strict, decay, 0.0)
    dec_i = jnp.where(incl, decay, 0.0)
    k2 = jnp.concatenate([k, k], axis=1)
    q2 = jnp.concatenate([q, q], axis=1)
    gram = _bdot_nt(jnp.concatenate([k2, q2], axis=1), k2)
    a_mat = gram[:, :n2] * beta2 * dec_s
    qk = gram[:, n2:] * dec_i
    t_inv = _neumann_inverse(-a_mat)
    e_gam = jnp.exp(gam2)
    uw = _bdot(t_inv, jnp.concatenate([v2 * beta2, k2 * (beta2 * e_gam)], axis=2))
    u2 = uw[:, :, :GDN_DV]
    w2 = uw[:, :, GDN_DV:]
    qd2 = q2 * e_gam
    ke2 = k2 * jnp.exp(tot2 - gam2)
    dl2 = jnp.exp(tot2)
    ws, qs = [], []
    for e in range(2):
        rows = slice(e * c, (e + 1) * c)
        both = _bdot(jnp.concatenate([w2[:, rows], qd2[:, rows]], axis=1), s_ref[e])
        ws.append(both[:, :c])
        qs.append(both[:, c:])
    vnew2 = u2 - jnp.concatenate(ws, axis=1)
    o2 = jnp.concatenate(qs, axis=1) + _bdot(qk, vnew2)
    for e in range(2):
        rows = slice(e * c, (e + 1) * c)
        s_ref[e] = s_ref[e] * dl2[:, e * c:e * c + 1, :] + _bdot_tn(ke2[:, rows], vnew2[:, rows])
    return o2


def _gdn_scan_kernel(qf, kf, vf, gf, qb, kb, vb, gb, of_ref, ob_ref, s_ref, *, heads):
    @pl.when(pl.program_id(2) == 0)
    def _():
        s_ref[...] = jnp.zeros_like(s_ref)

    c = CHUNK
    sub_steps = qf.shape[1] // c
    lane = _iota2((c, LANES), 1)

    def column(gates, idx):
        return jnp.sum(jnp.where(lane == idx, gates, 0.0), axis=1, keepdims=True)

    for sub in range(sub_steps):
        rows_of = (slice(sub * c, (sub + 1) * c), slice((sub_steps - 1 - sub) * c, (sub_steps - sub) * c))
        qs, ks, vs, betas, gams, revs = [], [], [], [], [], []
        for d, (q_ref, k_ref, v_ref, g_ref) in enumerate(((qf, kf, vf, gf), (qb, kb, vb, gb))):
            rows = rows_of[d]
            gates = g_ref[0, rows, :]
            for i in range(heads):
                vh = 2 * (pl.program_id(1) * heads + i)
                ksl = slice(i * GDN_DK, (i + 1) * GDN_DK)
                qs.append(q_ref[0, rows, ksl].astype(F32))
                ks.append(k_ref[0, rows, ksl].astype(F32))
                vs.append(jnp.concatenate(
                    [v_ref[0, rows, (2 * i + e) * GDN_DV:(2 * i + e + 1) * GDN_DV].astype(F32) for e in range(2)], axis=0))
                betas.append(jnp.concatenate([column(gates, d * GDN_VHEADS + vh + e) for e in range(2)], axis=0))
                gams.append(jnp.concatenate([column(gates, (2 + d) * GDN_VHEADS + vh + e) for e in range(2)], axis=0))
                revs.append(d == 1)
        o2 = _gdn_chunks(jnp.stack(qs), jnp.stack(ks), jnp.stack(vs), jnp.stack(betas), jnp.stack(gams), revs, s_ref)
        for d, o_ref in enumerate((of_ref, ob_ref)):
            for i in range(heads):
                j = d * heads + i
                for e in range(2):
                    o_ref[0, rows_of[d], (2 * i + e) * GDN_DV:(2 * i + e + 1) * GDN_DV] = (
                        o2[j, e * c:(e + 1) * c].astype(o_ref.dtype))


def gdn_scan(qk, v, gates, heads):
    b, s, _ = v.shape
    c = CHUNK * SCAN_CHUNKS_PER_STEP
    n = s // c
    ng = GDN_KHEADS // heads

    def spec(w, off, rev):
        if rev:
            return pl.BlockSpec((1, c, w), lambda bi, h, t: (bi, n - 1 - t, off + h))
        return pl.BlockSpec((1, c, w), lambda bi, h, t: (bi, t, off + h))

    def gspec(rev):
        if rev:
            return pl.BlockSpec((1, c, LANES), lambda bi, h, t: (bi, n - 1 - t, 0))
        return pl.BlockSpec((1, c, LANES), lambda bi, h, t: (bi, t, 0))

    kw = heads * GDN_DK
    vw = heads * 2 * GDN_DV
    in_specs = [spec(kw, 0, False), spec(kw, ng, False), spec(vw, 0, False), gspec(False),
                spec(kw, 0, True), spec(kw, ng, True), spec(vw, 0, True), gspec(True)]
    return pl.pallas_call(
        functools.partial(_gdn_scan_kernel, heads=heads),
        grid=(b, ng, n),
        in_specs=in_specs,
        out_specs=[spec(vw, 0, False), spec(vw, 0, True)],
        out_shape=[jax.ShapeDtypeStruct((b, s, GDN_VW), BF16)] * 2,
        scratch_shapes=[pltpu.VMEM((2, 2 * heads, GDN_DK, GDN_DV), F32)],
        compiler_params=_cparams(("parallel", "parallel", "arbitrary")),
        name="gdn_scan",
    )(qk, qk, v, gates, qk, qk, v, gates)


def _gdn_post_kernel(of_ref, ob_ref, gate_ref, nw_ref, y_ref):
    nw = nw_ref[...]
    for h in range(of_ref.shape[2] // GDN_DV):
        sl = slice(h * GDN_DV, (h + 1) * GDN_DV)
        o = of_ref[0, :, sl].astype(F32) + ob_ref[0, :, sl].astype(F32)
        ms = jnp.mean(o * o, axis=-1, keepdims=True)
        y = o * lax.rsqrt(ms + EPS) * nw
        y_ref[0, :, sl] = (y * _silu(gate_ref[0, :, sl].astype(F32))).astype(y_ref.dtype)


def gdn_post(o_f, o_b, z, gate_off, norm_w, tm, cw):
    b, s, w = o_f.shape
    blk = pl.BlockSpec((1, tm, cw), lambda bi, t, c: (bi, t, c))
    gb = gate_off // cw
    return pl.pallas_call(
        _gdn_post_kernel,
        grid=(b, s // tm, w // cw),
        in_specs=[blk, blk, pl.BlockSpec((1, tm, cw), lambda bi, t, c: (bi, t, gb + c)),
                  pl.BlockSpec((1, GDN_DV), lambda bi, t, c: (0, 0))],
        out_specs=blk,
        out_shape=jax.ShapeDtypeStruct((b, s, w), BF16),
        compiler_params=_cparams(("parallel", "parallel", "parallel")),
        name="gdn_post",
    )(o_f, o_b, z, norm_w.reshape(1, GDN_DV))


def _moe_router_kernel(x_ref, g_ref, w_ref, b_ref, route_ref, counts_ref):
    @pl.when(pl.program_id(0) == 0)
    def _():
        counts_ref[...] = jnp.zeros_like(counts_ref)

    x = x_ref[...]
    tm = x.shape[0]
    ms = jnp.mean(x * x, axis=-1, keepdims=True)
    h = x * lax.rsqrt(ms + EPS) * g_ref[...]
    lane_i = _iota2((tm, LANES), 1)
    lane = lane_i.astype(F32)
    lane_grp = lax.shift_right_logical(lane_i, 3).astype(F32)
    neg = -jnp.inf
    logits = _dot3(h, w_ref[...]) + b_ref[...]
    gl = jnp.where((lane_i >= N_EXPERTS) & (lane_i < N_EXPERTS + N_GROUPS), logits, neg)
    gmax = jnp.max(gl, axis=-1, keepdims=True)
    gidx = jnp.min(jnp.where(gl == gmax, lane, float(LANES)), axis=-1, keepdims=True) - float(N_EXPERTS)
    grp_w = 1.0 / jnp.sum(jnp.exp(gl - gmax), axis=-1, keepdims=True)
    sel = jnp.where((lane_i < N_EXPERTS) & (lane_grp == gidx), logits, neg)
    m1 = jnp.max(sel, axis=-1, keepdims=True)
    i1 = jnp.min(jnp.where(sel == m1, lane, float(LANES)), axis=-1, keepdims=True)
    sel2 = jnp.where(lane == i1, neg, sel)
    m2 = jnp.max(sel2, axis=-1, keepdims=True)
    i2 = jnp.min(jnp.where(sel2 == m2, lane, float(LANES)), axis=-1, keepdims=True)
    e2 = jnp.exp(m2 - m1)
    w1 = grp_w / (1.0 + e2)
    w2 = grp_w * e2 / (1.0 + e2)
    hits = jnp.where((lane == i1) | (lane == i2), 1.0, 0.0)
    before = counts_ref[...] + _bdot(_order_mask(tm, False, True, tm), hits)
    r1 = jnp.sum(jnp.where(lane == i1, before, 0.0), axis=-1, keepdims=True)
    r2 = jnp.sum(jnp.where(lane == i2, before, 0.0), axis=-1, keepdims=True)
    counts_ref[...] += jnp.sum(hits, axis=0, keepdims=True)
    route_ref[...] = jnp.where(lane_i == 0, i1, jnp.where(lane_i == 1, i2, jnp.where(
        lane_i == 2, w1, jnp.where(lane_i == 3, w2, jnp.where(lane_i == 4, r1, jnp.where(lane_i == 5, r2, 0.0))))))


def moe_router(x, g, w_group, b_group, w_router, b_router, tm):
    m, d = x.shape
    tm = min(tm, m)
    pad = lambda w: jnp.pad(w, ((0, 0), (0, LANES - w.shape[1])))
    full = lambda arr: pl.BlockSpec(arr.shape, lambda i: (0, 0))
    params = [g.reshape(1, d), pad(jnp.concatenate([w_router, w_group], axis=1)),
              pad(jnp.concatenate([b_router, b_group]).reshape(1, -1))]
    return pl.pallas_call(
        _moe_router_kernel,
        grid=(m // tm,),
        in_specs=[pl.BlockSpec((tm, d), lambda i: (i, 0))] + [full(p) for p in params],
        out_specs=[pl.BlockSpec((tm, LANES), lambda i: (i, 0)), pl.BlockSpec((1, LANES), lambda i: (0, 0))],
        out_shape=[jax.ShapeDtypeStruct((m, LANES), F32), jax.ShapeDtypeStruct((1, LANES), F32)],
        compiler_params=_cparams(("arbitrary",)),
        name="moe_router",
    )(x, *params)


def _route_tables(route, counts, tm, n_tiles):
    counts = counts[0, :N_EXPERTS].astype(jnp.int32)
    padded = ((counts + tm - 1) // tm) * tm
    ends = jnp.cumsum(padded)
    starts = ends - padded
    experts = route[:, 0:2].astype(jnp.int32)
    pos = (starts[experts] + route[:, 4:6].astype(jnp.int32)).reshape(-1)
    tile_start = jnp.arange(n_tiles, dtype=jnp.int32) * tm
    tile_e = jnp.minimum(jnp.sum(tile_start[:, None] >= ends[None, :], axis=1), N_EXPERTS - 1).astype(jnp.int32)
    n_valid = (ends[-1:] // tm).astype(jnp.int32)
    fill = jnp.stack([jnp.maximum(ends - tm, 0), (padded > 0).astype(jnp.int32)]).astype(jnp.int32)
    return pos, tile_e, n_valid, fill


def _moe_dispatch_kernel(pos_ref, nvalid_ref, fill_ref, x_ref, g_ref, hs_hbm, hbuf, zbuf, sem_rows, sem_fill, *,
                         n_tiles):
    i = pl.program_id(0)
    tm = x_ref.shape[0]
    row_tile = zbuf.shape[0]

    @pl.when(i == 0)
    def _():
        zbuf[...] = jnp.zeros_like(zbuf)
        fills = [(fill_ref[1, e] > 0, fill_ref[0, e]) for e in range(N_EXPERTS)]
        fills += [(nvalid_ref[0] + k < n_tiles, (nvalid_ref[0] + k) * row_tile) for k in range(N_EXPERTS)]
        fills = [(cond, pl.multiple_of(start, row_tile)) for cond, start in fills]
        for cond, start in fills:
            @pl.when(cond)
            def _(start=start):
                pltpu.make_async_copy(zbuf, hs_hbm.at[pl.ds(start, row_tile)], sem_fill).start()
        for cond, start in fills:
            @pl.when(cond)
            def _(start=start):
                pltpu.make_async_copy(zbuf, hs_hbm.at[pl.ds(start, row_tile)], sem_fill).wait()

    buf = i % 2
    x = x_ref[...]
    ms = jnp.mean(x * x, axis=-1, keepdims=True)
    hbuf[buf] = x * lax.rsqrt(ms + EPS) * g_ref[...]

    def issue(r, carry):
        base = 2 * (i * tm + r)
        for slot in range(2):
            pltpu.make_async_copy(hbuf.at[buf, pl.ds(r, 1)], hs_hbm.at[pl.ds(pos_ref[base + slot], 1)],
                                  sem_rows.at[buf]).start()
        return carry

    lax.fori_loop(0, tm, issue, 0, unroll=8)

    def drain(b):
        for _ in range(2):
            pltpu.make_async_copy(hbuf.at[b], hs_hbm.at[pl.ds(0, tm)], sem_rows.at[b]).wait()

    @pl.when(i > 0)
    def _():
        drain(1 - buf)

    @pl.when(i == pl.num_programs(0) - 1)
    def _():
        drain(buf)


def moe_dispatch(x, g, pos, n_valid, fill, n_tiles, tm):
    m, d = x.shape
    tm_x = min(256, m)
    return pl.pallas_call(
        functools.partial(_moe_dispatch_kernel, n_tiles=n_tiles),
        grid_spec=pltpu.PrefetchScalarGridSpec(
            num_scalar_prefetch=3,
            grid=(m // tm_x,),
            in_specs=[pl.BlockSpec((tm_x, d), lambda i, p, nv, fl: (i, 0)),
                      pl.BlockSpec((1, d), lambda i, p, nv, fl: (0, 0))],
            out_specs=pl.BlockSpec(memory_space=pl.ANY),
            scratch_shapes=[pltpu.VMEM((2, tm_x, d), F32), pltpu.VMEM((tm, d), F32),
                            pltpu.SemaphoreType.DMA((2,)), pltpu.SemaphoreType.DMA(())]),
        out_shape=jax.ShapeDtypeStruct((n_tiles * tm, d), F32),
        compiler_params=_cparams(("arbitrary",)),
        name="moe_dispatch",
    )(pos, n_valid, fill, x, g.reshape(1, d))


def _moe_ffn_kernel(tile_e_ref, nvalid_ref, h_ref, wg_ref, wu_ref, wd_ref, y_ref, wg_b, wu_b, wd_b):
    i = pl.program_id(0)

    @pl.when((i == 0) | (tile_e_ref[i] != tile_e_ref[jnp.maximum(i - 1, 0)]))
    def _():
        wg_b[...] = wg_ref[0, 0].astype(BF16)
        wu_b[...] = wu_ref[0, 0].astype(BF16)
        wd_b[...] = wd_ref[0, 0].astype(BF16)

    @pl.when(i < nvalid_ref[0])
    def _():
        x = h_ref[...].astype(BF16)
        hid = _silu(_dot(x, wg_b[...])) * _dot(x, wu_b[...])
        y_ref[...] = _dot(hid.astype(BF16), wd_b[...])

    @pl.when(i >= nvalid_ref[0])
    def _():
        y_ref[...] = jnp.zeros_like(y_ref)


def moe_ffn(h_sorted, tile_e, n_valid, w_gate, w_up, w_down, layer, tm):
    _, d = h_sorted.shape
    n_tiles = tile_e.shape[0]
    ff = w_gate.shape[-1]
    wspec = lambda shape: pl.BlockSpec((1, 1) + shape, lambda i, te, nv: (layer, te[i], 0, 0))
    return pl.pallas_call(
        _moe_ffn_kernel,
        grid_spec=pltpu.PrefetchScalarGridSpec(
            num_scalar_prefetch=2,
            grid=(n_tiles,),
            in_specs=[pl.BlockSpec((tm, d), lambda i, te, nv: (i, 0)), wspec((d, ff)), wspec((d, ff)), wspec((ff, d))],
            out_specs=pl.BlockSpec((tm, d), lambda i, te, nv: (i, 0)),
            scratch_shapes=[pltpu.VMEM((d, ff), BF16), pltpu.VMEM((d, ff), BF16), pltpu.VMEM((ff, d), BF16)]),
        out_shape=jax.ShapeDtypeStruct((n_tiles * tm, d), F32),
        compiler_params=_cparams(("arbitrary",)),
        name="moe_ffn",
    )(tile_e, n_valid, h_sorted, w_gate, w_up, w_down)


def _moe_combine_kernel(pos_ref, x_ref, route_ref, y_hbm, g_ref, o_ref, ybuf, sems, *, final_norm):
    i = pl.program_id(0)
    tm = x_ref.shape[0]

    def gather(tile, buf):
        def issue(r, carry):
            base = 2 * (tile * tm + r)
            for slot in range(2):
                pltpu.make_async_copy(y_hbm.at[pl.ds(pos_ref[base + slot], 1)], ybuf.at[buf, slot, pl.ds(r, 1)],
                                      sems.at[buf]).start()
            return carry

        lax.fori_loop(0, tm, issue, 0, unroll=8)

    @pl.when(i == 0)
    def _():
        gather(0, 0)

    buf = i % 2

    @pl.when(i + 1 < pl.num_programs(0))
    def _():
        gather(i + 1, 1 - buf)

    for slot in range(2):
        pltpu.make_async_copy(y_hbm.at[pl.ds(0, tm)], ybuf.at[buf, slot], sems.at[buf]).wait()
    route = route_ref[...]
    o = x_ref[...] + route[:, 2:3] * ybuf[buf, 0] + route[:, 3:4] * ybuf[buf, 1]
    if final_norm:
        ms = jnp.mean(o * o, axis=-1, keepdims=True)
        o = o * lax.rsqrt(ms + EPS) * g_ref[...]
    o_ref[...] = o


def moe_combine(x, route, y_sorted, pos, norm_g, tm):
    m, d = x.shape
    tm = min(tm, m)
    final_norm = norm_g is not None
    g = (norm_g if final_norm else jnp.ones((d,), F32)).reshape(1, d)
    return pl.pallas_call(
        functools.partial(_moe_combine_kernel, final_norm=final_norm),
        grid_spec=pltpu.PrefetchScalarGridSpec(
            num_scalar_prefetch=1,
            grid=(m // tm,),
            in_specs=[pl.BlockSpec((tm, d), lambda i, p: (i, 0)), pl.BlockSpec((tm, LANES), lambda i, p: (i, 0)),
                      pl.BlockSpec(memory_space=pl.ANY), pl.BlockSpec((1, d), lambda i, p: (0, 0))],
            out_specs=pl.BlockSpec((tm, d), lambda i, p: (i, 0)),
            scratch_shapes=[pltpu.VMEM((2, 2, tm, d), F32), pltpu.SemaphoreType.DMA((2,))]),
        out_shape=jax.ShapeDtypeStruct((m, d), F32),
        compiler_params=_cparams(("arbitrary",)),
        name="moe_combine",
    )(pos, x, route, y_sorted, g)


def _pad_rows(w, rows):
    return jnp.pad(w, ((0, rows - w.shape[0]), (0, 0)))


def _even_layer(x, norm_g, w_in, gla_w_alpha, gla_b_alpha, gla_norm, mu, w0, w2, a0, a2, g2,
                k_k, k_a, r_k, ln_w, ln_b, w_out):
    b, s, d = x.shape
    gla_cols, rw = w_in[:, :3104], w_in[:, 3104:]
    mu_g = lambda lo, hi, width: jnp.pad(mu[:, lo:hi], ((0, 0), (0, width - (hi - lo))))
    pad_c = lambda w, width: jnp.pad(w, ((0, 0), (0, width - w.shape[1])))
    w_cat = jnp.concatenate([
        gla_cols[:, 0:3072],
        rw[:, 0:3072], rw[:, 3360:3616],
        pad_c(rw[:, 3072:3168], LANES), pad_c(rw[:, 3168:3264], LANES), pad_c(rw[:, 3264:3360], LANES),
        pad_c(gla_cols[:, 3072:3104], LANES)], axis=1).astype(BF16)
    mu_cat = jnp.concatenate([
        mu[:, 0:3072], mu[:, 3360:3616], mu_g(3072, 3168, LANES), mu_g(3168, 3264, LANES),
        mu_g(3264, 3360, LANES)], axis=1)
    z = norm_matmul(x.reshape(b * s, d), norm_g, w_cat, 1024, 768).reshape(b, s, EVEN_PAD)

    wa = jnp.stack([_pad_rows(gla_w_alpha[0], LANES),
                    jnp.pad(gla_w_alpha[1], ((GLA_LOWRANK, LANES - 2 * GLA_LOWRANK), (0, 0)))])
    gla_f, gla_b = gla_scan(z, wa, gla_b_alpha)
    y_gla = gla_post(gla_f, gla_b, z, gla_norm, 256)

    head_of_lane = jnp.arange(RWKV_WIDTH) // RWKV_HEAD
    seg_e = (head_of_lane[:, None] == jnp.arange(LANES)[None, :]).astype(F32)
    seg_et = seg_e.T
    row = lambda p: p.reshape(1, RWKV_WIDTH)
    r, k, v, kk, bb, cum_f, exc_f, cum_b, exc_b, g, bonus = rwkv_prep(
        z, mu_cat, w0, _pad_rows(w2[0], LANES), _pad_rows(w2[1], LANES), row(a0), _pad_rows(a2, LANES), g2,
        row(k_k), row(k_a), row(r_k), seg_e, seg_et, 128)
    rw_f, rw_b = rwkv_scan(r, k, v, kk, bb, cum_f, exc_f, cum_b, exc_b, RWKV_PAIRS_PER_STEP)
    y_rwkv = rwkv_post(rw_f, rw_b, bonus, g, ln_w, ln_b, seg_e, seg_et, 256)

    w_out = w_out.astype(BF16)
    out = proj_residual([y_gla.reshape(b * s, GLA_WIDTH), y_rwkv.reshape(b * s, RWKV_WIDTH)],
                        [w_out[:GLA_WIDTH], w_out[GLA_WIDTH:]], x.reshape(b * s, d), 1024, 1024)
    return out.reshape(b, s, d)


def _odd_layer(x, norm_g, w_in, conv_w, a_log, dt_bias, norm_w, w_out):
    b, s, d = x.shape
    x2 = x.reshape(b * s, d)
    main = GDN_QKV + GDN_VW
    z = norm_matmul(x2, norm_g, w_in[:, :main].astype(BF16), 1024, 2048, BF16).reshape(b, s, main)
    zs = norm_matmul(x2, norm_g, w_in[:, main:].astype(BF16), 512, LANES).reshape(b, s, LANES)
    scale = jnp.concatenate([jnp.full((1, GDN_KW), GDN_DK ** -0.5, F32), jnp.ones((1, GDN_KW), F32)], axis=1)
    qk = gdn_conv(z, conv_w, scale, 0, 2 * GDN_KW, True, 256, 1024)
    v = gdn_conv(z, conv_w, scale, 2 * GDN_KW, GDN_VW, False, 256, 1024)
    zero = jnp.zeros((2 * GDN_VHEADS,), F32)
    neg_a = jnp.concatenate([zero, -jnp.exp(a_log.reshape(-1))]).reshape(1, LANES)
    dtb = jnp.concatenate([zero, dt_bias.reshape(-1)]).reshape(1, LANES)
    gates = gdn_gates(zs, neg_a, dtb, 256)
    o_f, o_b = gdn_scan(qk, v, gates, GDN_HEADS_PER_STEP)
    y = gdn_post(o_f, o_b, z, GDN_QKV, norm_w, 256, 1024)
    out = proj_residual([y.reshape(b * s, GDN_VW)], [w_out.astype(BF16)], x2, 1024, 1024)
    return out.reshape(b, s, d)


def _moe_layer(x, norm_g, w_group, b_group, w_router, b_router, w_gate, w_up, w_down, layer, final_g):
    b, s, d = x.shape
    x2 = x.reshape(b * s, d)
    route, counts = moe_router(x2, norm_g, w_group, b_group, w_router, b_router, 256)
    n_tiles = -(-2 * b * s // MOE_ROW_TILE) + N_EXPERTS
    pos, tile_e, n_valid, fill = _route_tables(route, counts, MOE_ROW_TILE, n_tiles)
    h_sorted = moe_dispatch(x2, norm_g, pos, n_valid, fill, n_tiles, MOE_ROW_TILE)
    y_sorted = moe_ffn(h_sorted, tile_e, n_valid, w_gate, w_up, w_down, layer, MOE_ROW_TILE)
    out = moe_combine(x2, route, y_sorted, pos, final_g, 256)
    return out.reshape(b, s, d)


def kernel(x, norm_mix, norm_ffn, norm_final, ev_w_in, ev_gla_w_alpha, ev_gla_b_alpha, ev_gla_norm, ev_rwkv_mu, ev_rwkv_w0, ev_rwkv_w2, ev_rwkv_a0, ev_rwkv_a2, ev_rwkv_g2, ev_rwkv_k_k, ev_rwkv_k_a, ev_rwkv_r_k, ev_rwkv_ln_w, ev_rwkv_ln_b, ev_w_out, od_w_in, od_conv, od_a_log, od_dt_bias, od_norm, od_w_out, moe_w_group, moe_b_group, moe_w_router, moe_b_router, moe_w_gate, moe_w_up, moe_w_down):
    depth = norm_mix.shape[0]
    for i in range(depth):
        j = i // 2
        if i % 2 == 0:
            x = _even_layer(x, norm_mix[i], ev_w_in[j], ev_gla_w_alpha[j], ev_gla_b_alpha[j], ev_gla_norm[j],
                            ev_rwkv_mu[j], ev_rwkv_w0[j], ev_rwkv_w2[j], ev_rwkv_a0[j], ev_rwkv_a2[j],
                            ev_rwkv_g2[j], ev_rwkv_k_k[j], ev_rwkv_k_a[j], ev_rwkv_r_k[j],
                            ev_rwkv_ln_w[j], ev_rwkv_ln_b[j], ev_w_out[j])
        else:
            x = _odd_layer(x, norm_mix[i], od_w_in[j], od_conv[j], od_a_log[j], od_dt_bias[j],
                           od_norm[j], od_w_out[j])
        x = _moe_layer(x, norm_ffn[i], moe_w_group[i], moe_b_group[i], moe_w_router[i], moe_b_router[i],
                       moe_w_gate, moe_w_up, moe_w_down, i, norm_final if i == depth - 1 else None)
    return x
```

```python
import functools
import math

import jax
import jax.numpy as jnp
from jax import lax
from jax.experimental import pallas as pl
from jax.experimental.pallas import tpu as pltpu

F32 = jnp.float32
BF16 = jnp.bfloat16
HI = lax.Precision.HIGHEST

EPS = 1e-6
CHUNK = 64
LANES = 128
VMEM_LIMIT = 56 * 1024 * 1024

D_MODEL = 2048
GLA_HEADS = 4
GLA_DK = 128
GLA_DV = 256
GLA_KW = 512
GLA_WIDTH = 1024
GLA_LOWRANK = 16
GLA_TAU = 16.0
RWKV_HEAD = 64
RWKV_WIDTH = 1024
RWKV_HEADS = 16
DECAY_LORA = 96
ICLR_LORA = 96
GATE_LORA = 256
RWKV_LN_EPS = 64e-5
GDN_DK = 128
GDN_DV = 128
GDN_KHEADS = 16
GDN_VHEADS = 32
GDN_KW = 2048
GDN_VW = 4096
GDN_QKV = 8192
SCAN_CHUNKS_PER_STEP = 4
RWKV_PAIRS_PER_STEP = 4
GDN_HEADS_PER_STEP = 4
MOE_ROW_TILE = 256
N_GROUPS = 4
EXPERTS_PER_GROUP = 8
N_EXPERTS = 32
EXPERT_FF = 512

GQ_OFF, GK_OFF, GV_OFF, GG_OFF = 0, 512, 1024, 2048
R_OFF, K_OFF, V_OFF = 3072, 4096, 5120
GLR_OFF, WLF_OFF, WLB_OFF, ALR_OFF, AL_OFF = 6144, 6400, 6528, 6656, 6784
RWKV_MAIN = 3072
RWKV_SMALL = 768
EVEN_PAD = 6912


def _cparams(sem):
    return pltpu.CompilerParams(dimension_semantics=sem, vmem_limit_bytes=VMEM_LIMIT)


def _mm(a, b, ca, cb, precision):
    if a.ndim == 3:
        dims = (((ca + 1,), (cb + 1,)), ((0,), (0,)))
    else:
        dims = (((ca,), (cb,)), ((), ()))
    return lax.dot_general(a, b, dims, preferred_element_type=F32, precision=precision)


def _dot(a, b, precision=None):
    return _mm(a, b, 1, 0, precision)


def _dot_nt(a, b, precision=None):
    return _mm(a, b, 1, 1, precision)


def _dot_tn(a, b, precision=None):
    return _mm(a, b, 0, 0, precision)


def _bf16_pieces(x, n):
    pieces = []
    for _ in range(n - 1):
        p = x.astype(BF16)
        pieces.append(p)
        x = x - p.astype(F32)
    pieces.append(x.astype(BF16))
    return pieces


def _mask_dot(mask, x, pieces=3):
    mb = mask.astype(BF16)
    return sum(_dot(mb, p) for p in _bf16_pieces(x, pieces))


def _dot_mask(x, mask, pieces=2):
    mb = mask.astype(BF16)
    return sum(_dot(p, mb) for p in _bf16_pieces(x, pieces))


def _dot3(a, b):
    ah, al = _bf16_pieces(a, 2)
    bh, bl = _bf16_pieces(b, 2)
    return _dot(ah, bh) + _dot(ah, bl) + _dot(al, bh)


def _bdot(a, b):
    return _dot(a.astype(BF16), b.astype(BF16))


def _bdot_nt(a, b):
    return _dot_nt(a.astype(BF16), b.astype(BF16))


def _bdot_tn(a, b):
    return _dot_tn(a.astype(BF16), b.astype(BF16))


def _sigmoid(x):
    return 1.0 / (1.0 + jnp.exp(-x))


def _silu(x):
    return x * _sigmoid(x)


def _softplus(x):
    return jnp.maximum(x, 0.0) + jnp.log(1.0 + jnp.exp(-jnp.abs(x)))


def _iota2(shape, dim):
    return lax.broadcasted_iota(jnp.int32, shape, dim)


def _order_mask(n, rev, strict, block):
    i = _iota2((n, n), 0)
    j = _iota2((n, n), 1)
    if rev:
        m = (j > i) if strict else (j >= i)
    else:
        m = (j < i) if strict else (j <= i)
    if block < n:
        sh = block.bit_length() - 1
        m = m & (lax.shift_right_logical(i, sh) == lax.shift_right_logical(j, sh))
    return m


def _neumann_inverse(nmat, negate=False):
    n = nmat.shape[-1]
    eye = (_iota2((n, n), 0) == _iota2((n, n), 1)).astype(F32)
    t = eye - nmat if negate else eye + nmat
    nb = nmat.astype(BF16)
    p = _dot(nb, nb)
    for step in range(5):
        pb = p.astype(BF16)
        if step < 4:
            both = _dot(pb, jnp.concatenate([pb, t.astype(BF16)], axis=-1))
            p = both[..., :n]
            t = t + both[..., n:]
        else:
            t = t + _dot(pb, t.astype(BF16))
    return t


def _norm_matmul_kernel(x_ref, g_ref, w_ref, o_ref, h_ref):
    @pl.when(pl.program_id(1) == 0)
    def _():
        x = x_ref[...]
        ms = jnp.mean(x * x, axis=-1, keepdims=True)
        h_ref[...] = (x * lax.rsqrt(ms + EPS) * g_ref[...]).astype(BF16)

    o_ref[...] = _dot(h_ref[...], w_ref[...]).astype(o_ref.dtype)


def norm_matmul(x, g, w, tm, tn, out_dtype=F32):
    m, d = x.shape
    tm = min(tm, m)
    n = w.shape[1]
    return pl.pallas_call(
        _norm_matmul_kernel,
        grid=(m // tm, n // tn),
        in_specs=[pl.BlockSpec((tm, d), lambda i, j: (i, 0)),
                  pl.BlockSpec((1, d), lambda i, j: (0, 0)),
                  pl.BlockSpec((d, tn), lambda i, j: (0, j))],
        out_specs=pl.BlockSpec((tm, tn), lambda i, j: (i, j)),
        out_shape=jax.ShapeDtypeStruct((m, n), out_dtype),
        scratch_shapes=[pltpu.VMEM((tm, d), BF16)],
        compiler_params=_cparams(("parallel", "arbitrary")),
        name="norm_matmul",
    )(x, g.reshape(1, d), w)


def _proj_residual_kernel(*refs, n_lhs):
    x_ref = refs[2 * n_lhs]
    o_ref = refs[2 * n_lhs + 1]
    acc = x_ref[...]
    for t in range(n_lhs):
        acc = acc + _dot(refs[t][...], refs[n_lhs + t][...])
    o_ref[...] = acc


def proj_residual(ys, ws, x, tm, tn):
    m, n = x.shape
    tm = min(tm, m)
    n_lhs = len(ys)
    in_specs = [pl.BlockSpec((tm, y.shape[1]), lambda i, j: (i, 0)) for y in ys]
    in_specs += [pl.BlockSpec((w.shape[0], tn), lambda i, j: (0, j)) for w in ws]
    in_specs += [pl.BlockSpec((tm, tn), lambda i, j: (i, j))]
    return pl.pallas_call(
        functools.partial(_proj_residual_kernel, n_lhs=n_lhs),
        grid=(m // tm, n // tn),
        in_specs=in_specs,
        out_specs=pl.BlockSpec((tm, tn), lambda i, j: (i, j)),
        out_shape=jax.ShapeDtypeStruct((m, n), F32),
        compiler_params=_cparams(("parallel", "arbitrary")),
        name="proj_residual",
    )(*ys, *ws, x)


def _final_norm_kernel(x_ref, g_ref, o_ref):
    x = x_ref[...]
    ms = jnp.mean(x * x, axis=-1, keepdims=True)
    o_ref[...] = x * lax.rsqrt(ms + EPS) * g_ref[...]


def final_norm(x, g, tm):
    m, d = x.shape
    tm = min(tm, m)
    return pl.pallas_call(
        _final_norm_kernel,
        grid=(m // tm,),
        in_specs=[pl.BlockSpec((tm, d), lambda i: (i, 0)), pl.BlockSpec((1, d), lambda i: (0, 0))],
        out_specs=pl.BlockSpec((tm, d), lambda i: (i, 0)),
        out_shape=jax.ShapeDtypeStruct((m, d), F32),
        compiler_params=_cparams(("parallel",)),
        name="final_norm",
    )(x, g.reshape(1, d))


def _gla_chunks(q, k, v, cum, revs, s_ref):
    c = q.shape[1]
    tot = jnp.stack([cum[j, (0 if rev else c - 1)][None] for j, rev in enumerate(revs)], axis=0)
    incl = jnp.stack([_order_mask(c, rev, False, c) for rev in revs], axis=0)
    q_dec = q * ((GLA_DK ** -0.5) * jnp.exp(cum))
    k_dec = k * jnp.exp(-cum)
    k_end = k * jnp.exp(tot - cum)
    scores = jnp.where(incl, _bdot_nt(q_dec, k_dec), 0.0)
    state = s_ref[...]
    o = _bdot(scores, v) + _bdot_nt(q_dec, state)
    s_ref[...] = jnp.exp(tot) * state + _bdot_tn(v, k_end)
    return o


def _gla_kernel(qf, kf, vf, alf, qb, kb, vb, alb, wa_ref, ba_ref, of_ref, ob_ref, s_ref):
    @pl.when(pl.program_id(1) == 0)
    def _():
        s_ref[...] = jnp.zeros_like(s_ref)

    c = CHUNK
    rows_all = qf.shape[1]
    sub_steps = rows_all // c
    cum_of = []
    for d, al_ref in enumerate((alf, alb)):
        pre = _dot3(al_ref[0], wa_ref[d]) + ba_ref[d:d + 1]
        log_a = (jnp.minimum(pre, 0.0) - jnp.log(1.0 + jnp.exp(-jnp.abs(pre)))) * (1.0 / GLA_TAU)
        cum_of.append(_mask_dot(_order_mask(rows_all, d == 1, False, c), log_a))
    for sub in range(sub_steps):
        rows_of = (slice(sub * c, (sub + 1) * c), slice((sub_steps - 1 - sub) * c, (sub_steps - sub) * c))
        qs, ks, vs, cums, revs = [], [], [], [], []
        for d, (q_ref, k_ref, v_ref) in enumerate(((qf, kf, vf), (qb, kb, vb))):
            rows = rows_of[d]
            for h in range(GLA_HEADS):
                ksl = slice(h * GLA_DK, (h + 1) * GLA_DK)
                qs.append(q_ref[0, rows, ksl])
                ks.append(k_ref[0, rows, ksl])
                vs.append(v_ref[0, rows, h * GLA_DV:(h + 1) * GLA_DV])
                cums.append(cum_of[d][rows, ksl])
                revs.append(d == 1)
        o = _gla_chunks(jnp.stack(qs), jnp.stack(ks), jnp.stack(vs), jnp.stack(cums), revs, s_ref)
        for d, o_ref in enumerate((of_ref, ob_ref)):
            for h in range(GLA_HEADS):
                o_ref[0, rows_of[d], h * GLA_DV:(h + 1) * GLA_DV] = o[d * GLA_HEADS + h].astype(o_ref.dtype)


def gla_scan(z, wa, ba):
    b, s, _ = z.shape
    c = CHUNK * SCAN_CHUNKS_PER_STEP
    n = s // c

    def fwd(off, w):
        return pl.BlockSpec((1, c, w), lambda bi, t: (bi, t, off // w))

    def bwd(off, w):
        return pl.BlockSpec((1, c, w), lambda bi, t: (bi, n - 1 - t, off // w))

    in_specs = [fwd(GQ_OFF, GLA_KW), fwd(GK_OFF, GLA_KW), fwd(GV_OFF, GLA_WIDTH), fwd(AL_OFF, LANES),
                bwd(GQ_OFF, GLA_KW), bwd(GK_OFF, GLA_KW), bwd(GV_OFF, GLA_WIDTH), bwd(AL_OFF, LANES),
                pl.BlockSpec(wa.shape, lambda bi, t: (0, 0, 0)), pl.BlockSpec(ba.shape, lambda bi, t: (0, 0))]
    out_specs = [pl.BlockSpec((1, c, GLA_WIDTH), lambda bi, t: (bi, t, 0)),
                 pl.BlockSpec((1, c, GLA_WIDTH), lambda bi, t: (bi, n - 1 - t, 0))]
    return pl.pallas_call(
        _gla_kernel,
        grid=(b, n),
        in_specs=in_specs,
        out_specs=out_specs,
        out_shape=[jax.ShapeDtypeStruct((b, s, GLA_WIDTH), BF16)] * 2,
        scratch_shapes=[pltpu.VMEM((2 * GLA_HEADS, GLA_DV, GLA_DK), F32)],
        compiler_params=_cparams(("parallel", "arbitrary")),
        name="gla_scan",
    )(z, z, z, z, z, z, z, z, wa, ba)


def _gla_post_kernel(of_ref, ob_ref, gate_ref, nw_ref, y_ref):
    o = of_ref[0].astype(F32) + ob_ref[0].astype(F32)
    ms = jnp.mean(o * o, axis=-1, keepdims=True)
    y = o * lax.rsqrt(ms + EPS) * nw_ref[...]
    y_ref[0] = (y * _silu(gate_ref[0])).astype(y_ref.dtype)


def gla_post(o_f, o_b, z, norm_w, tm):
    b, s, w = o_f.shape
    blk = pl.BlockSpec((1, tm, GLA_DV), lambda bi, t, h: (bi, t, h))
    return pl.pallas_call(
        _gla_post_kernel,
        grid=(b, s // tm, GLA_HEADS),
        in_specs=[blk, blk, pl.BlockSpec((1, tm, GLA_DV), lambda bi, t, h: (bi, t, GG_OFF // GLA_DV + h)),
                  pl.BlockSpec((1, GLA_DV), lambda bi, t, h: (0, 0))],
        out_specs=blk,
        out_shape=jax.ShapeDtypeStruct((b, s, w), BF16),
        compiler_params=_cparams(("parallel", "parallel", "parallel")),
        name="gla_post",
    )(o_f, o_b, z, norm_w.reshape(1, GLA_DV))


def _seg_sum(x, e, et):
    return _dot_mask(_dot_mask(x, e), et)


def _rwkv_prep_kernel(z_ref, zp_ref, zn_ref, y_ref, yp_ref, yn_ref, mu_ref, w0_ref, w2f_ref, w2b_ref, a0_ref, a2_ref, g2_ref,
                      kk_ref, ka_ref, rk_ref, e_ref, et_ref,
                      r_out, k_out, v_out, kk_out, b_out, cumf_out, excf_out, cumb_out, excb_out, g_out, bonus_out):
    t = pl.program_id(1)
    tm = z_ref.shape[1]
    row = _iota2((tm, 1), 0)
    first = t == 0
    last = t == pl.num_programs(1) - 1

    def shifted(lo, hi):
        cur, prv, nxt, base = (z_ref, zp_ref, zn_ref, R_OFF) if lo < GLR_OFF else (y_ref, yp_ref, yn_ref, GLR_OFF)
        z = cur[0, :, lo - base:hi - base]
        prev_row = jnp.where(first, 0.0, prv[0, 7:8, lo - base:hi - base])
        next_row = jnp.where(last, 0.0, nxt[0, 0:1, lo - base:hi - base])
        zprev = jnp.where(row == 0, prev_row, pltpu.roll(z, 1, axis=0))
        znext = jnp.where(row == tm - 1, next_row, pltpu.roll(z, tm - 1, axis=0))
        mu = mu_ref[:, lo - R_OFF:hi - R_OFF]
        return z + mu[0:1] * (zprev - z) + mu[1:2] * (znext - z)

    e = e_ref[...]
    et = et_ref[...]
    a = _sigmoid(a0_ref[...] + _bdot(shifted(ALR_OFF, ALR_OFF + LANES), a2_ref[...]))
    k = shifted(K_OFF, K_OFF + RWKV_WIDTH)
    kk_raw = k * kk_ref[...]
    kk = kk_raw * lax.rsqrt(_seg_sum(kk_raw * kk_raw, e, et) + EPS)
    kk_out[0] = kk.astype(kk_out.dtype)
    b_out[0] = (kk * a).astype(b_out.dtype)
    kmod = k * (1.0 + (a - 1.0) * ka_ref[...])
    k_out[0] = kmod.astype(k_out.dtype)
    r = shifted(R_OFF, R_OFF + RWKV_WIDTH)
    r_out[0] = r.astype(r_out.dtype)
    v = shifted(V_OFF, V_OFF + RWKV_WIDTH)
    v_out[0] = v.astype(v_out.dtype)
    bonus_out[0] = (_seg_sum(r * kmod * rk_ref[...], e, et) * v).astype(bonus_out.dtype)
    g_out[0] = _bdot(_sigmoid(shifted(GLR_OFF, GLR_OFF + GATE_LORA)), g2_ref[...]).astype(g_out.dtype)
    decay_scale = -math.exp(-0.5)
    wf = w0_ref[0:1] + _dot3(jnp.tanh(shifted(WLF_OFF, WLF_OFF + LANES)), w2f_ref[...])
    lwf = decay_scale * _sigmoid(wf)
    cum_f = _mask_dot(_order_mask(tm, False, False, CHUNK), lwf)
    cumf_out[0] = cum_f
    excf_out[0] = cum_f - lwf
    wb = w0_ref[1:2] + _dot3(jnp.tanh(shifted(WLB_OFF, WLB_OFF + LANES)), w2b_ref[...])
    lwb = decay_scale * _sigmoid(wb)
    cum_b = _mask_dot(_order_mask(tm, True, False, CHUNK), lwb)
    cumb_out[0] = cum_b
    excb_out[0] = cum_b - lwb


def rwkv_prep(z, mu, w0, w2f, w2b, a0, a2, g2, k_k, k_a, r_k, seg_e, seg_et, tm):
    b, s, _ = z.shape
    nt = s // tm
    hb = tm // 8
    full = lambda arr: pl.BlockSpec(arr.shape, lambda bi, t: (0,) * arr.ndim)
    in_specs = []
    for width, off in ((RWKV_MAIN, R_OFF), (RWKV_SMALL, GLR_OFF)):
        cb = off // width
        in_specs += [pl.BlockSpec((1, tm, width), lambda bi, t, cb=cb: (bi, t, cb)),
                     pl.BlockSpec((1, 8, width), lambda bi, t, cb=cb: (bi, jnp.maximum(t * hb - 1, 0), cb)),
                     pl.BlockSpec((1, 8, width), lambda bi, t, cb=cb: (bi, jnp.minimum((t + 1) * hb, nt * hb - 1), cb))]
    params = [mu, w0, w2f, w2b, a0, a2, g2, k_k, k_a, r_k, seg_e, seg_et]
    in_specs += [full(p) for p in params]
    out_blk = pl.BlockSpec((1, tm, RWKV_WIDTH), lambda bi, t: (bi, t, 0))
    return pl.pallas_call(
        _rwkv_prep_kernel,
        grid=(b, nt),
        in_specs=in_specs,
        out_specs=[out_blk] * 11,
        out_shape=[jax.ShapeDtypeStruct((b, s, RWKV_WIDTH), dt) for dt in [BF16] * 5 + [F32] * 4 + [BF16] * 2],
        compiler_params=_cparams(("parallel", "parallel")),
        name="rwkv_prep",
    )(z, z, z, z, z, z, *params)


def _stack_heads(x):
    lane = _iota2(x.shape, 2)
    return jnp.concatenate([jnp.where(lane < RWKV_HEAD, x, 0.0), jnp.where(lane < RWKV_HEAD, 0.0, x)], axis=1)


def _rwkv_chunks(r, k, v, kk, bb, cum, exc, revs, g_ref):
    nb, c, _ = r.shape
    n2 = 2 * c
    tot = jnp.stack([cum[j, (0 if rev else c - 1)][None] for j, rev in enumerate(revs)], axis=0)
    e_neg = jnp.exp(-cum)
    e_end = jnp.exp(tot - cum)
    al2 = _stack_heads(-kk * jnp.exp(exc))
    rb2 = _stack_heads(r * jnp.exp(cum))
    bt2 = _stack_heads(bb * e_neg)
    kt2 = _stack_heads(k * e_neg)
    be2 = _stack_heads(bb * e_end)
    ke2 = _stack_heads(k * e_end)
    v2 = _stack_heads(v)
    gram = _bdot_nt(jnp.concatenate([al2, rb2], axis=1), jnp.concatenate([bt2, kt2], axis=1))
    strict = jnp.stack([_order_mask(n2, rev, True, c) for rev in revs], axis=0)
    incl = jnp.stack([_order_mask(n2, rev, False, c) for rev in revs], axis=0)
    a_ab = jnp.where(strict, gram[:, :n2, :n2], 0.0)
    a_ak = jnp.where(strict, gram[:, :n2, n2:], 0.0)
    a_rb = jnp.where(incl, gram[:, n2:, :n2], 0.0)
    a_rk = jnp.where(incl, gram[:, n2:, n2:], 0.0)
    t_inv = _neumann_inverse(a_ab)
    wu = _bdot(t_inv, jnp.concatenate([al2, _bdot(a_ak, v2)], axis=2))
    g = g_ref[...]
    proj = _bdot_nt(jnp.concatenate([wu[:, :, :LANES], rb2], axis=1), g)
    u2 = proj[:, :n2] + wu[:, :, LANES:]
    uv = jnp.concatenate([u2, v2], axis=1)
    o2 = proj[:, n2:] + _bdot(jnp.concatenate([a_rb, a_rk], axis=2), uv)
    g_ref[...] = g * jnp.exp(tot) + _bdot_tn(uv, jnp.concatenate([be2, ke2], axis=1))
    return o2[:, :c] + o2[:, c:]


def _rwkv_scan_kernel(*refs, pairs):
    fwd_refs, bwd_refs = refs[0:7], refs[7:14]
    of_ref, ob_ref, g_ref = refs[14:17]

    @pl.when(pl.program_id(2) == 0)
    def _():
        g_ref[...] = jnp.zeros_like(g_ref)

    revs = [False] * pairs + [True] * pairs
    c = CHUNK
    sub_steps = of_ref.shape[1] // c
    for sub in range(sub_steps):
        rows_of = (slice(sub * c, (sub + 1) * c), slice((sub_steps - 1 - sub) * c, (sub_steps - sub) * c))
        operands = []
        for t in range(7):
            operands.append(jnp.stack([ref[0, rows_of[d], p * LANES:(p + 1) * LANES].astype(F32)
                                       for d, ref in enumerate((fwd_refs[t], bwd_refs[t])) for p in range(pairs)],
                                      axis=0))
        o = _rwkv_chunks(*operands, revs, g_ref)
        for d, o_ref in enumerate((of_ref, ob_ref)):
            for p in range(pairs):
                o_ref[0, rows_of[d], p * LANES:(p + 1) * LANES] = o[d * pairs + p].astype(o_ref.dtype)


def rwkv_scan(r, k, v, kk, bb, cum_f, exc_f, cum_b, exc_b, pairs):
    b, s, _ = r.shape
    c = CHUNK * SCAN_CHUNKS_PER_STEP
    n = s // c
    w = pairs * LANES
    fwd = pl.BlockSpec((1, c, w), lambda bi, p, t: (bi, t, p))
    bwd = pl.BlockSpec((1, c, w), lambda bi, p, t: (bi, n - 1 - t, p))
    return pl.pallas_call(
        functools.partial(_rwkv_scan_kernel, pairs=pairs),
        grid=(b, RWKV_WIDTH // w, n),
        in_specs=[fwd] * 7 + [bwd] * 7,
        out_specs=[fwd, bwd],
        out_shape=[jax.ShapeDtypeStruct((b, s, RWKV_WIDTH), BF16)] * 2,
        scratch_shapes=[pltpu.VMEM((2 * pairs, LANES, LANES), F32)],
        compiler_params=_cparams(("parallel", "parallel", "arbitrary")),
        name="rwkv_scan",
    )(r, k, v, kk, bb, cum_f, exc_f, r, k, v, kk, bb, cum_b, exc_b)


def _rwkv_post_kernel(of_ref, ob_ref, bonus_ref, g_ref, lnw_ref, lnb_ref, e_ref, et_ref, y_ref):
    e = e_ref[...]
    et = et_ref[...]
    o = of_ref[0].astype(F32) + ob_ref[0].astype(F32)
    mean = _seg_sum(o, e, et) * (1.0 / RWKV_HEAD)
    cen = o - mean
    var = _seg_sum(cen * cen, e, et) * (1.0 / RWKV_HEAD)
    y = cen * lax.rsqrt(var + RWKV_LN_EPS) * lnw_ref[...] + lnb_ref[...]
    y_ref[0] = ((y + bonus_ref[0].astype(F32)) * g_ref[0].astype(F32)).astype(y_ref.dtype)


def rwkv_post(o_f, o_b, bonus, g, ln_w, ln_b, seg_e, seg_et, tm):
    b, s, w = o_f.shape
    blk = pl.BlockSpec((1, tm, w), lambda bi, t: (bi, t, 0))
    full = lambda arr: pl.BlockSpec(arr.shape, lambda bi, t: (0,) * arr.ndim)
    params = [ln_w.reshape(1, w), ln_b.reshape(1, w), seg_e, seg_et]
    return pl.pallas_call(
        _rwkv_post_kernel,
        grid=(b, s // tm),
        in_specs=[blk] * 4 + [full(p) for p in params],
        out_specs=blk,
        out_shape=jax.ShapeDtypeStruct((b, s, w), BF16),
        compiler_params=_cparams(("parallel", "parallel")),
        name="rwkv_post",
    )(o_f, o_b, bonus, g, *params)


def _gdn_conv_kernel(z_ref, zp_ref, zn_ref, cw_ref, sc_ref, o_ref, *, l2norm):
    t = pl.program_id(1)
    tm = z_ref.shape[1]
    halo = zp_ref.shape[1]
    ext = jnp.concatenate([jnp.where(t == 0, 0.0, zp_ref[0].astype(F32)), z_ref[0].astype(F32),
                           jnp.where(t == pl.num_programs(1) - 1, 0.0, zn_ref[0].astype(F32))], axis=0)
    n = tm + 2 * halo
    pad = cw_ref.shape[0] // 2
    y = 0.0
    for j in range(cw_ref.shape[0]):
        shifted = ext if j == pad else pltpu.roll(ext, (pad - j) % n, axis=0)
        y = y + cw_ref[j:j + 1] * shifted[halo:halo + tm]
    y = _silu(y)
    if l2norm:
        for h in range(y.shape[1] // GDN_DK):
            sl = slice(h * GDN_DK, (h + 1) * GDN_DK)
            yh = y[:, sl]
            ss = jnp.sum(yh * yh, axis=-1, keepdims=True)
            o_ref[0, :, sl] = (yh * lax.rsqrt(ss + EPS) * sc_ref[0:1, sl]).astype(o_ref.dtype)
    else:
        o_ref[0] = y.astype(o_ref.dtype)


def gdn_conv(z, conv_w, scale, col_off, width, l2norm, tm, cw):
    b, s, _ = z.shape
    nt = s // tm
    halo = 8 * (4 // z.dtype.itemsize)
    hb = tm // halo
    cb = col_off // cw
    in_specs = [pl.BlockSpec((1, tm, cw), lambda bi, t, c: (bi, t, cb + c)),
                pl.BlockSpec((1, halo, cw), lambda bi, t, c: (bi, jnp.maximum(t * hb - 1, 0), cb + c)),
                pl.BlockSpec((1, halo, cw), lambda bi, t, c: (bi, jnp.minimum((t + 1) * hb, nt * hb - 1), cb + c)),
                pl.BlockSpec((conv_w.shape[0], cw), lambda bi, t, c: (0, cb + c)),
                pl.BlockSpec((1, cw), lambda bi, t, c: (0, c))]
    return pl.pallas_call(
        functools.partial(_gdn_conv_kernel, l2norm=l2norm),
        grid=(b, nt, width // cw),
        in_specs=in_specs,
        out_specs=pl.BlockSpec((1, tm, cw), lambda bi, t, c: (bi, t, c)),
        out_shape=jax.ShapeDtypeStruct((b, s, width), BF16),
        compiler_params=_cparams(("parallel", "parallel", "parallel")),
        name="gdn_conv_norm" if l2norm else "gdn_conv",
    )(z, z, z, conv_w, scale)


def _gdn_gates_kernel(zs_ref, na_ref, dtb_ref, o_ref):
    zs = zs_ref[0]
    tm = zs.shape[0]
    gg = na_ref[...] * _softplus(zs + dtb_ref[...])
    gam_f = _mask_dot(_order_mask(tm, False, False, CHUNK), gg)
    gam_b = _mask_dot(_order_mask(tm, True, False, CHUNK), gg)
    lane = _iota2(zs.shape, 1)
    o_ref[0] = jnp.where(lane < 2 * GDN_VHEADS, _sigmoid(zs), jnp.where(lane < 3 * GDN_VHEADS, gam_f, gam_b))


def gdn_gates(zs, neg_a, dtb, tm):
    b, s, w = zs.shape
    blk = pl.BlockSpec((1, tm, w), lambda bi, t: (bi, t, 0))
    row = pl.BlockSpec((1, w), lambda bi, t: (0, 0))
    return pl.pallas_call(
        _gdn_gates_kernel,
        grid=(b, s // tm),
        in_specs=[blk, row, row],
        out_specs=blk,
        out_shape=jax.ShapeDtypeStruct((b, s, w), F32),
        compiler_params=_cparams(("parallel", "parallel")),
        name="gdn_gates",
    )(zs, neg_a, dtb)


def _gdn_chunks(q, k, v2, beta2, gam2, revs, s_ref):
    nb, c, _ = q.shape
    n2 = 2 * c
    per_batch = lambda fn: jnp.stack([fn(r) for r in revs], axis=0)
    incl = per_batch(lambda r: _order_mask(n2, r, False, c))
    strict = per_batch(lambda r: _order_mask(n2, r, True, c))
    gam_c = jnp.broadcast_to(gam2, (nb, n2, n2))
    gam_r = jnp.swapaxes(gam_c, 1, 2)
    tot2 = jnp.stack([jnp.concatenate(
        [jnp.broadcast_to(gam2[j, h * c + (0 if r else c - 1)][None], (c, 1)) for h in range(2)], axis=0)
        for j, r in enumerate(revs)], axis=0)
    diff = gam_c - gam_r
    decay = jnp.exp(jnp.where(incl, diff, 0.0))
    dec_s = jnp.where(strict, decay, 0.0)
    dec_i = jnp.where(incl, decay, 0.0)
    k2 = jnp.concatenate([k, k], axis=1)
    q2 = jnp.concatenate([q, q], axis=1)
    gram = _bdot_nt(jnp.concatenate([k2, q2], axis=1), k2)
    a_mat = gram[:, :n2] * beta2 * dec_s
    qk = gram[:, n2:] * dec_i
    t_inv = _neumann_inverse(a_mat, negate=True)
    e_gam = jnp.exp(gam2)
    uw = _bdot(t_inv, jnp.concatenate([v2 * beta2, k2 * (beta2 * e_gam)], axis=2))
    u2 = uw[:, :, :GDN_DV]
    w2 = uw[:, :, GDN_DV:]
    qd2 = q2 * e_gam
    ke2 = k2 * jnp.exp(tot2 - gam2)
    dl2 = jnp.exp(tot2)
    ws, qs = [], []
    for e in range(2):
        rows = slice(e * c, (e + 1) * c)
        both = _bdot(jnp.concatenate([w2[:, rows], qd2[:, rows]], axis=1), s_ref[e])
        ws.append(both[:, :c])
        qs.append(both[:, c:])
    vnew2 = u2 - jnp.concatenate(ws, axis=1)
    o2 = jnp.concatenate(qs, axis=1) + _bdot(qk, vnew2)
    for e in range(2):
        rows = slice(e * c, (e + 1) * c)
        s_ref[e] = s_ref[e] * dl2[:, e * c:e * c + 1, :] + _bdot_tn(ke2[:, rows], vnew2[:, rows])
    return o2


def _gdn_scan_kernel(qf, kf, vf, gf, qb, kb, vb, gb, of_ref, ob_ref, s_ref, *, heads):
    @pl.when(pl.program_id(2) == 0)
    def _():
        s_ref[...] = jnp.zeros_like(s_ref)

    c = CHUNK
    sub_steps = qf.shape[1] // c
    lane = _iota2((c, LANES), 1)

    def column(gates, idx):
        return jnp.sum(jnp.where(lane == idx, gates, 0.0), axis=1, keepdims=True)

    for sub in range(sub_steps):
        rows_of = (slice(sub * c, (sub + 1) * c), slice((sub_steps - 1 - sub) * c, (sub_steps - sub) * c))
        qs, ks, vs, betas, gams, revs = [], [], [], [], [], []
        for d, (q_ref, k_ref, v_ref, g_ref) in enumerate(((qf, kf, vf, gf), (qb, kb, vb, gb))):
            rows = rows_of[d]
            gates = g_ref[0, rows, :]
            for i in range(heads):
                vh = 2 * (pl.program_id(1) * heads + i)
                ksl = slice(i * GDN_DK, (i + 1) * GDN_DK)
                qs.append(q_ref[0, rows, ksl].astype(F32))
                ks.append(k_ref[0, rows, ksl].astype(F32))
                vs.append(jnp.concatenate(
                    [v_ref[0, rows, (2 * i + e) * GDN_DV:(2 * i + e + 1) * GDN_DV].astype(F32) for e in range(2)], axis=0))
                betas.append(jnp.concatenate([column(gates, d * GDN_VHEADS + vh + e) for e in range(2)], axis=0))
                gams.append(jnp.concatenate([column(gates, (2 + d) * GDN_VHEADS + vh + e) for e in range(2)], axis=0))
                revs.append(d == 1)
        o2 = _gdn_chunks(jnp.stack(qs), jnp.stack(ks), jnp.stack(vs), jnp.stack(betas), jnp.stack(gams), revs, s_ref)
        for d, o_ref in enumerate((of_ref, ob_ref)):
            for i in range(heads):
                j = d * heads + i
                for e in range(2):
                    o_ref[0, rows_of[d], (2 * i + e) * GDN_DV:(2 * i + e + 1) * GDN_DV] = (
                        o2[j, e * c:(e + 1) * c].astype(o_ref.dtype))


def gdn_scan(qk, v, gates, heads):
    b, s, _ = v.shape
    c = CHUNK * SCAN_CHUNKS_PER_STEP
    n = s // c
    ng = GDN_KHEADS // heads

    def spec(w, off, rev):
        if rev:
            return pl.BlockSpec((1, c, w), lambda bi, h, t: (bi, n - 1 - t, off + h))
        return pl.BlockSpec((1, c, w), lambda bi, h, t: (bi, t, off + h))

    def gspec(rev):
        if rev:
            return pl.BlockSpec((1, c, LANES), lambda bi, h, t: (bi, n - 1 - t, 0))
        return pl.BlockSpec((1, c, LANES), lambda bi, h, t: (bi, t, 0))

    kw = heads * GDN_DK
    vw = heads * 2 * GDN_DV
    in_specs = [spec(kw, 0, False), spec(kw, ng, False), spec(vw, 0, False), gspec(False),
                spec(kw, 0, True), spec(kw, ng, True), spec(vw, 0, True), gspec(True)]
    return pl.pallas_call(
        functools.partial(_gdn_scan_kernel, heads=heads),
        grid=(b, ng, n),
        in_specs=in_specs,
        out_specs=[spec(vw, 0, False), spec(vw, 0, True)],
        out_shape=[jax.ShapeDtypeStruct((b, s, GDN_VW), BF16)] * 2,
        scratch_shapes=[pltpu.VMEM((2, 2 * heads, GDN_DK, GDN_DV), F32)],
        compiler_params=_cparams(("parallel", "parallel", "arbitrary")),
        name="gdn_scan",
    )(qk, qk, v, gates, qk, qk, v, gates)


def _gdn_post_kernel(of_ref, ob_ref, gate_ref, nw_ref, y_ref):
    nw = nw_ref[...]
    for h in range(of_ref.shape[2] // GDN_DV):
        sl = slice(h * GDN_DV, (h + 1) * GDN_DV)
        o = of_ref[0, :, sl].astype(F32) + ob_ref[0, :, sl].astype(F32)
        ms = jnp.mean(o * o, axis=-1, keepdims=True)
        y = o * lax.rsqrt(ms + EPS) * nw
        y_ref[0, :, sl] = (y * _silu(gate_ref[0, :, sl].astype(F32))).astype(y_ref.dtype)


def gdn_post(o_f, o_b, z, gate_off, norm_w, tm, cw):
    b, s, w = o_f.shape
    blk = pl.BlockSpec((1, tm, cw), lambda bi, t, c: (bi, t, c))
    gb = gate_off // cw
    return pl.pallas_call(
        _gdn_post_kernel,
        grid=(b, s // tm, w // cw),
        in_specs=[blk, blk, pl.BlockSpec((1, tm, cw), lambda bi, t, c: (bi, t, gb + c)),
                  pl.BlockSpec((1, GDN_DV), lambda bi, t, c: (0, 0))],
        out_specs=blk,
        out_shape=jax.ShapeDtypeStruct((b, s, w), BF16),
        compiler_params=_cparams(("parallel", "parallel", "parallel")),
        name="gdn_post",
    )(o_f, o_b, z, norm_w.reshape(1, GDN_DV))


def _moe_router_kernel(x_ref, g_ref, w_ref, b_ref, route_ref, counts_ref):
    @pl.when(pl.program_id(0) == 0)
    def _():
        counts_ref[...] = jnp.zeros_like(counts_ref)

    x = x_ref[...]
    tm = x.shape[0]
    ms = jnp.mean(x * x, axis=-1, keepdims=True)
    h = x * lax.rsqrt(ms + EPS) * g_ref[...]
    lane_i = _iota2((tm, LANES), 1)
    lane = lane_i.astype(F32)
    lane_grp = lax.shift_right_logical(lane_i, 3).astype(F32)
    neg = -jnp.inf
    logits = _dot3(h, w_ref[...]) + b_ref[...]
    gl = jnp.where((lane_i >= N_EXPERTS) & (lane_i < N_EXPERTS + N_GROUPS), logits, neg)
    gmax = jnp.max(gl, axis=-1, keepdims=True)
    gidx = jnp.min(jnp.where(gl == gmax, lane, float(LANES)), axis=-1, keepdims=True) - float(N_EXPERTS)
    grp_w = 1.0 / jnp.sum(jnp.exp(gl - gmax), axis=-1, keepdims=True)
    sel = jnp.where((lane_i < N_EXPERTS) & (lane_grp == gidx), logits, neg)
    m1 = jnp.max(sel, axis=-1, keepdims=True)
    i1 = jnp.min(jnp.where(sel == m1, lane, float(LANES)), axis=-1, keepdims=True)
    sel2 = jnp.where(lane == i1, neg, sel)
    m2 = jnp.max(sel2, axis=-1, keepdims=True)
    i2 = jnp.min(jnp.where(sel2 == m2, lane, float(LANES)), axis=-1, keepdims=True)
    e2 = jnp.exp(m2 - m1)
    w1 = grp_w / (1.0 + e2)
    w2 = grp_w * e2 / (1.0 + e2)
    hits = jnp.where((lane == i1) | (lane == i2), 1.0, 0.0)
    before = counts_ref[...] + _bdot(_order_mask(tm, False, True, tm), hits)
    r1 = jnp.sum(jnp.where(lane == i1, before, 0.0), axis=-1, keepdims=True)
    r2 = jnp.sum(jnp.where(lane == i2, before, 0.0), axis=-1, keepdims=True)
    counts_ref[...] += jnp.sum(hits, axis=0, keepdims=True)
    route_ref[...] = jnp.where(lane_i == 0, i1, jnp.where(lane_i == 1, i2, jnp.where(
        lane_i == 2, w1, jnp.where(lane_i == 3, w2, jnp.where(lane_i == 4, r1, jnp.where(lane_i == 5, r2, 0.0))))))


def moe_router(x, g, w_group, b_group, w_router, b_router, tm):
    m, d = x.shape
    tm = min(tm, m)
    pad = lambda w: jnp.pad(w, ((0, 0), (0, LANES - w.shape[1])))
    full = lambda arr: pl.BlockSpec(arr.shape, lambda i: (0, 0))
    params = [g.reshape(1, d), pad(jnp.concatenate([w_router, w_group], axis=1)),
              pad(jnp.concatenate([b_router, b_group]).reshape(1, -1))]
    return pl.pallas_call(
        _moe_router_kernel,
        grid=(m // tm,),
        in_specs=[pl.BlockSpec((tm, d), lambda i: (i, 0))] + [full(p) for p in params],
        out_specs=[pl.BlockSpec((tm, LANES), lambda i: (i, 0)), pl.BlockSpec((1, LANES), lambda i: (0, 0))],
        out_shape=[jax.ShapeDtypeStruct((m, LANES), F32), jax.ShapeDtypeStruct((1, LANES), F32)],
        compiler_params=_cparams(("arbitrary",)),
        name="moe_router",
    )(x, *params)


def _route_tables(route, counts, tm, n_tiles):
    counts = counts[0, :N_EXPERTS].astype(jnp.int32)
    padded = ((counts + tm - 1) // tm) * tm
    ends = jnp.cumsum(padded)
    starts = ends - padded
    experts = route[:, 0:2].astype(jnp.int32)
    pos = (starts[experts] + route[:, 4:6].astype(jnp.int32)).reshape(-1)
    tile_start = jnp.arange(n_tiles, dtype=jnp.int32) * tm
    tile_e = jnp.minimum(jnp.sum(tile_start[:, None] >= ends[None, :], axis=1), N_EXPERTS - 1).astype(jnp.int32)
    n_valid = (ends[-1:] // tm).astype(jnp.int32)
    fill = jnp.stack([jnp.maximum(ends - tm, 0), (padded > 0).astype(jnp.int32)]).astype(jnp.int32)
    return pos, tile_e, n_valid, fill


def _moe_dispatch_kernel(pos_ref, nvalid_ref, fill_ref, x_ref, g_ref, hs_hbm, hbuf, zbuf, sem_rows, sem_fill, *,
                         n_tiles):
    i = pl.program_id(0)
    tm = x_ref.shape[0]
    row_tile = zbuf.shape[0]

    @pl.when(i == 0)
    def _():
        zbuf[...] = jnp.zeros_like(zbuf)
        fills = [(fill_ref[1, e] > 0, fill_ref[0, e]) for e in range(N_EXPERTS)]
        fills += [(nvalid_ref[0] + k < n_tiles, (nvalid_ref[0] + k) * row_tile) for k in range(N_EXPERTS)]
        fills = [(cond, pl.multiple_of(start, row_tile)) for cond, start in fills]
        for cond, start in fills:
            @pl.when(cond)
            def _(start=start):
                pltpu.make_async_copy(zbuf, hs_hbm.at[pl.ds(start, row_tile)], sem_fill).start()
        for cond, start in fills:
            @pl.when(cond)
            def _(start=start):
                pltpu.make_async_copy(zbuf, hs_hbm.at[pl.ds(start, row_tile)], sem_fill).wait()

    buf = i % 2
    x = x_ref[...]
    ms = jnp.mean(x * x, axis=-1, keepdims=True)
    hbuf[buf] = x * lax.rsqrt(ms + EPS) * g_ref[...]

    def issue(r, carry):
        base = 2 * (i * tm + r)
        for slot in range(2):
            pltpu.make_async_copy(hbuf.at[buf, pl.ds(r, 1)], hs_hbm.at[pl.ds(pos_ref[base + slot], 1)],
                                  sem_rows.at[buf]).start()
        return carry

    lax.fori_loop(0, tm, issue, 0, unroll=8)

    def drain(b):
        for _ in range(2):
            pltpu.make_async_copy(hbuf.at[b], hs_hbm.at[pl.ds(0, tm)], sem_rows.at[b]).wait()

    @pl.when(i > 0)
    def _():
        drain(1 - buf)

    @pl.when(i == pl.num_programs(0) - 1)
    def _():
        drain(buf)


def moe_dispatch(x, g, pos, n_valid, fill, n_tiles, tm):
    m, d = x.shape
    tm_x = min(256, m)
    return pl.pallas_call(
        functools.partial(_moe_dispatch_kernel, n_tiles=n_tiles),
        grid_spec=pltpu.PrefetchScalarGridSpec(
            num_scalar_prefetch=3,
            grid=(m // tm_x,),
            in_specs=[pl.BlockSpec((tm_x, d), lambda i, p, nv, fl: (i, 0)),
                      pl.BlockSpec((1, d), lambda i, p, nv, fl: (0, 0))],
            out_specs=pl.BlockSpec(memory_space=pl.ANY),
            scratch_shapes=[pltpu.VMEM((2, tm_x, d), F32), pltpu.VMEM((tm, d), F32),
                            pltpu.SemaphoreType.DMA((2,)), pltpu.SemaphoreType.DMA(())]),
        out_shape=jax.ShapeDtypeStruct((n_tiles * tm, d), F32),
        compiler_params=_cparams(("arbitrary",)),
        name="moe_dispatch",
    )(pos, n_valid, fill, x, g.reshape(1, d))


def _moe_ffn_kernel(tile_e_ref, nvalid_ref, h_ref, wg_ref, wu_ref, wd_ref, y_ref, wg_b, wu_b, wd_b):
    i = pl.program_id(0)

    @pl.when((i == 0) | (tile_e_ref[i] != tile_e_ref[jnp.maximum(i - 1, 0)]))
    def _():
        wg_b[...] = wg_ref[0, 0].astype(BF16)
        wu_b[...] = wu_ref[0, 0].astype(BF16)
        wd_b[...] = wd_ref[0, 0].astype(BF16)

    @pl.when(i < nvalid_ref[0])
    def _():
        x = h_ref[...].astype(BF16)
        hid = _silu(_dot(x, wg_b[...])) * _dot(x, wu_b[...])
        y_ref[...] = _dot(hid.astype(BF16), wd_b[...])

    @pl.when(i >= nvalid_ref[0])
    def _():
        y_ref[...] = jnp.zeros_like(y_ref)


def moe_ffn(h_sorted, tile_e, n_valid, w_gate, w_up, w_down, layer, tm):
    _, d = h_sorted.shape
    n_tiles = tile_e.shape[0]
    ff = w_gate.shape[-1]
    wspec = lambda shape: pl.BlockSpec((1, 1) + shape, lambda i, te, nv: (layer, te[i], 0, 0))
    return pl.pallas_call(
        _moe_ffn_kernel,
        grid_spec=pltpu.PrefetchScalarGridSpec(
            num_scalar_prefetch=2,
            grid=(n_tiles,),
            in_specs=[pl.BlockSpec((tm, d), lambda i, te, nv: (i, 0)), wspec((d, ff)), wspec((d, ff)), wspec((ff, d))],
            out_specs=pl.BlockSpec((tm, d), lambda i, te, nv: (i, 0)),
            scratch_shapes=[pltpu.VMEM((d, ff), BF16), pltpu.VMEM((d, ff), BF16), pltpu.VMEM((ff, d), BF16)]),
        out_shape=jax.ShapeDtypeStruct((n_tiles * tm, d), F32),
        compiler_params=_cparams(("arbitrary",)),
        name="moe_ffn",
    )(tile_e, n_valid, h_sorted, w_gate, w_up, w_down)


def _moe_combine_kernel(pos_ref, x_ref, route_ref, y_hbm, g_ref, o_ref, ybuf, sems, *, final_norm):
    i = pl.program_id(0)
    tm = x_ref.shape[0]

    def gather(tile, buf):
        def issue(r, carry):
            base = 2 * (tile * tm + r)
            for slot in range(2):
                pltpu.make_async_copy(y_hbm.at[pl.ds(pos_ref[base + slot], 1)], ybuf.at[buf, slot, pl.ds(r, 1)],
                                      sems.at[buf]).start()
            return carry

        lax.fori_loop(0, tm, issue, 0, unroll=8)

    @pl.when(i == 0)
    def _():
        gather(0, 0)

    buf = i % 2

    @pl.when(i + 1 < pl.num_programs(0))
    def _():
        gather(i + 1, 1 - buf)

    for slot in range(2):
        pltpu.make_async_copy(y_hbm.at[pl.ds(0, tm)], ybuf.at[buf, slot], sems.at[buf]).wait()
    route = route_ref[...]
    o = x_ref[...] + route[:, 2:3] * ybuf[buf, 0] + route[:, 3:4] * ybuf[buf, 1]
    if final_norm:
        ms = jnp.mean(o * o, axis=-1, keepdims=True)
        o = o * lax.rsqrt(ms + EPS) * g_ref[...]
    o_ref[...] = o


def moe_combine(x, route, y_sorted, pos, norm_g, tm):
    m, d = x.shape
    tm = min(tm, m)
    final_norm = norm_g is not None
    g = (norm_g if final_norm else jnp.ones((d,), F32)).reshape(1, d)
    return pl.pallas_call(
        functools.partial(_moe_combine_kernel, final_norm=final_norm),
        grid_spec=pltpu.PrefetchScalarGridSpec(
            num_scalar_prefetch=1,
            grid=(m // tm,),
            in_specs=[pl.BlockSpec((tm, d), lambda i, p: (i, 0)), pl.BlockSpec((tm, LANES), lambda i, p: (i, 0)),
                      pl.BlockSpec(memory_space=pl.ANY), pl.BlockSpec((1, d), lambda i, p: (0, 0))],
            out_specs=pl.BlockSpec((tm, d), lambda i, p: (i, 0)),
            scratch_shapes=[pltpu.VMEM((2, 2, tm, d), F32), pltpu.SemaphoreType.DMA((2,))]),
        out_shape=jax.ShapeDtypeStruct((m, d), F32),
        compiler_params=_cparams(("arbitrary",)),
        name="moe_combine",
    )(pos, x, route, y_sorted, g)


def _pad_rows(w, rows):
    return jnp.pad(w, ((0, rows - w.shape[0]), (0, 0)))


def _even_layer(x, norm_g, w_in, gla_w_alpha, gla_b_alpha, gla_norm, mu, w0, w2, a0, a2, g2,
                k_k, k_a, r_k, ln_w, ln_b, w_out):
    b, s, d = x.shape
    gla_cols, rw = w_in[:, :3104], w_in[:, 3104:]
    mu_g = lambda lo, hi, width: jnp.pad(mu[:, lo:hi], ((0, 0), (0, width - (hi - lo))))
    pad_c = lambda w, width: jnp.pad(w, ((0, 0), (0, width - w.shape[1])))
    w_cat = jnp.concatenate([
        gla_cols[:, 0:3072],
        rw[:, 0:3072], rw[:, 3360:3616],
        pad_c(rw[:, 3072:3168], LANES), pad_c(rw[:, 3168:3264], LANES), pad_c(rw[:, 3264:3360], LANES),
        pad_c(gla_cols[:, 3072:3104], LANES)], axis=1).astype(BF16)
    mu_cat = jnp.concatenate([
        mu[:, 0:3072], mu[:, 3360:3616], mu_g(3072, 3168, LANES), mu_g(3168, 3264, LANES),
        mu_g(3264, 3360, LANES)], axis=1)
    z = norm_matmul(x.reshape(b * s, d), norm_g, w_cat, 1024, 1152).reshape(b, s, EVEN_PAD)

    wa = jnp.stack([_pad_rows(gla_w_alpha[0], LANES),
                    jnp.pad(gla_w_alpha[1], ((GLA_LOWRANK, LANES - 2 * GLA_LOWRANK), (0, 0)))])
    gla_f, gla_b = gla_scan(z, wa, gla_b_alpha)
    y_gla = gla_post(gla_f, gla_b, z, gla_norm, 256)

    head_of_lane = jnp.arange(RWKV_WIDTH) // RWKV_HEAD
    seg_e = (head_of_lane[:, None] == jnp.arange(LANES)[None, :]).astype(F32)
    seg_et = seg_e.T
    row = lambda p: p.reshape(1, RWKV_WIDTH)
    r, k, v, kk, bb, cum_f, exc_f, cum_b, exc_b, g, bonus = rwkv_prep(
        z, mu_cat, w0, _pad_rows(w2[0], LANES), _pad_rows(w2[1], LANES), row(a0), _pad_rows(a2, LANES), g2,
        row(k_k), row(k_a), row(r_k), seg_e, seg_et, 256)
    rw_f, rw_b = rwkv_scan(r, k, v, kk, bb, cum_f, exc_f, cum_b, exc_b, RWKV_PAIRS_PER_STEP)
    y_rwkv = rwkv_post(rw_f, rw_b, bonus, g, ln_w, ln_b, seg_e, seg_et, 256)

    w_out = w_out.astype(BF16)
    out = proj_residual([y_gla.reshape(b * s, GLA_WIDTH), y_rwkv.reshape(b * s, RWKV_WIDTH)],
                        [w_out[:GLA_WIDTH], w_out[GLA_WIDTH:]], x.reshape(b * s, d), 1024, 1024)
    return out.reshape(b, s, d)


def _odd_layer(x, norm_g, w_in, conv_w, a_log, dt_bias, norm_w, w_out):
    b, s, d = x.shape
    x2 = x.reshape(b * s, d)
    main = GDN_QKV + GDN_VW
    z = norm_matmul(x2, norm_g, w_in[:, :main].astype(BF16), 1024, 2048, BF16).reshape(b, s, main)
    zs = norm_matmul(x2, norm_g, w_in[:, main:].astype(BF16), 512, LANES).reshape(b, s, LANES)
    scale = jnp.concatenate([jnp.full((1, GDN_KW), GDN_DK ** -0.5, F32), jnp.ones((1, GDN_KW), F32)], axis=1)
    qk = gdn_conv(z, conv_w, scale, 0, 2 * GDN_KW, True, 256, 1024)
    v = gdn_conv(z, conv_w, scale, 2 * GDN_KW, GDN_VW, False, 256, 1024)
    zero = jnp.zeros((2 * GDN_VHEADS,), F32)
    neg_a = jnp.concatenate([zero, -jnp.exp(a_log.reshape(-1))]).reshape(1, LANES)
    dtb = jnp.concatenate([zero, dt_bias.reshape(-1)]).reshape(1, LANES)
    gates = gdn_gates(zs, neg_a, dtb, 256)
    o_f, o_b = gdn_scan(qk, v, gates, GDN_HEADS_PER_STEP)
    y = gdn_post(o_f, o_b, z, GDN_QKV, norm_w, 256, 1024)
    out = proj_residual([y.reshape(b * s, GDN_VW)], [w_out.astype(BF16)], x2, 1024, 1024)
    return out.reshape(b, s, d)


def _moe_layer(x, norm_g, w_group, b_group, w_router, b_router, w_gate, w_up, w_down, layer, final_g):
    b, s, d = x.shape
    x2 = x.reshape(b * s, d)
    route, counts = moe_router(x2, norm_g, w_group, b_group, w_router, b_router, 256)
    n_tiles = -(-2 * b * s // MOE_ROW_TILE) + N_EXPERTS
    pos, tile_e, n_valid, fill = _route_tables(route, counts, MOE_ROW_TILE, n_tiles)
    h_sorted = moe_dispatch(x2, norm_g, pos, n_valid, fill, n_tiles, MOE_ROW_TILE)
    y_sorted = moe_ffn(h_sorted, tile_e, n_valid, w_gate, w_up, w_down, layer, MOE_ROW_TILE)
    out = moe_combine(x2, route, y_sorted, pos, final_g, 256)
    return out.reshape(b, s, d)


def kernel(x, norm_mix, norm_ffn, norm_final, ev_w_in, ev_gla_w_alpha, ev_gla_b_alpha, ev_gla_norm, ev_rwkv_mu, ev_rwkv_w0, ev_rwkv_w2, ev_rwkv_a0, ev_rwkv_a2, ev_rwkv_g2, ev_rwkv_k_k, ev_rwkv_k_a, ev_rwkv_r_k, ev_rwkv_ln_w, ev_rwkv_ln_b, ev_w_out, od_w_in, od_conv, od_a_log, od_dt_bias, od_norm, od_w_out, moe_w_group, moe_b_group, moe_w_router, moe_b_router, moe_w_gate, moe_w_up, moe_w_down):
    depth = norm_mix.shape[0]
    for i in range(depth):
        j = i // 2
        if i % 2 == 0:
            x = _even_layer(x, norm_mix[i], ev_w_in[j], ev_gla_w_alpha[j], ev_gla_b_alpha[j], ev_gla_norm[j],
                            ev_rwkv_mu[j], ev_rwkv_w0[j], ev_rwkv_w2[j], ev_rwkv_a0[j], ev_rwkv_a2[j],
                            ev_rwkv_g2[j], ev_rwkv_k_k[j], ev_rwkv_k_a[j], ev_rwkv_r_k[j],
                            ev_rwkv_ln_w[j], ev_rwkv_ln_b[j], ev_w_out[j])
        else:
            x = _odd_layer(x, norm_mix[i], od_w_in[j], od_conv[j], od_a_log[j], od_dt_bias[j],
                           od_norm[j], od_w_out[j])
        x = _moe_layer(x, norm_ffn[i], moe_w_group[i], moe_b_group[i], moe_w_router[i], moe_b_router[i],
                       moe_w_gate, moe_w_up, moe_w_down, i, norm_final if i == depth - 1 else None)
    return x
```

```python
import functools
import math

import jax
import jax.numpy as jnp
from jax import lax
from jax.experimental import pallas as pl
from jax.experimental.pallas import tpu as pltpu

F32 = jnp.float32
BF16 = jnp.bfloat16
HI = lax.Precision.HIGHEST

EPS = 1e-6
CHUNK = 64
LANES = 128
VMEM_LIMIT = 56 * 1024 * 1024

D_MODEL = 2048
GLA_HEADS = 4
GLA_DK = 128
GLA_DV = 256
GLA_KW = 512
GLA_WIDTH = 1024
GLA_LOWRANK = 16
GLA_TAU = 16.0
RWKV_HEAD = 64
RWKV_WIDTH = 1024
RWKV_HEADS = 16
DECAY_LORA = 96
ICLR_LORA = 96
GATE_LORA = 256
RWKV_LN_EPS = 64e-5
GDN_DK = 128
GDN_DV = 128
GDN_KHEADS = 16
GDN_VHEADS = 32
GDN_KW = 2048
GDN_VW = 4096
GDN_QKV = 8192
SCAN_CHUNKS_PER_STEP = 4
RWKV_PAIRS_PER_STEP = 8
GDN_HEADS_PER_STEP = 8
MOE_ROW_TILE = 256
N_GROUPS = 4
EXPERTS_PER_GROUP = 8
N_EXPERTS = 32
EXPERT_FF = 512

GQ_OFF, GK_OFF, GV_OFF, GG_OFF = 0, 512, 1024, 2048
R_OFF, K_OFF, V_OFF = 3072, 4096, 5120
GLR_OFF, WLF_OFF, WLB_OFF, ALR_OFF, AL_OFF = 6144, 6400, 6528, 6656, 6784
RWKV_MAIN = 3072
RWKV_SMALL = 768
EVEN_PAD = 6912


def _cparams(sem):
    return pltpu.CompilerParams(dimension_semantics=sem, vmem_limit_bytes=VMEM_LIMIT)


def _mm(a, b, ca, cb, precision):
    if a.ndim == 3:
        dims = (((ca + 1,), (cb + 1,)), ((0,), (0,)))
    else:
        dims = (((ca,), (cb,)), ((), ()))
    return lax.dot_general(a, b, dims, preferred_element_type=F32, precision=precision)


def _dot(a, b, precision=None):
    return _mm(a, b, 1, 0, precision)


def _dot_nt(a, b, precision=None):
    return _mm(a, b, 1, 1, precision)


def _dot_tn(a, b, precision=None):
    return _mm(a, b, 0, 0, precision)


def _bf16_pieces(x, n):
    pieces = []
    for _ in range(n - 1):
        p = x.astype(BF16)
        pieces.append(p)
        x = x - p.astype(F32)
    pieces.append(x.astype(BF16))
    return pieces


def _mask_dot(mask, x, pieces=3):
    mb = mask.astype(BF16)
    return sum(_dot(mb, p) for p in _bf16_pieces(x, pieces))


def _dot_mask(x, mask, pieces=2):
    mb = mask.astype(BF16)
    return sum(_dot(p, mb) for p in _bf16_pieces(x, pieces))


def _dot3(a, b):
    ah, al = _bf16_pieces(a, 2)
    bh, bl = _bf16_pieces(b, 2)
    return _dot(ah, bh) + _dot(ah, bl) + _dot(al, bh)


def _bdot(a, b):
    return _dot(a.astype(BF16), b.astype(BF16))


def _bdot_nt(a, b):
    return _dot_nt(a.astype(BF16), b.astype(BF16))


def _bdot_tn(a, b):
    return _dot_tn(a.astype(BF16), b.astype(BF16))


def _sigmoid(x):
    return 1.0 / (1.0 + jnp.exp(-x))


def _silu(x):
    return x * _sigmoid(x)


def _softplus(x):
    return jnp.maximum(x, 0.0) + jnp.log(1.0 + jnp.exp(-jnp.abs(x)))


def _iota2(shape, dim):
    return lax.broadcasted_iota(jnp.int32, shape, dim)


def _order_mask(n, rev, strict, block):
    i = _iota2((n, n), 0)
    j = _iota2((n, n), 1)
    if rev:
        m = (j > i) if strict else (j >= i)
    else:
        m = (j < i) if strict else (j <= i)
    if block < n:
        sh = block.bit_length() - 1
        m = m & (lax.shift_right_logical(i, sh) == lax.shift_right_logical(j, sh))
    return m


def _neumann_inverse(nmat, negate=False):
    n = nmat.shape[-1]
    eye = (_iota2((n, n), 0) == _iota2((n, n), 1)).astype(F32)
    t = eye - nmat if negate else eye + nmat
    nb = nmat.astype(BF16)
    p = _dot(nb, nb)
    for step in range(5):
        pb = p.astype(BF16)
        if step < 4:
            both = _dot(pb, jnp.concatenate([pb, t.astype(BF16)], axis=-1))
            p = both[..., :n]
            t = t + both[..., n:]
        else:
            t = t + _dot(pb, t.astype(BF16))
    return t


def _norm_matmul_kernel(x_ref, g_ref, w_ref, o_ref, h_ref):
    @pl.when(pl.program_id(1) == 0)
    def _():
        x = x_ref[...]
        ms = jnp.mean(x * x, axis=-1, keepdims=True)
        h_ref[...] = (x * lax.rsqrt(ms + EPS) * g_ref[...]).astype(BF16)

    o_ref[...] = _dot(h_ref[...], w_ref[...]).astype(o_ref.dtype)


def norm_matmul(x, g, w, tm, tn, out_dtype=F32):
    m, d = x.shape
    tm = min(tm, m)
    n = w.shape[1]
    return pl.pallas_call(
        _norm_matmul_kernel,
        grid=(m // tm, n // tn),
        in_specs=[pl.BlockSpec((tm, d), lambda i, j: (i, 0)),
                  pl.BlockSpec((1, d), lambda i, j: (0, 0)),
                  pl.BlockSpec((d, tn), lambda i, j: (0, j))],
        out_specs=pl.BlockSpec((tm, tn), lambda i, j: (i, j)),
        out_shape=jax.ShapeDtypeStruct((m, n), out_dtype),
        scratch_shapes=[pltpu.VMEM((tm, d), BF16)],
        compiler_params=_cparams(("parallel", "arbitrary")),
        name="norm_matmul",
    )(x, g.reshape(1, d), w)


def _proj_residual_kernel(*refs, n_lhs):
    x_ref = refs[2 * n_lhs]
    o_ref = refs[2 * n_lhs + 1]
    acc = x_ref[...]
    for t in range(n_lhs):
        acc = acc + _dot(refs[t][...], refs[n_lhs + t][...])
    o_ref[...] = acc


def proj_residual(ys, ws, x, tm, tn):
    m, n = x.shape
    tm = min(tm, m)
    n_lhs = len(ys)
    in_specs = [pl.BlockSpec((tm, y.shape[1]), lambda i, j: (i, 0)) for y in ys]
    in_specs += [pl.BlockSpec((w.shape[0], tn), lambda i, j: (0, j)) for w in ws]
    in_specs += [pl.BlockSpec((tm, tn), lambda i, j: (i, j))]
    return pl.pallas_call(
        functools.partial(_proj_residual_kernel, n_lhs=n_lhs),
        grid=(m // tm, n // tn),
        in_specs=in_specs,
        out_specs=pl.BlockSpec((tm, tn), lambda i, j: (i, j)),
        out_shape=jax.ShapeDtypeStruct((m, n), F32),
        compiler_params=_cparams(("parallel", "arbitrary")),
        name="proj_residual",
    )(*ys, *ws, x)


def _final_norm_kernel(x_ref, g_ref, o_ref):
    x = x_ref[...]
    ms = jnp.mean(x * x, axis=-1, keepdims=True)
    o_ref[...] = x * lax.rsqrt(ms + EPS) * g_ref[...]


def final_norm(x, g, tm):
    m, d = x.shape
    tm = min(tm, m)
    return pl.pallas_call(
        _final_norm_kernel,
        grid=(m // tm,),
        in_specs=[pl.BlockSpec((tm, d), lambda i: (i, 0)), pl.BlockSpec((1, d), lambda i: (0, 0))],
        out_specs=pl.BlockSpec((tm, d), lambda i: (i, 0)),
        out_shape=jax.ShapeDtypeStruct((m, d), F32),
        compiler_params=_cparams(("parallel",)),
        name="final_norm",
    )(x, g.reshape(1, d))


def _gla_chunks(q, k, v, cum, revs, s_ref):
    c = q.shape[1]
    tot = jnp.stack([cum[j, (0 if rev else c - 1)][None] for j, rev in enumerate(revs)], axis=0)
    incl = jnp.stack([_order_mask(c, rev, False, c) for rev in revs], axis=0)
    q_dec = q * ((GLA_DK ** -0.5) * jnp.exp(cum))
    k_dec = k * jnp.exp(-cum)
    k_end = k * jnp.exp(tot - cum)
    scores = jnp.where(incl, _bdot_nt(q_dec, k_dec), 0.0)
    state = s_ref[...]
    o = _bdot(scores, v) + _bdot_nt(q_dec, state)
    s_ref[...] = jnp.exp(tot) * state + _bdot_tn(v, k_end)
    return o


def _gla_kernel(qf, kf, vf, alf, qb, kb, vb, alb, wa_ref, ba_ref, of_ref, ob_ref, s_ref):
    @pl.when(pl.program_id(1) == 0)
    def _():
        s_ref[...] = jnp.zeros_like(s_ref)

    c = CHUNK
    rows_all = qf.shape[1]
    sub_steps = rows_all // c
    cum_of = []
    for d, al_ref in enumerate((alf, alb)):
        pre = _dot3(al_ref[0], wa_ref[d]) + ba_ref[d:d + 1]
        log_a = (jnp.minimum(pre, 0.0) - jnp.log(1.0 + jnp.exp(-jnp.abs(pre)))) * (1.0 / GLA_TAU)
        cum_of.append(_mask_dot(_order_mask(rows_all, d == 1, False, c), log_a))
    for sub in range(sub_steps):
        rows_of = (slice(sub * c, (sub + 1) * c), slice((sub_steps - 1 - sub) * c, (sub_steps - sub) * c))
        qs, ks, vs, cums, revs = [], [], [], [], []
        for d, (q_ref, k_ref, v_ref) in enumerate(((qf, kf, vf), (qb, kb, vb))):
            rows = rows_of[d]
            for h in range(GLA_HEADS):
                ksl = slice(h * GLA_DK, (h + 1) * GLA_DK)
                qs.append(q_ref[0, rows, ksl])
                ks.append(k_ref[0, rows, ksl])
                vs.append(v_ref[0, rows, h * GLA_DV:(h + 1) * GLA_DV])
                cums.append(cum_of[d][rows, ksl])
                revs.append(d == 1)
        o = _gla_chunks(jnp.stack(qs), jnp.stack(ks), jnp.stack(vs), jnp.stack(cums), revs, s_ref)
        for d, o_ref in enumerate((of_ref, ob_ref)):
            for h in range(GLA_HEADS):
                o_ref[0, rows_of[d], h * GLA_DV:(h + 1) * GLA_DV] = o[d * GLA_HEADS + h].astype(o_ref.dtype)


def gla_scan(z, wa, ba):
    b, s, _ = z.shape
    c = CHUNK * SCAN_CHUNKS_PER_STEP
    n = s // c

    def fwd(off, w):
        return pl.BlockSpec((1, c, w), lambda bi, t: (bi, t, off // w))

    def bwd(off, w):
        return pl.BlockSpec((1, c, w), lambda bi, t: (bi, n - 1 - t, off // w))

    in_specs = [fwd(GQ_OFF, GLA_KW), fwd(GK_OFF, GLA_KW), fwd(GV_OFF, GLA_WIDTH), fwd(AL_OFF, LANES),
                bwd(GQ_OFF, GLA_KW), bwd(GK_OFF, GLA_KW), bwd(GV_OFF, GLA_WIDTH), bwd(AL_OFF, LANES),
                pl.BlockSpec(wa.shape, lambda bi, t: (0, 0, 0)), pl.BlockSpec(ba.shape, lambda bi, t: (0, 0))]
    out_specs = [pl.BlockSpec((1, c, GLA_WIDTH), lambda bi, t: (bi, t, 0)),
                 pl.BlockSpec((1, c, GLA_WIDTH), lambda bi, t: (bi, n - 1 - t, 0))]
    return pl.pallas_call(
        _gla_kernel,
        grid=(b, n),
        in_specs=in_specs,
        out_specs=out_specs,
        out_shape=[jax.ShapeDtypeStruct((b, s, GLA_WIDTH), BF16)] * 2,
        scratch_shapes=[pltpu.VMEM((2 * GLA_HEADS, GLA_DV, GLA_DK), F32)],
        compiler_params=_cparams(("parallel", "arbitrary")),
        name="gla_scan",
    )(z, z, z, z, z, z, z, z, wa, ba)


def _gla_post_kernel(of_ref, ob_ref, gate_ref, nw_ref, y_ref):
    o = of_ref[0].astype(F32) + ob_ref[0].astype(F32)
    ms = jnp.mean(o * o, axis=-1, keepdims=True)
    y = o * lax.rsqrt(ms + EPS) * nw_ref[...]
    y_ref[0] = (y * _silu(gate_ref[0])).astype(y_ref.dtype)


def gla_post(o_f, o_b, z, norm_w, tm):
    b, s, w = o_f.shape
    blk = pl.BlockSpec((1, tm, GLA_DV), lambda bi, t, h: (bi, t, h))
    return pl.pallas_call(
        _gla_post_kernel,
        grid=(b, s // tm, GLA_HEADS),
        in_specs=[blk, blk, pl.BlockSpec((1, tm, GLA_DV), lambda bi, t, h: (bi, t, GG_OFF // GLA_DV + h)),
                  pl.BlockSpec((1, GLA_DV), lambda bi, t, h: (0, 0))],
        out_specs=blk,
        out_shape=jax.ShapeDtypeStruct((b, s, w), BF16),
        compiler_params=_cparams(("parallel", "parallel", "parallel")),
        name="gla_post",
    )(o_f, o_b, z, norm_w.reshape(1, GLA_DV))


def _seg_sum(x, e, et):
    return _dot_mask(_dot_mask(x, e), et)


def _rwkv_prep_kernel(z_ref, zp_ref, zn_ref, y_ref, yp_ref, yn_ref, mu_ref, w0_ref, w2f_ref, w2b_ref, a0_ref, a2_ref, g2_ref,
                      kk_ref, ka_ref, rk_ref, e_ref, et_ref,
                      r_out, k_out, v_out, kk_out, b_out, cumf_out, excf_out, cumb_out, excb_out, g_out, bonus_out):
    t = pl.program_id(1)
    tm = z_ref.shape[1]
    row = _iota2((tm, 1), 0)
    first = t == 0
    last = t == pl.num_programs(1) - 1

    def shifted(lo, hi):
        cur, prv, nxt, base = (z_ref, zp_ref, zn_ref, R_OFF) if lo < GLR_OFF else (y_ref, yp_ref, yn_ref, GLR_OFF)
        z = cur[0, :, lo - base:hi - base]
        prev_row = jnp.where(first, 0.0, prv[0, 7:8, lo - base:hi - base])
        next_row = jnp.where(last, 0.0, nxt[0, 0:1, lo - base:hi - base])
        zprev = jnp.where(row == 0, prev_row, pltpu.roll(z, 1, axis=0))
        znext = jnp.where(row == tm - 1, next_row, pltpu.roll(z, tm - 1, axis=0))
        mu = mu_ref[:, lo - R_OFF:hi - R_OFF]
        return z + mu[0:1] * (zprev - z) + mu[1:2] * (znext - z)

    e = e_ref[...]
    et = et_ref[...]
    a = _sigmoid(a0_ref[...] + _bdot(shifted(ALR_OFF, ALR_OFF + LANES), a2_ref[...]))
    k = shifted(K_OFF, K_OFF + RWKV_WIDTH)
    kk_raw = k * kk_ref[...]
    kk = kk_raw * lax.rsqrt(_seg_sum(kk_raw * kk_raw, e, et) + EPS)
    kk_out[0] = kk.astype(kk_out.dtype)
    b_out[0] = (kk * a).astype(b_out.dtype)
    kmod = k * (1.0 + (a - 1.0) * ka_ref[...])
    k_out[0] = kmod.astype(k_out.dtype)
    r = shifted(R_OFF, R_OFF + RWKV_WIDTH)
    r_out[0] = r.astype(r_out.dtype)
    v = shifted(V_OFF, V_OFF + RWKV_WIDTH)
    v_out[0] = v.astype(v_out.dtype)
    bonus_out[0] = (_seg_sum(r * kmod * rk_ref[...], e, et) * v).astype(bonus_out.dtype)
    g_out[0] = _bdot(_sigmoid(shifted(GLR_OFF, GLR_OFF + GATE_LORA)), g2_ref[...]).astype(g_out.dtype)
    decay_scale = -math.exp(-0.5)
    wf = w0_ref[0:1] + _dot3(jnp.tanh(shifted(WLF_OFF, WLF_OFF + LANES)), w2f_ref[...])
    lwf = decay_scale * _sigmoid(wf)
    cum_f = _mask_dot(_order_mask(tm, False, False, CHUNK), lwf)
    cumf_out[0] = cum_f
    excf_out[0] = cum_f - lwf
    wb = w0_ref[1:2] + _dot3(jnp.tanh(shifted(WLB_OFF, WLB_OFF + LANES)), w2b_ref[...])
    lwb = decay_scale * _sigmoid(wb)
    cum_b = _mask_dot(_order_mask(tm, True, False, CHUNK), lwb)
    cumb_out[0] = cum_b
    excb_out[0] = cum_b - lwb


def rwkv_prep(z, mu, w0, w2f, w2b, a0, a2, g2, k_k, k_a, r_k, seg_e, seg_et, tm):
    b, s, _ = z.shape
    nt = s // tm
    hb = tm // 8
    full = lambda arr: pl.BlockSpec(arr.shape, lambda bi, t: (0,) * arr.ndim)
    in_specs = []
    for width, off in ((RWKV_MAIN, R_OFF), (RWKV_SMALL, GLR_OFF)):
        cb = off // width
        in_specs += [pl.BlockSpec((1, tm, width), lambda bi, t, cb=cb: (bi, t, cb)),
                     pl.BlockSpec((1, 8, width), lambda bi, t, cb=cb: (bi, jnp.maximum(t * hb - 1, 0), cb)),
                     pl.BlockSpec((1, 8, width), lambda bi, t, cb=cb: (bi, jnp.minimum((t + 1) * hb, nt * hb - 1), cb))]
    params = [mu, w0, w2f, w2b, a0, a2, g2, k_k, k_a, r_k, seg_e, seg_et]
    in_specs += [full(p) for p in params]
    out_blk = pl.BlockSpec((1, tm, RWKV_WIDTH), lambda bi, t: (bi, t, 0))
    return pl.pallas_call(
        _rwkv_prep_kernel,
        grid=(b, nt),
        in_specs=in_specs,
        out_specs=[out_blk] * 11,
        out_shape=[jax.ShapeDtypeStruct((b, s, RWKV_WIDTH), dt) for dt in [BF16] * 5 + [F32] * 4 + [BF16] * 2],
        compiler_params=_cparams(("parallel", "parallel")),
        name="rwkv_prep",
    )(z, z, z, z, z, z, *params)


def _stack_heads(x):
    lane = _iota2(x.shape, 2)
    return jnp.concatenate([jnp.where(lane < RWKV_HEAD, x, 0.0), jnp.where(lane < RWKV_HEAD, 0.0, x)], axis=1)


def _rwkv_chunks(r, k, v, kk, bb, cum, exc, revs, g_ref):
    nb, c, _ = r.shape
    n2 = 2 * c
    tot = jnp.stack([cum[j, (0 if rev else c - 1)][None] for j, rev in enumerate(revs)], axis=0)
    e_neg = jnp.exp(-cum)
    e_end = jnp.exp(tot - cum)
    al2 = _stack_heads(-kk * jnp.exp(exc))
    rb2 = _stack_heads(r * jnp.exp(cum))
    bt2 = _stack_heads(bb * e_neg)
    kt2 = _stack_heads(k * e_neg)
    be2 = _stack_heads(bb * e_end)
    ke2 = _stack_heads(k * e_end)
    v2 = _stack_heads(v)
    gram = _bdot_nt(jnp.concatenate([al2, rb2], axis=1), jnp.concatenate([bt2, kt2], axis=1))
    strict = jnp.stack([_order_mask(n2, rev, True, c) for rev in revs], axis=0)
    incl = jnp.stack([_order_mask(n2, rev, False, c) for rev in revs], axis=0)
    a_ab = jnp.where(strict, gram[:, :n2, :n2], 0.0)
    a_ak = jnp.where(strict, gram[:, :n2, n2:], 0.0)
    a_rb = jnp.where(incl, gram[:, n2:, :n2], 0.0)
    a_rk = jnp.where(incl, gram[:, n2:, n2:], 0.0)
    t_inv = _neumann_inverse(a_ab)
    wu = _bdot(t_inv, jnp.concatenate([al2, _bdot(a_ak, v2)], axis=2))
    g = g_ref[...]
    proj = _bdot_nt(jnp.concatenate([wu[:, :, :LANES], rb2], axis=1), g)
    u2 = proj[:, :n2] + wu[:, :, LANES:]
    uv = jnp.concatenate([u2, v2], axis=1)
    o2 = proj[:, n2:] + _bdot(jnp.concatenate([a_rb, a_rk], axis=2), uv)
    g_ref[...] = g * jnp.exp(tot) + _bdot_tn(uv, jnp.concatenate([be2, ke2], axis=1))
    return o2[:, :c] + o2[:, c:]


def _rwkv_scan_kernel(*refs, pairs):
    fwd_refs, bwd_refs = refs[0:7], refs[7:14]
    of_ref, ob_ref, g_ref = refs[14:17]

    @pl.when(pl.program_id(2) == 0)
    def _():
        g_ref[...] = jnp.zeros_like(g_ref)

    revs = [False] * pairs + [True] * pairs
    c = CHUNK
    sub_steps = of_ref.shape[1] // c
    for sub in range(sub_steps):
        rows_of = (slice(sub * c, (sub + 1) * c), slice((sub_steps - 1 - sub) * c, (sub_steps - sub) * c))
        operands = []
        for t in range(7):
            operands.append(jnp.stack([ref[0, rows_of[d], p * LANES:(p + 1) * LANES].astype(F32)
                                       for d, ref in enumerate((fwd_refs[t], bwd_refs[t])) for p in range(pairs)],
                                      axis=0))
        o = _rwkv_chunks(*operands, revs, g_ref)
        for d, o_ref in enumerate((of_ref, ob_ref)):
            for p in range(pairs):
                o_ref[0, rows_of[d], p * LANES:(p + 1) * LANES] = o[d * pairs + p].astype(o_ref.dtype)


def rwkv_scan(r, k, v, kk, bb, cum_f, exc_f, cum_b, exc_b, pairs):
    b, s, _ = r.shape
    c = CHUNK * SCAN_CHUNKS_PER_STEP
    n = s // c
    w = pairs * LANES
    fwd = pl.BlockSpec((1, c, w), lambda bi, p, t: (bi, t, p))
    bwd = pl.BlockSpec((1, c, w), lambda bi, p, t: (bi, n - 1 - t, p))
    return pl.pallas_call(
        functools.partial(_rwkv_scan_kernel, pairs=pairs),
        grid=(b, RWKV_WIDTH // w, n),
        in_specs=[fwd] * 7 + [bwd] * 7,
        out_specs=[fwd, bwd],
        out_shape=[jax.ShapeDtypeStruct((b, s, RWKV_WIDTH), BF16)] * 2,
        scratch_shapes=[pltpu.VMEM((2 * pairs, LANES, LANES), F32)],
        compiler_params=_cparams(("parallel", "parallel", "arbitrary")),
        name="rwkv_scan",
    )(r, k, v, kk, bb, cum_f, exc_f, r, k, v, kk, bb, cum_b, exc_b)


def _rwkv_post_kernel(of_ref, ob_ref, bonus_ref, g_ref, lnw_ref, lnb_ref, e_ref, et_ref, y_ref):
    e = e_ref[...]
    et = et_ref[...]
    o = of_ref[0].astype(F32) + ob_ref[0].astype(F32)
    mean = _seg_sum(o, e, et) * (1.0 / RWKV_HEAD)
    cen = o - mean
    var = _seg_sum(cen * cen, e, et) * (1.0 / RWKV_HEAD)
    y = cen * lax.rsqrt(var + RWKV_LN_EPS) * lnw_ref[...] + lnb_ref[...]
    y_ref[0] = ((y + bonus_ref[0].astype(F32)) * g_ref[0].astype(F32)).astype(y_ref.dtype)


def rwkv_post(o_f, o_b, bonus, g, ln_w, ln_b, seg_e, seg_et, tm):
    b, s, w = o_f.shape
    blk = pl.BlockSpec((1, tm, w), lambda bi, t: (bi, t, 0))
    full = lambda arr: pl.BlockSpec(arr.shape, lambda bi, t: (0,) * arr.ndim)
    params = [ln_w.reshape(1, w), ln_b.reshape(1, w), seg_e, seg_et]
    return pl.pallas_call(
        _rwkv_post_kernel,
        grid=(b, s // tm),
        in_specs=[blk] * 4 + [full(p) for p in params],
        out_specs=blk,
        out_shape=jax.ShapeDtypeStruct((b, s, w), BF16),
        compiler_params=_cparams(("parallel", "parallel")),
        name="rwkv_post",
    )(o_f, o_b, bonus, g, *params)


def _gdn_conv_kernel(z_ref, zp_ref, zn_ref, cw_ref, sc_ref, o_ref, *, l2norm):
    t = pl.program_id(1)
    tm = z_ref.shape[1]
    halo = zp_ref.shape[1]
    ext = jnp.concatenate([jnp.where(t == 0, 0.0, zp_ref[0].astype(F32)), z_ref[0].astype(F32),
                           jnp.where(t == pl.num_programs(1) - 1, 0.0, zn_ref[0].astype(F32))], axis=0)
    n = tm + 2 * halo
    pad = cw_ref.shape[0] // 2
    y = 0.0
    for j in range(cw_ref.shape[0]):
        shifted = ext if j == pad else pltpu.roll(ext, (pad - j) % n, axis=0)
        y = y + cw_ref[j:j + 1] * shifted[halo:halo + tm]
    y = _silu(y)
    if l2norm:
        for h in range(y.shape[1] // GDN_DK):
            sl = slice(h * GDN_DK, (h + 1) * GDN_DK)
            yh = y[:, sl]
            ss = jnp.sum(yh * yh, axis=-1, keepdims=True)
            o_ref[0, :, sl] = (yh * lax.rsqrt(ss + EPS) * sc_ref[0:1, sl]).astype(o_ref.dtype)
    else:
        o_ref[0] = y.astype(o_ref.dtype)


def gdn_conv(z, conv_w, scale, col_off, width, l2norm, tm, cw):
    b, s, _ = z.shape
    nt = s // tm
    halo = 8 * (4 // z.dtype.itemsize)
    hb = tm // halo
    cb = col_off // cw
    in_specs = [pl.BlockSpec((1, tm, cw), lambda bi, t, c: (bi, t, cb + c)),
                pl.BlockSpec((1, halo, cw), lambda bi, t, c: (bi, jnp.maximum(t * hb - 1, 0), cb + c)),
                pl.BlockSpec((1, halo, cw), lambda bi, t, c: (bi, jnp.minimum((t + 1) * hb, nt * hb - 1), cb + c)),
                pl.BlockSpec((conv_w.shape[0], cw), lambda bi, t, c: (0, cb + c)),
                pl.BlockSpec((1, cw), lambda bi, t, c: (0, c))]
    return pl.pallas_call(
        functools.partial(_gdn_conv_kernel, l2norm=l2norm),
        grid=(b, nt, width // cw),
        in_specs=in_specs,
        out_specs=pl.BlockSpec((1, tm, cw), lambda bi, t, c: (bi, t, c)),
        out_shape=jax.ShapeDtypeStruct((b, s, width), BF16),
        compiler_params=_cparams(("parallel", "parallel", "parallel")),
        name="gdn_conv_norm" if l2norm else "gdn_conv",
    )(z, z, z, conv_w, scale)


def _gdn_gates_kernel(zs_ref, na_ref, dtb_ref, o_ref):
    zs = zs_ref[0]
    tm = zs.shape[0]
    gg = na_ref[...] * _softplus(zs + dtb_ref[...])
    gam_f = _mask_dot(_order_mask(tm, False, False, CHUNK), gg)
    gam_b = _mask_dot(_order_mask(tm, True, False, CHUNK), gg)
    lane = _iota2(zs.shape, 1)
    o_ref[0] = jnp.where(lane < 2 * GDN_VHEADS, _sigmoid(zs), jnp.where(lane < 3 * GDN_VHEADS, gam_f, gam_b))


def gdn_gates(zs, neg_a, dtb, tm):
    b, s, w = zs.shape
    blk = pl.BlockSpec((1, tm, w), lambda bi, t: (bi, t, 0))
    row = pl.BlockSpec((1, w), lambda bi, t: (0, 0))
    return pl.pallas_call(
        _gdn_gates_kernel,
        grid=(b, s // tm),
        in_specs=[blk, row, row],
        out_specs=blk,
        out_shape=jax.ShapeDtypeStruct((b, s, w), F32),
        compiler_params=_cparams(("parallel", "parallel")),
        name="gdn_gates",
    )(zs, neg_a, dtb)


def _gdn_chunks(q, k, v2, beta2, gam2, revs, s_ref):
    nb, c, _ = q.shape
    n2 = 2 * c
    per_batch = lambda fn: jnp.stack([fn(r) for r in revs], axis=0)
    incl = per_batch(lambda r: _order_mask(n2, r, False, c))
    strict = per_batch(lambda r: _order_mask(n2, r, True, c))
    gam_c = jnp.broadcast_to(gam2, (nb, n2, n2))
    gam_r = jnp.swapaxes(gam_c, 1, 2)
    tot2 = jnp.stack([jnp.concatenate(
        [jnp.broadcast_to(gam2[j, h * c + (0 if r else c - 1)][None], (c, 1)) for h in range(2)], axis=0)
        for j, r in enumerate(revs)], axis=0)
    diff = gam_c - gam_r
    decay = jnp.exp(jnp.where(incl, diff, 0.0))
    dec_s = jnp.where(strict, decay, 0.0)
    dec_i = jnp.where(incl, decay, 0.0)
    k2 = jnp.concatenate([k, k], axis=1)
    q2 = jnp.concatenate([q, q], axis=1)
    gram = _bdot_nt(jnp.concatenate([k2, q2], axis=1), k2)
    a_mat = gram[:, :n2] * beta2 * dec_s
    qk = gram[:, n2:] * dec_i
    t_inv = _neumann_inverse(a_mat, negate=True)
    e_gam = jnp.exp(gam2)
    uw = _bdot(t_inv, jnp.concatenate([v2 * beta2, k2 * (beta2 * e_gam)], axis=2))
    u2 = uw[:, :, :GDN_DV]
    w2 = uw[:, :, GDN_DV:]
    qd2 = q2 * e_gam
    ke2 = k2 * jnp.exp(tot2 - gam2)
    dl2 = jnp.exp(tot2)
    ws, qs = [], []
    for e in range(2):
        rows = slice(e * c, (e + 1) * c)
        both = _bdot(jnp.concatenate([w2[:, rows], qd2[:, rows]], axis=1), s_ref[e])
        ws.append(both[:, :c])
        qs.append(both[:, c:])
    vnew2 = u2 - jnp.concatenate(ws, axis=1)
    o2 = jnp.concatenate(qs, axis=1) + _bdot(qk, vnew2)
    for e in range(2):
        rows = slice(e * c, (e + 1) * c)
        s_ref[e] = s_ref[e] * dl2[:, e * c:e * c + 1, :] + _bdot_tn(ke2[:, rows], vnew2[:, rows])
    return o2


def _gdn_scan_kernel(qf, kf, vf, gf, qb, kb, vb, gb, of_ref, ob_ref, s_ref, *, heads):
    @pl.when(pl.program_id(2) == 0)
    def _():
        s_ref[...] = jnp.zeros_like(s_ref)

    c = CHUNK
    sub_steps = qf.shape[1] // c
    lane = _iota2((c, LANES), 1)

    def column(gates, idx):
        return jnp.sum(jnp.where(lane == idx, gates, 0.0), axis=1, keepdims=True)

    for sub in range(sub_steps):
        rows_of = (slice(sub * c, (sub + 1) * c), slice((sub_steps - 1 - sub) * c, (sub_steps - sub) * c))
        qs, ks, vs, betas, gams, revs = [], [], [], [], [], []
        for d, (q_ref, k_ref, v_ref, g_ref) in enumerate(((qf, kf, vf, gf), (qb, kb, vb, gb))):
            rows = rows_of[d]
            gates = g_ref[0, rows, :]
            for i in range(heads):
                vh = 2 * (pl.program_id(1) * heads + i)
                ksl = slice(i * GDN_DK, (i + 1) * GDN_DK)
                qs.append(q_ref[0, rows, ksl].astype(F32))
                ks.append(k_ref[0, rows, ksl].astype(F32))
                vs.append(jnp.concatenate(
                    [v_ref[0, rows, (2 * i + e) * GDN_DV:(2 * i + e + 1) * GDN_DV].astype(F32) for e in range(2)], axis=0))
                betas.append(jnp.concatenate([column(gates, d * GDN_VHEADS + vh + e) for e in range(2)], axis=0))
                gams.append(jnp.concatenate([column(gates, (2 + d) * GDN_VHEADS + vh + e) for e in range(2)], axis=0))
                revs.append(d == 1)
        o2 = _gdn_chunks(jnp.stack(qs), jnp.stack(ks), jnp.stack(vs), jnp.stack(betas), jnp.stack(gams), revs, s_ref)
        for d, o_ref in enumerate((of_ref, ob_ref)):
            for i in range(heads):
                j = d * heads + i
                for e in range(2):
                    o_ref[0, rows_of[d], (2 * i + e) * GDN_DV:(2 * i + e + 1) * GDN_DV] = (
                        o2[j, e * c:(e + 1) * c].astype(o_ref.dtype))


def gdn_scan(qk, v, gates, heads):
    b, s, _ = v.shape
    c = CHUNK * SCAN_CHUNKS_PER_STEP
    n = s // c
    ng = GDN_KHEADS // heads

    def spec(w, off, rev):
        if rev:
            return pl.BlockSpec((1, c, w), lambda bi, h, t: (bi, n - 1 - t, off + h))
        return pl.BlockSpec((1, c, w), lambda bi, h, t: (bi, t, off + h))

    def gspec(rev):
        if rev:
            return pl.BlockSpec((1, c, LANES), lambda bi, h, t: (bi, n - 1 - t, 0))
        return pl.BlockSpec((1, c, LANES), lambda bi, h, t: (bi, t, 0))

    kw = heads * GDN_DK
    vw = heads * 2 * GDN_DV
    in_specs = [spec(kw, 0, False), spec(kw, ng, False), spec(vw, 0, False), gspec(False),
                spec(kw, 0, True), spec(kw, ng, True), spec(vw, 0, True), gspec(True)]
    return pl.pallas_call(
        functools.partial(_gdn_scan_kernel, heads=heads),
        grid=(b, ng, n),
        in_specs=in_specs,
        out_specs=[spec(vw, 0, False), spec(vw, 0, True)],
        out_shape=[jax.ShapeDtypeStruct((b, s, GDN_VW), BF16)] * 2,
        scratch_shapes=[pltpu.VMEM((2, 2 * heads, GDN_DK, GDN_DV), F32)],
        compiler_params=_cparams(("parallel", "parallel", "arbitrary")),
        name="gdn_scan",
    )(qk, qk, v, gates, qk, qk, v, gates)


def _gdn_post_kernel(of_ref, ob_ref, gate_ref, nw_ref, y_ref):
    nw = nw_ref[...]
    for h in range(of_ref.shape[2] // GDN_DV):
        sl = slice(h * GDN_DV, (h + 1) * GDN_DV)
        o = of_ref[0, :, sl].astype(F32) + ob_ref[0, :, sl].astype(F32)
        ms = jnp.mean(o * o, axis=-1, keepdims=True)
        y = o * lax.rsqrt(ms + EPS) * nw
        y_ref[0, :, sl] = (y * _silu(gate_ref[0, :, sl].astype(F32))).astype(y_ref.dtype)


def gdn_post(o_f, o_b, z, gate_off, norm_w, tm, cw):
    b, s, w = o_f.shape
    blk = pl.BlockSpec((1, tm, cw), lambda bi, t, c: (bi, t, c))
    gb = gate_off // cw
    return pl.pallas_call(
        _gdn_post_kernel,
        grid=(b, s // tm, w // cw),
        in_specs=[blk, blk, pl.BlockSpec((1, tm, cw), lambda bi, t, c: (bi, t, gb + c)),
                  pl.BlockSpec((1, GDN_DV), lambda bi, t, c: (0, 0))],
        out_specs=blk,
        out_shape=jax.ShapeDtypeStruct((b, s, w), BF16),
        compiler_params=_cparams(("parallel", "parallel", "parallel")),
        name="gdn_post",
    )(o_f, o_b, z, norm_w.reshape(1, GDN_DV))


def _moe_router_kernel(x_ref, g_ref, w_ref, b_ref, route_ref, counts_ref):
    @pl.when(pl.program_id(0) == 0)
    def _():
        counts_ref[...] = jnp.zeros_like(counts_ref)

    x = x_ref[...]
    tm = x.shape[0]
    ms = jnp.mean(x * x, axis=-1, keepdims=True)
    h = x * lax.rsqrt(ms + EPS) * g_ref[...]
    lane_i = _iota2((tm, LANES), 1)
    lane = lane_i.astype(F32)
    lane_grp = lax.shift_right_logical(lane_i, 3).astype(F32)
    neg = -jnp.inf
    logits = _dot3(h, w_ref[...]) + b_ref[...]
    gl = jnp.where((lane_i >= N_EXPERTS) & (lane_i < N_EXPERTS + N_GROUPS), logits, neg)
    gmax = jnp.max(gl, axis=-1, keepdims=True)
    gidx = jnp.min(jnp.where(gl == gmax, lane, float(LANES)), axis=-1, keepdims=True) - float(N_EXPERTS)
    grp_w = 1.0 / jnp.sum(jnp.exp(gl - gmax), axis=-1, keepdims=True)
    sel = jnp.where((lane_i < N_EXPERTS) & (lane_grp == gidx), logits, neg)
    m1 = jnp.max(sel, axis=-1, keepdims=True)
    i1 = jnp.min(jnp.where(sel == m1, lane, float(LANES)), axis=-1, keepdims=True)
    sel2 = jnp.where(lane == i1, neg, sel)
    m2 = jnp.max(sel2, axis=-1, keepdims=True)
    i2 = jnp.min(jnp.where(sel2 == m2, lane, float(LANES)), axis=-1, keepdims=True)
    e2 = jnp.exp(m2 - m1)
    w1 = grp_w / (1.0 + e2)
    w2 = grp_w * e2 / (1.0 + e2)
    hits = jnp.where((lane == i1) | (lane == i2), 1.0, 0.0)
    before = counts_ref[...] + _bdot(_order_mask(tm, False, True, tm), hits)
    r1 = jnp.sum(jnp.where(lane == i1, before, 0.0), axis=-1, keepdims=True)
    r2 = jnp.sum(jnp.where(lane == i2, before, 0.0), axis=-1, keepdims=True)
    counts_ref[...] += jnp.sum(hits, axis=0, keepdims=True)
    route_ref[...] = jnp.where(lane_i == 0, i1, jnp.where(lane_i == 1, i2, jnp.where(
        lane_i == 2, w1, jnp.where(lane_i == 3, w2, jnp.where(lane_i == 4, r1, jnp.where(lane_i == 5, r2, 0.0))))))


def moe_router(x, g, w_group, b_group, w_router, b_router, tm):
    m, d = x.shape
    tm = min(tm, m)
    pad = lambda w: jnp.pad(w, ((0, 0), (0, LANES - w.shape[1])))
    full = lambda arr: pl.BlockSpec(arr.shape, lambda i: (0, 0))
    params = [g.reshape(1, d), pad(jnp.concatenate([w_router, w_group], axis=1)),
              pad(jnp.concatenate([b_router, b_group]).reshape(1, -1))]
    return pl.pallas_call(
        _moe_router_kernel,
        grid=(m // tm,),
        in_specs=[pl.BlockSpec((tm, d), lambda i: (i, 0))] + [full(p) for p in params],
        out_specs=[pl.BlockSpec((tm, LANES), lambda i: (i, 0)), pl.BlockSpec((1, LANES), lambda i: (0, 0))],
        out_shape=[jax.ShapeDtypeStruct((m, LANES), F32), jax.ShapeDtypeStruct((1, LANES), F32)],
        compiler_params=_cparams(("arbitrary",)),
        name="moe_router",
    )(x, *params)


def _route_tables(route, counts, tm, n_tiles):
    counts = counts[0, :N_EXPERTS].astype(jnp.int32)
    padded = ((counts + tm - 1) // tm) * tm
    ends = jnp.cumsum(padded)
    starts = ends - padded
    experts = route[:, 0:2].astype(jnp.int32)
    pos = (starts[experts] + route[:, 4:6].astype(jnp.int32)).reshape(-1)
    tile_start = jnp.arange(n_tiles, dtype=jnp.int32) * tm
    tile_e = jnp.minimum(jnp.sum(tile_start[:, None] >= ends[None, :], axis=1), N_EXPERTS - 1).astype(jnp.int32)
    n_valid = (ends[-1:] // tm).astype(jnp.int32)
    fill = jnp.stack([jnp.maximum(ends - tm, 0), (padded > 0).astype(jnp.int32)]).astype(jnp.int32)
    return pos, tile_e, n_valid, fill


def _moe_dispatch_kernel(pos_ref, nvalid_ref, fill_ref, x_ref, g_ref, hs_hbm, hbuf, zbuf, sem_rows, sem_fill, *,
                         n_tiles):
    i = pl.program_id(0)
    tm = x_ref.shape[0]
    row_tile = zbuf.shape[0]

    @pl.when(i == 0)
    def _():
        zbuf[...] = jnp.zeros_like(zbuf)
        fills = [(fill_ref[1, e] > 0, fill_ref[0, e]) for e in range(N_EXPERTS)]
        fills += [(nvalid_ref[0] + k < n_tiles, (nvalid_ref[0] + k) * row_tile) for k in range(N_EXPERTS)]
        fills = [(cond, pl.multiple_of(start, row_tile)) for cond, start in fills]
        for cond, start in fills:
            @pl.when(cond)
            def _(start=start):
                pltpu.make_async_copy(zbuf, hs_hbm.at[pl.ds(start, row_tile)], sem_fill).start()
        for cond, start in fills:
            @pl.when(cond)
            def _(start=start):
                pltpu.make_async_copy(zbuf, hs_hbm.at[pl.ds(start, row_tile)], sem_fill).wait()

    buf = i % 2
    x = x_ref[...]
    ms = jnp.mean(x * x, axis=-1, keepdims=True)
    hbuf[buf] = x * lax.rsqrt(ms + EPS) * g_ref[...]

    def issue(r, carry):
        base = 2 * (i * tm + r)
        for slot in range(2):
            pltpu.make_async_copy(hbuf.at[buf, pl.ds(r, 1)], hs_hbm.at[pl.ds(pos_ref[base + slot], 1)],
                                  sem_rows.at[buf]).start()
        return carry

    lax.fori_loop(0, tm, issue, 0, unroll=8)

    def drain(b):
        for _ in range(2):
            pltpu.make_async_copy(hbuf.at[b], hs_hbm.at[pl.ds(0, tm)], sem_rows.at[b]).wait()

    @pl.when(i > 0)
    def _():
        drain(1 - buf)

    @pl.when(i == pl.num_programs(0) - 1)
    def _():
        drain(buf)


def moe_dispatch(x, g, pos, n_valid, fill, n_tiles, tm):
    m, d = x.shape
    tm_x = min(256, m)
    return pl.pallas_call(
        functools.partial(_moe_dispatch_kernel, n_tiles=n_tiles),
        grid_spec=pltpu.PrefetchScalarGridSpec(
            num_scalar_prefetch=3,
            grid=(m // tm_x,),
            in_specs=[pl.BlockSpec((tm_x, d), lambda i, p, nv, fl: (i, 0)),
                      pl.BlockSpec((1, d), lambda i, p, nv, fl: (0, 0))],
            out_specs=pl.BlockSpec(memory_space=pl.ANY),
            scratch_shapes=[pltpu.VMEM((2, tm_x, d), F32), pltpu.VMEM((tm, d), F32),
                            pltpu.SemaphoreType.DMA((2,)), pltpu.SemaphoreType.DMA(())]),
        out_shape=jax.ShapeDtypeStruct((n_tiles * tm, d), F32),
        compiler_params=_cparams(("arbitrary",)),
        name="moe_dispatch",
    )(pos, n_valid, fill, x, g.reshape(1, d))


def _moe_ffn_kernel(tile_e_ref, nvalid_ref, h_ref, wg_ref, wu_ref, wd_ref, y_ref, wg_b, wu_b, wd_b):
    i = pl.program_id(0)

    @pl.when((i == 0) | (tile_e_ref[i] != tile_e_ref[jnp.maximum(i - 1, 0)]))
    def _():
        wg_b[...] = wg_ref[0, 0].astype(BF16)
        wu_b[...] = wu_ref[0, 0].astype(BF16)
        wd_b[...] = wd_ref[0, 0].astype(BF16)

    @pl.when(i < nvalid_ref[0])
    def _():
        x = h_ref[...].astype(BF16)
        hid = _silu(_dot(x, wg_b[...])) * _dot(x, wu_b[...])
        y_ref[...] = _dot(hid.astype(BF16), wd_b[...])

    @pl.when(i >= nvalid_ref[0])
    def _():
        y_ref[...] = jnp.zeros_like(y_ref)


def moe_ffn(h_sorted, tile_e, n_valid, w_gate, w_up, w_down, layer, tm):
    _, d = h_sorted.shape
    n_tiles = tile_e.shape[0]
    ff = w_gate.shape[-1]
    wspec = lambda shape: pl.BlockSpec((1, 1) + shape, lambda i, te, nv: (layer, te[i], 0, 0))
    return pl.pallas_call(
        _moe_ffn_kernel,
        grid_spec=pltpu.PrefetchScalarGridSpec(
            num_scalar_prefetch=2,
            grid=(n_tiles,),
            in_specs=[pl.BlockSpec((tm, d), lambda i, te, nv: (i, 0)), wspec((d, ff)), wspec((d, ff)), wspec((ff, d))],
            out_specs=pl.BlockSpec((tm, d), lambda i, te, nv: (i, 0)),
            scratch_shapes=[pltpu.VMEM((d, ff), BF16), pltpu.VMEM((d, ff), BF16), pltpu.VMEM((ff, d), BF16)]),
        out_shape=jax.ShapeDtypeStruct((n_tiles * tm, d), F32),
        compiler_params=_cparams(("arbitrary",)),
        name="moe_ffn",
    )(tile_e, n_valid, h_sorted, w_gate, w_up, w_down)


def _moe_combine_kernel(pos_ref, x_ref, route_ref, y_hbm, g_ref, o_ref, ybuf, sems, *, final_norm):
    i = pl.program_id(0)
    tm = x_ref.shape[0]

    def gather(tile, buf):
        def issue(r, carry):
            base = 2 * (tile * tm + r)
            for slot in range(2):
                pltpu.make_async_copy(y_hbm.at[pl.ds(pos_ref[base + slot], 1)], ybuf.at[buf, slot, pl.ds(r, 1)],
                                      sems.at[buf]).start()
            return carry

        lax.fori_loop(0, tm, issue, 0, unroll=8)

    @pl.when(i == 0)
    def _():
        gather(0, 0)

    buf = i % 2

    @pl.when(i + 1 < pl.num_programs(0))
    def _():
        gather(i + 1, 1 - buf)

    for slot in range(2):
        pltpu.make_async_copy(y_hbm.at[pl.ds(0, tm)], ybuf.at[buf, slot], sems.at[buf]).wait()
    route = route_ref[...]
    o = x_ref[...] + route[:, 2:3] * ybuf[buf, 0] + route[:, 3:4] * ybuf[buf, 1]
    if final_norm:
        ms = jnp.mean(o * o, axis=-1, keepdims=True)
        o = o * lax.rsqrt(ms + EPS) * g_ref[...]
    o_ref[...] = o


def moe_combine(x, route, y_sorted, pos, norm_g, tm):
    m, d = x.shape
    tm = min(tm, m)
    final_norm = norm_g is not None
    g = (norm_g if final_norm else jnp.ones((d,), F32)).reshape(1, d)
    return pl.pallas_call(
        functools.partial(_moe_combine_kernel, final_norm=final_norm),
        grid_spec=pltpu.PrefetchScalarGridSpec(
            num_scalar_prefetch=1,
            grid=(m // tm,),
            in_specs=[pl.BlockSpec((tm, d), lambda i, p: (i, 0)), pl.BlockSpec((tm, LANES), lambda i, p: (i, 0)),
                      pl.BlockSpec(memory_space=pl.ANY), pl.BlockSpec((1, d), lambda i, p: (0, 0))],
            out_specs=pl.BlockSpec((tm, d), lambda i, p: (i, 0)),
            scratch_shapes=[pltpu.VMEM((2, 2, tm, d), F32), pltpu.SemaphoreType.DMA((2,))]),
        out_shape=jax.ShapeDtypeStruct((m, d), F32),
        compiler_params=_cparams(("arbitrary",)),
        name="moe_combine",
    )(pos, x, route, y_sorted, g)


def _pad_rows(w, rows):
    return jnp.pad(w, ((0, rows - w.shape[0]), (0, 0)))


def _even_layer(x, norm_g, w_in, gla_w_alpha, gla_b_alpha, gla_norm, mu, w0, w2, a0, a2, g2,
                k_k, k_a, r_k, ln_w, ln_b, w_out):
    b, s, d = x.shape
    gla_cols, rw = w_in[:, :3104], w_in[:, 3104:]
    mu_g = lambda lo, hi, width: jnp.pad(mu[:, lo:hi], ((0, 0), (0, width - (hi - lo))))
    pad_c = lambda w, width: jnp.pad(w, ((0, 0), (0, width - w.shape[1])))
    w_cat = jnp.concatenate([
        gla_cols[:, 0:3072],
        rw[:, 0:3072], rw[:, 3360:3616],
        pad_c(rw[:, 3072:3168], LANES), pad_c(rw[:, 3168:3264], LANES), pad_c(rw[:, 3264:3360], LANES),
        pad_c(gla_cols[:, 3072:3104], LANES)], axis=1).astype(BF16)
    mu_cat = jnp.concatenate([
        mu[:, 0:3072], mu[:, 3360:3616], mu_g(3072, 3168, LANES), mu_g(3168, 3264, LANES),
        mu_g(3264, 3360, LANES)], axis=1)
    z = norm_matmul(x.reshape(b * s, d), norm_g, w_cat, 1024, 1152).reshape(b, s, EVEN_PAD)

    wa = jnp.stack([_pad_rows(gla_w_alpha[0], LANES),
                    jnp.pad(gla_w_alpha[1], ((GLA_LOWRANK, LANES - 2 * GLA_LOWRANK), (0, 0)))])
    gla_f, gla_b = gla_scan(z, wa, gla_b_alpha)
    y_gla = gla_post(gla_f, gla_b, z, gla_norm, 256)

    head_of_lane = jnp.arange(RWKV_WIDTH) // RWKV_HEAD
    seg_e = (head_of_lane[:, None] == jnp.arange(LANES)[None, :]).astype(F32)
    seg_et = seg_e.T
    row = lambda p: p.reshape(1, RWKV_WIDTH)
    r, k, v, kk, bb, cum_f, exc_f, cum_b, exc_b, g, bonus = rwkv_prep(
        z, mu_cat, w0, _pad_rows(w2[0], LANES), _pad_rows(w2[1], LANES), row(a0), _pad_rows(a2, LANES), g2,
        row(k_k), row(k_a), row(r_k), seg_e, seg_et, 256)
    rw_f, rw_b = rwkv_scan(r, k, v, kk, bb, cum_f, exc_f, cum_b, exc_b, RWKV_PAIRS_PER_STEP)
    y_rwkv = rwkv_post(rw_f, rw_b, bonus, g, ln_w, ln_b, seg_e, seg_et, 256)

    w_out = w_out.astype(BF16)
    out = proj_residual([y_gla.reshape(b * s, GLA_WIDTH), y_rwkv.reshape(b * s, RWKV_WIDTH)],
                        [w_out[:GLA_WIDTH], w_out[GLA_WIDTH:]], x.reshape(b * s, d), 1024, 1024)
    return out.reshape(b, s, d)


def _odd_layer(x, norm_g, w_in, conv_w, a_log, dt_bias, norm_w, w_out):
    b, s, d = x.shape
    x2 = x.reshape(b * s, d)
    main = GDN_QKV + GDN_VW
    z = norm_matmul(x2, norm_g, w_in[:, :main].astype(BF16), 1024, 2048, BF16).reshape(b, s, main)
    zs = norm_matmul(x2, norm_g, w_in[:, main:].astype(BF16), 512, LANES).reshape(b, s, LANES)
    scale = jnp.concatenate([jnp.full((1, GDN_KW), GDN_DK ** -0.5, F32), jnp.ones((1, GDN_KW), F32)], axis=1)
    qk = gdn_conv(z, conv_w, scale, 0, 2 * GDN_KW, True, 256, 1024)
    v = gdn_conv(z, conv_w, scale, 2 * GDN_KW, GDN_VW, False, 256, 1024)
    zero = jnp.zeros((2 * GDN_VHEADS,), F32)
    neg_a = jnp.concatenate([zero, -jnp.exp(a_log.reshape(-1))]).reshape(1, LANES)
    dtb = jnp.concatenate([zero, dt_bias.reshape(-1)]).reshape(1, LANES)
    gates = gdn_gates(zs, neg_a, dtb, 256)
    o_f, o_b = gdn_scan(qk, v, gates, GDN_HEADS_PER_STEP)
    y = gdn_post(o_f, o_b, z, GDN_QKV, norm_w, 256, 1024)
    out = proj_residual([y.reshape(b * s, GDN_VW)], [w_out.astype(BF16)], x2, 1024, 1024)
    return out.reshape(b, s, d)


def _moe_layer(x, norm_g, w_group, b_group, w_router, b_router, w_gate, w_up, w_down, layer, final_g):
    b, s, d = x.shape
    x2 = x.reshape(b * s, d)
    route, counts = moe_router(x2, norm_g, w_group, b_group, w_router, b_router, 256)
    n_tiles = -(-2 * b * s // MOE_ROW_TILE) + N_EXPERTS
    pos, tile_e, n_valid, fill = _route_tables(route, counts, MOE_ROW_TILE, n_tiles)
    h_sorted = moe_dispatch(x2, norm_g, pos, n_valid, fill, n_tiles, MOE_ROW_TILE)
    y_sorted = moe_ffn(h_sorted, tile_e, n_valid, w_gate, w_up, w_down, layer, MOE_ROW_TILE)
    out = moe_combine(x2, route, y_sorted, pos, final_g, 256)
    return out.reshape(b, s, d)


def kernel(x, norm_mix, norm_ffn, norm_final, ev_w_in, ev_gla_w_alpha, ev_gla_b_alpha, ev_gla_norm, ev_rwkv_mu, ev_rwkv_w0, ev_rwkv_w2, ev_rwkv_a0, ev_rwkv_a2, ev_rwkv_g2, ev_rwkv_k_k, ev_rwkv_k_a, ev_rwkv_r_k, ev_rwkv_ln_w, ev_rwkv_ln_b, ev_w_out, od_w_in, od_conv, od_a_log, od_dt_bias, od_norm, od_w_out, moe_w_group, moe_b_group, moe_w_router, moe_b_router, moe_w_gate, moe_w_up, moe_w_down):
    depth = norm_mix.shape[0]
    for i in range(depth):
        j = i // 2
        if i % 2 == 0:
            x = _even_layer(x, norm_mix[i], ev_w_in[j], ev_gla_w_alpha[j], ev_gla_b_alpha[j], ev_gla_norm[j],
                            ev_rwkv_mu[j], ev_rwkv_w0[j], ev_rwkv_w2[j], ev_rwkv_a0[j], ev_rwkv_a2[j],
                            ev_rwkv_g2[j], ev_rwkv_k_k[j], ev_rwkv_k_a[j], ev_rwkv_r_k[j],
                            ev_rwkv_ln_w[j], ev_rwkv_ln_b[j], ev_w_out[j])
        else:
            x = _odd_layer(x, norm_mix[i], od_w_in[j], od_conv[j], od_a_log[j], od_dt_bias[j],
                           od_norm[j], od_w_out[j])
        x = _moe_layer(x, norm_ffn[i], moe_w_group[i], moe_b_group[i], moe_w_router[i], moe_b_router[i],
                       moe_w_gate, moe_w_up, moe_w_down, i, norm_final if i == depth - 1 else None)
    return x
```

```python
import functools
import math

import jax
import jax.numpy as jnp
from jax import lax
from jax.experimental import pallas as pl
from jax.experimental.pallas import tpu as pltpu

F32 = jnp.float32
BF16 = jnp.bfloat16
HI = lax.Precision.HIGHEST

EPS = 1e-6
CHUNK = 64
LANES = 128
VMEM_LIMIT = 56 * 1024 * 1024

D_MODEL = 2048
GLA_HEADS = 4
GLA_DK = 128
GLA_DV = 256
GLA_KW = 512
GLA_WIDTH = 1024
GLA_LOWRANK = 16
GLA_TAU = 16.0
RWKV_HEAD = 64
RWKV_WIDTH = 1024
RWKV_HEADS = 16
DECAY_LORA = 96
ICLR_LORA = 96
GATE_LORA = 256
RWKV_LN_EPS = 64e-5
GDN_DK = 128
GDN_DV = 128
GDN_KHEADS = 16
GDN_VHEADS = 32
GDN_KW = 2048
GDN_VW = 4096
GDN_QKV = 8192
SCAN_CHUNKS_PER_STEP = 4
RWKV_PAIRS_PER_STEP = 8
GDN_HEADS_PER_STEP = 8
MOE_ROW_TILE = 256
N_GROUPS = 4
EXPERTS_PER_GROUP = 8
N_EXPERTS = 32
EXPERT_FF = 512

GQ_OFF, GK_OFF, GV_OFF, GG_OFF = 0, 512, 1024, 2048
R_OFF, K_OFF, V_OFF = 3072, 4096, 5120
GLR_OFF, WLF_OFF, WLB_OFF, ALR_OFF, AL_OFF = 6144, 6400, 6528, 6656, 6784
RWKV_MAIN = 3072
RWKV_SMALL = 768
EVEN_PAD = 6912


def _cparams(sem):
    return pltpu.CompilerParams(dimension_semantics=sem, vmem_limit_bytes=VMEM_LIMIT)


def _mm(a, b, ca, cb, precision):
    if a.ndim == 3:
        dims = (((ca + 1,), (cb + 1,)), ((0,), (0,)))
    else:
        dims = (((ca,), (cb,)), ((), ()))
    return lax.dot_general(a, b, dims, preferred_element_type=F32, precision=precision)


def _dot(a, b, precision=None):
    return _mm(a, b, 1, 0, precision)


def _dot_nt(a, b, precision=None):
    return _mm(a, b, 1, 1, precision)


def _dot_tn(a, b, precision=None):
    return _mm(a, b, 0, 0, precision)


def _bf16_pieces(x, n):
    pieces = []
    for _ in range(n - 1):
        p = x.astype(BF16)
        pieces.append(p)
        x = x - p.astype(F32)
    pieces.append(x.astype(BF16))
    return pieces


def _mask_dot(mask, x, pieces=3):
    mb = mask.astype(BF16)
    return sum(_dot(mb, p) for p in _bf16_pieces(x, pieces))


def _dot_mask(x, mask, pieces=2):
    mb = mask.astype(BF16)
    return sum(_dot(p, mb) for p in _bf16_pieces(x, pieces))


def _dot3(a, b):
    ah, al = _bf16_pieces(a, 2)
    bh, bl = _bf16_pieces(b, 2)
    return _dot(ah, bh) + _dot(ah, bl) + _dot(al, bh)


def _bdot(a, b):
    return _dot(a.astype(BF16), b.astype(BF16))


def _bdot_nt(a, b):
    return _dot_nt(a.astype(BF16), b.astype(BF16))


def _bdot_tn(a, b):
    return _dot_tn(a.astype(BF16), b.astype(BF16))


def _sigmoid(x):
    return 1.0 / (1.0 + jnp.exp(-x))


def _silu(x):
    return x * _sigmoid(x)


def _softplus(x):
    return jnp.maximum(x, 0.0) + jnp.log(1.0 + jnp.exp(-jnp.abs(x)))


def _iota2(shape, dim):
    return lax.broadcasted_iota(jnp.int32, shape, dim)


def _order_mask(n, rev, strict, block):
    i = _iota2((n, n), 0)
    j = _iota2((n, n), 1)
    if rev:
        m = (j > i) if strict else (j >= i)
    else:
        m = (j < i) if strict else (j <= i)
    if block < n:
        sh = block.bit_length() - 1
        m = m & (lax.shift_right_logical(i, sh) == lax.shift_right_logical(j, sh))
    return m


def _neumann_inverse(nmat, negate=False):
    n = nmat.shape[-1]
    eye = (_iota2((n, n), 0) == _iota2((n, n), 1)).astype(F32)
    t = eye - nmat if negate else eye + nmat
    nb = nmat.astype(BF16)
    p = _dot(nb, nb)
    for step in range(5):
        pb = p.astype(BF16)
        if step < 4:
            both = _dot(pb, jnp.concatenate([pb, t.astype(BF16)], axis=-1))
            p = both[..., :n]
            t = t + both[..., n:]
        else:
            t = t + _dot(pb, t.astype(BF16))
    return t


def _norm_matmul_kernel(x_ref, g_ref, w_ref, o_ref, h_ref):
    @pl.when(pl.program_id(1) == 0)
    def _():
        x = x_ref[...]
        ms = jnp.mean(x * x, axis=-1, keepdims=True)
        h_ref[...] = (x * lax.rsqrt(ms + EPS) * g_ref[...]).astype(BF16)

    o_ref[...] = _dot(h_ref[...], w_ref[...]).astype(o_ref.dtype)


def norm_matmul(x, g, w, tm, tn, out_dtype=F32):
    m, d = x.shape
    tm = min(tm, m)
    n = w.shape[1]
    return pl.pallas_call(
        _norm_matmul_kernel,
        grid=(m // tm, n // tn),
        in_specs=[pl.BlockSpec((tm, d), lambda i, j: (i, 0)),
                  pl.BlockSpec((1, d), lambda i, j: (0, 0)),
                  pl.BlockSpec((d, tn), lambda i, j: (0, j))],
        out_specs=pl.BlockSpec((tm, tn), lambda i, j: (i, j)),
        out_shape=jax.ShapeDtypeStruct((m, n), out_dtype),
        scratch_shapes=[pltpu.VMEM((tm, d), BF16)],
        compiler_params=_cparams(("parallel", "arbitrary")),
        name="norm_matmul",
    )(x, g.reshape(1, d), w)


def _proj_residual_kernel(*refs, n_lhs):
    x_ref = refs[2 * n_lhs]
    o_ref = refs[2 * n_lhs + 1]
    acc = x_ref[...]
    for t in range(n_lhs):
        acc = acc + _dot(refs[t][...], refs[n_lhs + t][...])
    o_ref[...] = acc


def proj_residual(ys, ws, x, tm, tn):
    m, n = x.shape
    tm = min(tm, m)
    n_lhs = len(ys)
    in_specs = [pl.BlockSpec((tm, y.shape[1]), lambda i, j: (i, 0)) for y in ys]
    in_specs += [pl.BlockSpec((w.shape[0], tn), lambda i, j: (0, j)) for w in ws]
    in_specs += [pl.BlockSpec((tm, tn), lambda i, j: (i, j))]
    return pl.pallas_call(
        functools.partial(_proj_residual_kernel, n_lhs=n_lhs),
        grid=(m // tm, n // tn),
        in_specs=in_specs,
        out_specs=pl.BlockSpec((tm, tn), lambda i, j: (i, j)),
        out_shape=jax.ShapeDtypeStruct((m, n), F32),
        compiler_params=_cparams(("parallel", "arbitrary")),
        name="proj_residual",
    )(*ys, *ws, x)


def _final_norm_kernel(x_ref, g_ref, o_ref):
    x = x_ref[...]
    ms = jnp.mean(x * x, axis=-1, keepdims=True)
    o_ref[...] = x * lax.rsqrt(ms + EPS) * g_ref[...]


def final_norm(x, g, tm):
    m, d = x.shape
    tm = min(tm, m)
    return pl.pallas_call(
        _final_norm_kernel,
        grid=(m // tm,),
        in_specs=[pl.BlockSpec((tm, d), lambda i: (i, 0)), pl.BlockSpec((1, d), lambda i: (0, 0))],
        out_specs=pl.BlockSpec((tm, d), lambda i: (i, 0)),
        out_shape=jax.ShapeDtypeStruct((m, d), F32),
        compiler_params=_cparams(("parallel",)),
        name="final_norm",
    )(x, g.reshape(1, d))


def _gla_chunks(q, k, v, cum, revs, s_ref):
    c = q.shape[1]
    tot = jnp.stack([cum[j, (0 if rev else c - 1)][None] for j, rev in enumerate(revs)], axis=0)
    incl = jnp.stack([_order_mask(c, rev, False, c) for rev in revs], axis=0)
    q_dec = q * ((GLA_DK ** -0.5) * jnp.exp(cum))
    k_dec = k * jnp.exp(-cum)
    k_end = k * jnp.exp(tot - cum)
    scores = jnp.where(incl, _bdot_nt(q_dec, k_dec), 0.0)
    state = s_ref[...]
    o = _bdot(scores, v) + _bdot_nt(q_dec, state)
    s_ref[...] = jnp.exp(tot) * state + _bdot_tn(v, k_end)
    return o


def _gla_kernel(qf, kf, vf, alf, qb, kb, vb, alb, wa_ref, ba_ref, of_ref, ob_ref, s_ref):
    @pl.when(pl.program_id(1) == 0)
    def _():
        s_ref[...] = jnp.zeros_like(s_ref)

    c = CHUNK
    rows_all = qf.shape[1]
    sub_steps = rows_all // c
    cum_of = []
    for d, al_ref in enumerate((alf, alb)):
        pre = _dot3(al_ref[0], wa_ref[d]) + ba_ref[d:d + 1]
        log_a = (jnp.minimum(pre, 0.0) - jnp.log(1.0 + jnp.exp(-jnp.abs(pre)))) * (1.0 / GLA_TAU)
        cum_of.append(_mask_dot(_order_mask(rows_all, d == 1, False, c), log_a))
    for sub in range(sub_steps):
        rows_of = (slice(sub * c, (sub + 1) * c), slice((sub_steps - 1 - sub) * c, (sub_steps - sub) * c))
        qs, ks, vs, cums, revs = [], [], [], [], []
        for d, (q_ref, k_ref, v_ref) in enumerate(((qf, kf, vf), (qb, kb, vb))):
            rows = rows_of[d]
            for h in range(GLA_HEADS):
                ksl = slice(h * GLA_DK, (h + 1) * GLA_DK)
                qs.append(q_ref[0, rows, ksl])
                ks.append(k_ref[0, rows, ksl])
                vs.append(v_ref[0, rows, h * GLA_DV:(h + 1) * GLA_DV])
                cums.append(cum_of[d][rows, ksl])
                revs.append(d == 1)
        o = _gla_chunks(jnp.stack(qs), jnp.stack(ks), jnp.stack(vs), jnp.stack(cums), revs, s_ref)
        for d, o_ref in enumerate((of_ref, ob_ref)):
            for h in range(GLA_HEADS):
                o_ref[0, rows_of[d], h * GLA_DV:(h + 1) * GLA_DV] = o[d * GLA_HEADS + h].astype(o_ref.dtype)


def gla_scan(z, wa, ba):
    b, s, _ = z.shape
    c = CHUNK * SCAN_CHUNKS_PER_STEP
    n = s // c

    def fwd(off, w):
        return pl.BlockSpec((1, c, w), lambda bi, t: (bi, t, off // w))

    def bwd(off, w):
        return pl.BlockSpec((1, c, w), lambda bi, t: (bi, n - 1 - t, off // w))

    in_specs = [fwd(GQ_OFF, GLA_KW), fwd(GK_OFF, GLA_KW), fwd(GV_OFF, GLA_WIDTH), fwd(AL_OFF, LANES),
                bwd(GQ_OFF, GLA_KW), bwd(GK_OFF, GLA_KW), bwd(GV_OFF, GLA_WIDTH), bwd(AL_OFF, LANES),
                pl.BlockSpec(wa.shape, lambda bi, t: (0, 0, 0)), pl.BlockSpec(ba.shape, lambda bi, t: (0, 0))]
    out_specs = [pl.BlockSpec((1, c, GLA_WIDTH), lambda bi, t: (bi, t, 0)),
                 pl.BlockSpec((1, c, GLA_WIDTH), lambda bi, t: (bi, n - 1 - t, 0))]
    return pl.pallas_call(
        _gla_kernel,
        grid=(b, n),
        in_specs=in_specs,
        out_specs=out_specs,
        out_shape=[jax.ShapeDtypeStruct((b, s, GLA_WIDTH), BF16)] * 2,
        scratch_shapes=[pltpu.VMEM((2 * GLA_HEADS, GLA_DV, GLA_DK), F32)],
        compiler_params=_cparams(("parallel", "arbitrary")),
        name="gla_scan",
    )(z, z, z, z, z, z, z, z, wa, ba)


def _gla_post_kernel(of_ref, ob_ref, gate_ref, nw_ref, y_ref):
    o = of_ref[0].astype(F32) + ob_ref[0].astype(F32)
    ms = jnp.mean(o * o, axis=-1, keepdims=True)
    y = o * lax.rsqrt(ms + EPS) * nw_ref[...]
    y_ref[0] = (y * _silu(gate_ref[0])).astype(y_ref.dtype)


def gla_post(o_f, o_b, z, norm_w, tm):
    b, s, w = o_f.shape
    blk = pl.BlockSpec((1, tm, GLA_DV), lambda bi, t, h: (bi, t, h))
    return pl.pallas_call(
        _gla_post_kernel,
        grid=(b, s // tm, GLA_HEADS),
        in_specs=[blk, blk, pl.BlockSpec((1, tm, GLA_DV), lambda bi, t, h: (bi, t, GG_OFF // GLA_DV + h)),
                  pl.BlockSpec((1, GLA_DV), lambda bi, t, h: (0, 0))],
        out_specs=blk,
        out_shape=jax.ShapeDtypeStruct((b, s, w), BF16),
        compiler_params=_cparams(("parallel", "parallel", "parallel")),
        name="gla_post",
    )(o_f, o_b, z, norm_w.reshape(1, GLA_DV))


def _seg_sum(x, e, et):
    return _dot_mask(_dot_mask(x, e), et)


def _rwkv_prep_kernel(z_ref, zp_ref, zn_ref, y_ref, yp_ref, yn_ref, mu_ref, w0_ref, w2f_ref, w2b_ref, a0_ref, a2_ref, g2_ref,
                      kk_ref, ka_ref, rk_ref, e_ref, et_ref,
                      r_out, k_out, v_out, kk_out, b_out, cumf_out, excf_out, cumb_out, excb_out, g_out, bonus_out):
    t = pl.program_id(1)
    tm = z_ref.shape[1]
    row = _iota2((tm, 1), 0)
    first = t == 0
    last = t == pl.num_programs(1) - 1

    def shifted(lo, hi):
        cur, prv, nxt, base = (z_ref, zp_ref, zn_ref, R_OFF) if lo < GLR_OFF else (y_ref, yp_ref, yn_ref, GLR_OFF)
        z = cur[0, :, lo - base:hi - base]
        prev_row = jnp.where(first, 0.0, prv[0, 7:8, lo - base:hi - base])
        next_row = jnp.where(last, 0.0, nxt[0, 0:1, lo - base:hi - base])
        zprev = jnp.where(row == 0, prev_row, pltpu.roll(z, 1, axis=0))
        znext = jnp.where(row == tm - 1, next_row, pltpu.roll(z, tm - 1, axis=0))
        mu = mu_ref[:, lo - R_OFF:hi - R_OFF]
        return z + mu[0:1] * (zprev - z) + mu[1:2] * (znext - z)

    e = e_ref[...]
    et = et_ref[...]
    a = _sigmoid(a0_ref[...] + _bdot(shifted(ALR_OFF, ALR_OFF + LANES), a2_ref[...]))
    k = shifted(K_OFF, K_OFF + RWKV_WIDTH)
    kk_raw = k * kk_ref[...]
    kk = kk_raw * lax.rsqrt(_seg_sum(kk_raw * kk_raw, e, et) + EPS)
    kk_out[0] = kk.astype(kk_out.dtype)
    b_out[0] = (kk * a).astype(b_out.dtype)
    kmod = k * (1.0 + (a - 1.0) * ka_ref[...])
    k_out[0] = kmod.astype(k_out.dtype)
    r = shifted(R_OFF, R_OFF + RWKV_WIDTH)
    r_out[0] = r.astype(r_out.dtype)
    v = shifted(V_OFF, V_OFF + RWKV_WIDTH)
    v_out[0] = v.astype(v_out.dtype)
    bonus_out[0] = (_seg_sum(r * kmod * rk_ref[...], e, et) * v).astype(bonus_out.dtype)
    g_out[0] = _bdot(_sigmoid(shifted(GLR_OFF, GLR_OFF + GATE_LORA)), g2_ref[...]).astype(g_out.dtype)
    decay_scale = -math.exp(-0.5)
    wf = w0_ref[0:1] + _dot3(jnp.tanh(shifted(WLF_OFF, WLF_OFF + LANES)), w2f_ref[...])
    lwf = decay_scale * _sigmoid(wf)
    cum_f = _mask_dot(_order_mask(tm, False, False, CHUNK), lwf)
    cumf_out[0] = cum_f
    excf_out[0] = cum_f - lwf
    wb = w0_ref[1:2] + _dot3(jnp.tanh(shifted(WLB_OFF, WLB_OFF + LANES)), w2b_ref[...])
    lwb = decay_scale * _sigmoid(wb)
    cum_b = _mask_dot(_order_mask(tm, True, False, CHUNK), lwb)
    cumb_out[0] = cum_b
    excb_out[0] = cum_b - lwb


def rwkv_prep(z, mu, w0, w2f, w2b, a0, a2, g2, k_k, k_a, r_k, seg_e, seg_et, tm):
    b, s, _ = z.shape
    nt = s // tm
    hb = tm // 8
    full = lambda arr: pl.BlockSpec(arr.shape, lambda bi, t: (0,) * arr.ndim)
    in_specs = []
    for width, off in ((RWKV_MAIN, R_OFF), (RWKV_SMALL, GLR_OFF)):
        cb = off // width
        in_specs += [pl.BlockSpec((1, tm, width), lambda bi, t, cb=cb: (bi, t, cb)),
                     pl.BlockSpec((1, 8, width), lambda bi, t, cb=cb: (bi, jnp.maximum(t * hb - 1, 0), cb)),
                     pl.BlockSpec((1, 8, width), lambda bi, t, cb=cb: (bi, jnp.minimum((t + 1) * hb, nt * hb - 1), cb))]
    params = [mu, w0, w2f, w2b, a0, a2, g2, k_k, k_a, r_k, seg_e, seg_et]
    in_specs += [full(p) for p in params]
    out_blk = pl.BlockSpec((1, tm, RWKV_WIDTH), lambda bi, t: (bi, t, 0))
    return pl.pallas_call(
        _rwkv_prep_kernel,
        grid=(b, nt),
        in_specs=in_specs,
        out_specs=[out_blk] * 11,
        out_shape=[jax.ShapeDtypeStruct((b, s, RWKV_WIDTH), dt) for dt in [BF16] * 5 + [F32] * 4 + [BF16] * 2],
        compiler_params=_cparams(("parallel", "parallel")),
        name="rwkv_prep",
    )(z, z, z, z, z, z, *params)


def _stack_heads(x):
    lane = _iota2(x.shape, 2)
    return jnp.concatenate([jnp.where(lane < RWKV_HEAD, x, 0.0), jnp.where(lane < RWKV_HEAD, 0.0, x)], axis=1)


def _rwkv_chunks(r, k, v, kk, bb, cum, exc, revs, g_ref):
    nb, c, _ = r.shape
    n2 = 2 * c
    tot = jnp.stack([cum[j, (0 if rev else c - 1)][None] for j, rev in enumerate(revs)], axis=0)
    e_neg = jnp.exp(-cum)
    e_end = jnp.exp(tot - cum)
    al2 = _stack_heads(-kk * jnp.exp(exc))
    rb2 = _stack_heads(r * jnp.exp(cum))
    bt2 = _stack_heads(bb * e_neg)
    kt2 = _stack_heads(k * e_neg)
    be2 = _stack_heads(bb * e_end)
    ke2 = _stack_heads(k * e_end)
    v2 = _stack_heads(v)
    gram = _bdot_nt(jnp.concatenate([al2, rb2], axis=1), jnp.concatenate([bt2, kt2], axis=1))
    strict = jnp.stack([_order_mask(n2, rev, True, c) for rev in revs], axis=0)
    incl = jnp.stack([_order_mask(n2, rev, False, c) for rev in revs], axis=0)
    a_ab = jnp.where(strict, gram[:, :n2, :n2], 0.0)
    a_ak = jnp.where(strict, gram[:, :n2, n2:], 0.0)
    a_rb = jnp.where(incl, gram[:, n2:, :n2], 0.0)
    a_rk = jnp.where(incl, gram[:, n2:, n2:], 0.0)
    t_inv = _neumann_inverse(a_ab)
    wu = _bdot(t_inv, jnp.concatenate([al2, _bdot(a_ak, v2)], axis=2))
    g = g_ref[...]
    proj = _bdot_nt(jnp.concatenate([wu[:, :, :LANES], rb2], axis=1), g)
    u2 = proj[:, :n2] + wu[:, :, LANES:]
    uv = jnp.concatenate([u2, v2], axis=1)
    o2 = proj[:, n2:] + _bdot(jnp.concatenate([a_rb, a_rk], axis=2), uv)
    g_ref[...] = g * jnp.exp(tot) + _bdot_tn(uv, jnp.concatenate([be2, ke2], axis=1))
    return o2[:, :c] + o2[:, c:]


def _rwkv_scan_kernel(*refs, pairs):
    fwd_refs, bwd_refs = refs[0:7], refs[7:14]
    of_ref, ob_ref, g_ref = refs[14:17]

    @pl.when(pl.program_id(2) == 0)
    def _():
        g_ref[...] = jnp.zeros_like(g_ref)

    revs = [False] * pairs + [True] * pairs
    c = CHUNK
    sub_steps = of_ref.shape[1] // c
    for sub in range(sub_steps):
        rows_of = (slice(sub * c, (sub + 1) * c), slice((sub_steps - 1 - sub) * c, (sub_steps - sub) * c))
        operands = []
        for t in range(7):
            operands.append(jnp.stack([ref[0, rows_of[d], p * LANES:(p + 1) * LANES].astype(F32)
                                       for d, ref in enumerate((fwd_refs[t], bwd_refs[t])) for p in range(pairs)],
                                      axis=0))
        o = _rwkv_chunks(*operands, revs, g_ref)
        for d, o_ref in enumerate((of_ref, ob_ref)):
            for p in range(pairs):
                o_ref[0, rows_of[d], p * LANES:(p + 1) * LANES] = o[d * pairs + p].astype(o_ref.dtype)


def rwkv_scan(r, k, v, kk, bb, cum_f, exc_f, cum_b, exc_b, pairs):
    b, s, _ = r.shape
    c = CHUNK * SCAN_CHUNKS_PER_STEP
    n = s // c
    w = pairs * LANES
    fwd = pl.BlockSpec((1, c, w), lambda bi, p, t: (bi, t, p))
    bwd = pl.BlockSpec((1, c, w), lambda bi, p, t: (bi, n - 1 - t, p))
    return pl.pallas_call(
        functools.partial(_rwkv_scan_kernel, pairs=pairs),
        grid=(b, RWKV_WIDTH // w, n),
        in_specs=[fwd] * 7 + [bwd] * 7,
        out_specs=[fwd, bwd],
        out_shape=[jax.ShapeDtypeStruct((b, s, RWKV_WIDTH), BF16)] * 2,
        scratch_shapes=[pltpu.VMEM((2 * pairs, LANES, LANES), F32)],
        compiler_params=_cparams(("parallel", "parallel", "arbitrary")),
        name="rwkv_scan",
    )(r, k, v, kk, bb, cum_f, exc_f, r, k, v, kk, bb, cum_b, exc_b)


def _rwkv_post_kernel(of_ref, ob_ref, bonus_ref, g_ref, lnw_ref, lnb_ref, e_ref, et_ref, y_ref):
    e = e_ref[...]
    et = et_ref[...]
    o = of_ref[0].astype(F32) + ob_ref[0].astype(F32)
    mean = _seg_sum(o, e, et) * (1.0 / RWKV_HEAD)
    cen = o - mean
    var = _seg_sum(cen * cen, e, et) * (1.0 / RWKV_HEAD)
    y = cen * lax.rsqrt(var + RWKV_LN_EPS) * lnw_ref[...] + lnb_ref[...]
    y_ref[0] = ((y + bonus_ref[0].astype(F32)) * g_ref[0].astype(F32)).astype(y_ref.dtype)


def rwkv_post(o_f, o_b, bonus, g, ln_w, ln_b, seg_e, seg_et, tm):
    b, s, w = o_f.shape
    blk = pl.BlockSpec((1, tm, w), lambda bi, t: (bi, t, 0))
    full = lambda arr: pl.BlockSpec(arr.shape, lambda bi, t: (0,) * arr.ndim)
    params = [ln_w.reshape(1, w), ln_b.reshape(1, w), seg_e, seg_et]
    return pl.pallas_call(
        _rwkv_post_kernel,
        grid=(b, s // tm),
        in_specs=[blk] * 4 + [full(p) for p in params],
        out_specs=blk,
        out_shape=jax.ShapeDtypeStruct((b, s, w), BF16),
        compiler_params=_cparams(("parallel", "parallel")),
        name="rwkv_post",
    )(o_f, o_b, bonus, g, *params)


def _gdn_conv_kernel(z_ref, zp_ref, zn_ref, cw_ref, sc_ref, o_ref, *, l2norm):
    t = pl.program_id(1)
    tm = z_ref.shape[1]
    halo = zp_ref.shape[1]
    ext = jnp.concatenate([jnp.where(t == 0, 0.0, zp_ref[0].astype(F32)), z_ref[0].astype(F32),
                           jnp.where(t == pl.num_programs(1) - 1, 0.0, zn_ref[0].astype(F32))], axis=0)
    n = tm + 2 * halo
    pad = cw_ref.shape[0] // 2
    y = 0.0
    for j in range(cw_ref.shape[0]):
        shifted = ext if j == pad else pltpu.roll(ext, (pad - j) % n, axis=0)
        y = y + cw_ref[j:j + 1] * shifted[halo:halo + tm]
    y = _silu(y)
    if l2norm:
        for h in range(y.shape[1] // GDN_DK):
            sl = slice(h * GDN_DK, (h + 1) * GDN_DK)
            yh = y[:, sl]
            ss = jnp.sum(yh * yh, axis=-1, keepdims=True)
            o_ref[0, :, sl] = (yh * lax.rsqrt(ss + EPS) * sc_ref[0:1, sl]).astype(o_ref.dtype)
    else:
        o_ref[0] = y.astype(o_ref.dtype)


def gdn_conv(z, conv_w, scale, col_off, width, l2norm, tm, cw):
    b, s, _ = z.shape
    nt = s // tm
    halo = 8 * (4 // z.dtype.itemsize)
    hb = tm // halo
    cb = col_off // cw
    in_specs = [pl.BlockSpec((1, tm, cw), lambda bi, t, c: (bi, t, cb + c)),
                pl.BlockSpec((1, halo, cw), lambda bi, t, c: (bi, jnp.maximum(t * hb - 1, 0), cb + c)),
                pl.BlockSpec((1, halo, cw), lambda bi, t, c: (bi, jnp.minimum((t + 1) * hb, nt * hb - 1), cb + c)),
                pl.BlockSpec((conv_w.shape[0], cw), lambda bi, t, c: (0, cb + c)),
                pl.BlockSpec((1, cw), lambda bi, t, c: (0, c))]
    return pl.pallas_call(
        functools.partial(_gdn_conv_kernel, l2norm=l2norm),
        grid=(b, nt, width // cw),
        in_specs=in_specs,
        out_specs=pl.BlockSpec((1, tm, cw), lambda bi, t, c: (bi, t, c)),
        out_shape=jax.ShapeDtypeStruct((b, s, width), BF16),
        compiler_params=_cparams(("parallel", "parallel", "parallel")),
        name="gdn_conv_norm" if l2norm else "gdn_conv",
    )(z, z, z, conv_w, scale)


def _gdn_gates_kernel(zs_ref, na_ref, dtb_ref, o_ref):
    zs = zs_ref[0]
    tm = zs.shape[0]
    gg = na_ref[...] * _softplus(zs + dtb_ref[...])
    gam_f = _mask_dot(_order_mask(tm, False, False, CHUNK), gg)
    gam_b = _mask_dot(_order_mask(tm, True, False, CHUNK), gg)
    lane = _iota2(zs.shape, 1)
    o_ref[0] = jnp.where(lane < 2 * GDN_VHEADS, _sigmoid(zs), jnp.where(lane < 3 * GDN_VHEADS, gam_f, gam_b))


def gdn_gates(zs, neg_a, dtb, tm):
    b, s, w = zs.shape
    blk = pl.BlockSpec((1, tm, w), lambda bi, t: (bi, t, 0))
    row = pl.BlockSpec((1, w), lambda bi, t: (0, 0))
    return pl.pallas_call(
        _gdn_gates_kernel,
        grid=(b, s // tm),
        in_specs=[blk, row, row],
        out_specs=blk,
        out_shape=jax.ShapeDtypeStruct((b, s, w), F32),
        compiler_params=_cparams(("parallel", "parallel")),
        name="gdn_gates",
    )(zs, neg_a, dtb)


def _gdn_chunks(q, k, v2, beta2, gam2, revs, s_ref):
    nb, c, _ = q.shape
    n2 = 2 * c
    per_batch = lambda fn: jnp.stack([fn(r) for r in revs], axis=0)
    incl = per_batch(lambda r: _order_mask(n2, r, False, c))
    strict = per_batch(lambda r: _order_mask(n2, r, True, c))
    gam_c = jnp.broadcast_to(gam2, (nb, n2, n2))
    gam_r = jnp.swapaxes(gam_c, 1, 2)
    tot2 = jnp.stack([jnp.concatenate(
        [jnp.broadcast_to(gam2[j, h * c + (0 if r else c - 1)][None], (c, 1)) for h in range(2)], axis=0)
        for j, r in enumerate(revs)], axis=0)
    diff = gam_c - gam_r
    decay = jnp.exp(jnp.where(incl, diff, 0.0))
    dec_s = jnp.where(strict, decay, 0.0)
    dec_i = jnp.where(incl, decay, 0.0)
    k2 = jnp.concatenate([k, k], axis=1)
    q2 = jnp.concatenate([q, q], axis=1)
    gram = _bdot_nt(jnp.concatenate([k2, q2], axis=1), k2)
    a_mat = gram[:, :n2] * beta2 * dec_s
    qk = gram[:, n2:] * dec_i
    t_inv = _neumann_inverse(a_mat, negate=True)
    e_gam = jnp.exp(gam2)
    uw = _bdot(t_inv, jnp.concatenate([v2 * beta2, k2 * (beta2 * e_gam)], axis=2))
    u2 = uw[:, :, :GDN_DV]
    w2 = uw[:, :, GDN_DV:]
    qd2 = q2 * e_gam
    ke2 = k2 * jnp.exp(tot2 - gam2)
    dl2 = jnp.exp(tot2)
    ws, qs = [], []
    for e in range(2):
        rows = slice(e * c, (e + 1) * c)
        both = _bdot(jnp.concatenate([w2[:, rows], qd2[:, rows]], axis=1), s_ref[e])
        ws.append(both[:, :c])
        qs.append(both[:, c:])
    vnew2 = u2 - jnp.concatenate(ws, axis=1)
    o2 = jnp.concatenate(qs, axis=1) + _bdot(qk, vnew2)
    for e in range(2):
        rows = slice(e * c, (e + 1) * c)
        s_ref[e] = s_ref[e] * dl2[:, e * c:e * c + 1, :] + _bdot_tn(ke2[:, rows], vnew2[:, rows])
    return o2


def _gdn_scan_kernel(qf, kf, vf, gf, qb, kb, vb, gb, of_ref, ob_ref, s_ref, *, heads):
    @pl.when(pl.program_id(2) == 0)
    def _():
        s_ref[...] = jnp.zeros_like(s_ref)

    c = CHUNK
    sub_steps = qf.shape[1] // c
    lane = _iota2((c, LANES), 1)

    def column(gates, idx):
        return jnp.sum(jnp.where(lane == idx, gates, 0.0), axis=1, keepdims=True)

    for sub in range(sub_steps):
        rows_of = (slice(sub * c, (sub + 1) * c), slice((sub_steps - 1 - sub) * c, (sub_steps - sub) * c))
        qs, ks, vs, betas, gams, revs = [], [], [], [], [], []
        for d, (q_ref, k_ref, v_ref, g_ref) in enumerate(((qf, kf, vf, gf), (qb, kb, vb, gb))):
            rows = rows_of[d]
            gates = g_ref[0, rows, :]
            for i in range(heads):
                vh = 2 * (pl.program_id(1) * heads + i)
                ksl = slice(i * GDN_DK, (i + 1) * GDN_DK)
                qs.append(q_ref[0, rows, ksl].astype(F32))
                ks.append(k_ref[0, rows, ksl].astype(F32))
                vs.append(jnp.concatenate(
                    [v_ref[0, rows, (2 * i + e) * GDN_DV:(2 * i + e + 1) * GDN_DV].astype(F32) for e in range(2)], axis=0))
                betas.append(jnp.concatenate([column(gates, d * GDN_VHEADS + vh + e) for e in range(2)], axis=0))
                gams.append(jnp.concatenate([column(gates, (2 + d) * GDN_VHEADS + vh + e) for e in range(2)], axis=0))
                revs.append(d == 1)
        o2 = _gdn_chunks(jnp.stack(qs), jnp.stack(ks), jnp.stack(vs), jnp.stack(betas), jnp.stack(gams), revs, s_ref)
        for d, o_ref in enumerate((of_ref, ob_ref)):
            for i in range(heads):
                j = d * heads + i
                for e in range(2):
                    o_ref[0, rows_of[d], (2 * i + e) * GDN_DV:(2 * i + e + 1) * GDN_DV] = (
                        o2[j, e * c:(e + 1) * c].astype(o_ref.dtype))


def gdn_scan(qk, v, gates, heads):
    b, s, _ = v.shape
    c = CHUNK * SCAN_CHUNKS_PER_STEP
    n = s // c
    ng = GDN_KHEADS // heads

    def spec(w, off, rev):
        if rev:
            return pl.BlockSpec((1, c, w), lambda bi, h, t: (bi, n - 1 - t, off + h))
        return pl.BlockSpec((1, c, w), lambda bi, h, t: (bi, t, off + h))

    def gspec(rev):
        if rev:
            return pl.BlockSpec((1, c, LANES), lambda bi, h, t: (bi, n - 1 - t, 0))
        return pl.BlockSpec((1, c, LANES), lambda bi, h, t: (bi, t, 0))

    kw = heads * GDN_DK
    vw = heads * 2 * GDN_DV
    in_specs = [spec(kw, 0, False), spec(kw, ng, False), spec(vw, 0, False), gspec(False),
                spec(kw, 0, True), spec(kw, ng, True), spec(vw, 0, True), gspec(True)]
    return pl.pallas_call(
        functools.partial(_gdn_scan_kernel, heads=heads),
        grid=(b, ng, n),
        in_specs=in_specs,
        out_specs=[spec(vw, 0, False), spec(vw, 0, True)],
        out_shape=[jax.ShapeDtypeStruct((b, s, GDN_VW), BF16)] * 2,
        scratch_shapes=[pltpu.VMEM((2, 2 * heads, GDN_DK, GDN_DV), F32)],
        compiler_params=_cparams(("parallel", "parallel", "arbitrary")),
        name="gdn_scan",
    )(qk, qk, v, gates, qk, qk, v, gates)


def _gdn_post_kernel(of_ref, ob_ref, gate_ref, nw_ref, y_ref):
    nw = nw_ref[...]
    for h in range(of_ref.shape[2] // GDN_DV):
        sl = slice(h * GDN_DV, (h + 1) * GDN_DV)
        o = of_ref[0, :, sl].astype(F32) + ob_ref[0, :, sl].astype(F32)
        ms = jnp.mean(o * o, axis=-1, keepdims=True)
        y = o * lax.rsqrt(ms + EPS) * nw
        y_ref[0, :, sl] = (y * _silu(gate_ref[0, :, sl].astype(F32))).astype(y_ref.dtype)


def gdn_post(o_f, o_b, z, gate_off, norm_w, tm, cw):
    b, s, w = o_f.shape
    blk = pl.BlockSpec((1, tm, cw), lambda bi, t, c: (bi, t, c))
    gb = gate_off // cw
    return pl.pallas_call(
        _gdn_post_kernel,
        grid=(b, s // tm, w // cw),
        in_specs=[blk, blk, pl.BlockSpec((1, tm, cw), lambda bi, t, c: (bi, t, gb + c)),
                  pl.BlockSpec((1, GDN_DV), lambda bi, t, c: (0, 0))],
        out_specs=blk,
        out_shape=jax.ShapeDtypeStruct((b, s, w), BF16),
        compiler_params=_cparams(("parallel", "parallel", "parallel")),
        name="gdn_post",
    )(o_f, o_b, z, norm_w.reshape(1, GDN_DV))


def _moe_router_kernel(x_ref, g_ref, w_ref, b_ref, route_ref, counts_ref):
    @pl.when(pl.program_id(0) == 0)
    def _():
        counts_ref[...] = jnp.zeros_like(counts_ref)

    x = x_ref[...]
    tm = x.shape[0]
    ms = jnp.mean(x * x, axis=-1, keepdims=True)
    h = x * lax.rsqrt(ms + EPS) * g_ref[...]
    lane_i = _iota2((tm, LANES), 1)
    lane = lane_i.astype(F32)
    lane_grp = lax.shift_right_logical(lane_i, 3).astype(F32)
    neg = -jnp.inf
    logits = _dot3(h, w_ref[...]) + b_ref[...]
    gl = jnp.where((lane_i >= N_EXPERTS) & (lane_i < N_EXPERTS + N_GROUPS), logits, neg)
    gmax = jnp.max(gl, axis=-1, keepdims=True)
    gidx = jnp.min(jnp.where(gl == gmax, lane, float(LANES)), axis=-1, keepdims=True) - float(N_EXPERTS)
    grp_w = 1.0 / jnp.sum(jnp.exp(gl - gmax), axis=-1, keepdims=True)
    sel = jnp.where((lane_i < N_EXPERTS) & (lane_grp == gidx), logits, neg)
    m1 = jnp.max(sel, axis=-1, keepdims=True)
    i1 = jnp.min(jnp.where(sel == m1, lane, float(LANES)), axis=-1, keepdims=True)
    sel2 = jnp.where(lane == i1, neg, sel)
    m2 = jnp.max(sel2, axis=-1, keepdims=True)
    i2 = jnp.min(jnp.where(sel2 == m2, lane, float(LANES)), axis=-1, keepdims=True)
    e2 = jnp.exp(m2 - m1)
    w1 = grp_w / (1.0 + e2)
    w2 = grp_w * e2 / (1.0 + e2)
    hits = jnp.where((lane == i1) | (lane == i2), 1.0, 0.0)
    before = counts_ref[...] + _bdot(_order_mask(tm, False, True, tm), hits)
    r1 = jnp.sum(jnp.where(lane == i1, before, 0.0), axis=-1, keepdims=True)
    r2 = jnp.sum(jnp.where(lane == i2, before, 0.0), axis=-1, keepdims=True)
    counts_ref[...] += jnp.sum(hits, axis=0, keepdims=True)
    route_ref[...] = jnp.where(lane_i == 0, i1, jnp.where(lane_i == 1, i2, jnp.where(
        lane_i == 2, w1, jnp.where(lane_i == 3, w2, jnp.where(lane_i == 4, r1, jnp.where(lane_i == 5, r2, 0.0))))))


def moe_router(x, g, w_group, b_group, w_router, b_router, tm):
    m, d = x.shape
    tm = min(tm, m)
    pad = lambda w: jnp.pad(w, ((0, 0), (0, LANES - w.shape[1])))
    full = lambda arr: pl.BlockSpec(arr.shape, lambda i: (0, 0))
    params = [g.reshape(1, d), pad(jnp.concatenate([w_router, w_group], axis=1)),
              pad(jnp.concatenate([b_router, b_group]).reshape(1, -1))]
    return pl.pallas_call(
        _moe_router_kernel,
        grid=(m // tm,),
        in_specs=[pl.BlockSpec((tm, d), lambda i: (i, 0))] + [full(p) for p in params],
        out_specs=[pl.BlockSpec((tm, LANES), lambda i: (i, 0)), pl.BlockSpec((1, LANES), lambda i: (0, 0))],
        out_shape=[jax.ShapeDtypeStruct((m, LANES), F32), jax.ShapeDtypeStruct((1, LANES), F32)],
        compiler_params=_cparams(("arbitrary",)),
        name="moe_router",
    )(x, *params)


def _route_tables(route, counts, tm, n_tiles):
    counts = counts[0, :N_EXPERTS].astype(jnp.int32)
    padded = ((counts + tm - 1) // tm) * tm
    ends = jnp.cumsum(padded)
    starts = ends - padded
    experts = route[:, 0:2].astype(jnp.int32)
    pos = (starts[experts] + route[:, 4:6].astype(jnp.int32)).reshape(-1)
    tile_start = jnp.arange(n_tiles, dtype=jnp.int32) * tm
    tile_e = jnp.minimum(jnp.sum(tile_start[:, None] >= ends[None, :], axis=1), N_EXPERTS - 1).astype(jnp.int32)
    n_valid = (ends[-1:] // tm).astype(jnp.int32)
    fill = jnp.stack([jnp.maximum(ends - tm, 0), (padded > 0).astype(jnp.int32)]).astype(jnp.int32)
    return pos, tile_e, n_valid, fill


def _moe_dispatch_kernel(pos_ref, nvalid_ref, fill_ref, x_ref, g_ref, hs_hbm, hbuf, zbuf, sem_rows, sem_fill, *,
                         n_tiles):
    i = pl.program_id(0)
    tm = x_ref.shape[0]
    row_tile = zbuf.shape[0]

    @pl.when(i == 0)
    def _():
        zbuf[...] = jnp.zeros_like(zbuf)
        fills = [(fill_ref[1, e] > 0, fill_ref[0, e]) for e in range(N_EXPERTS)]
        fills += [(nvalid_ref[0] + k < n_tiles, (nvalid_ref[0] + k) * row_tile) for k in range(N_EXPERTS)]
        fills = [(cond, pl.multiple_of(start, row_tile)) for cond, start in fills]
        for cond, start in fills:
            @pl.when(cond)
            def _(start=start):
                pltpu.make_async_copy(zbuf, hs_hbm.at[pl.ds(start, row_tile)], sem_fill).start()
        for cond, start in fills:
            @pl.when(cond)
            def _(start=start):
                pltpu.make_async_copy(zbuf, hs_hbm.at[pl.ds(start, row_tile)], sem_fill).wait()

    buf = i % 2
    x = x_ref[...]
    ms = jnp.mean(x * x, axis=-1, keepdims=True)
    hbuf[buf] = x * lax.rsqrt(ms + EPS) * g_ref[...]

    def issue(r, carry):
        base = 2 * (i * tm + r)
        for slot in range(2):
            pltpu.make_async_copy(hbuf.at[buf, pl.ds(r, 1)], hs_hbm.at[pl.ds(pos_ref[base + slot], 1)],
                                  sem_rows.at[buf]).start(priority=slot)
        return carry

    lax.fori_loop(0, tm, issue, 0, unroll=8)

    def drain(b):
        for _ in range(2):
            pltpu.make_async_copy(hbuf.at[b], hs_hbm.at[pl.ds(0, tm)], sem_rows.at[b]).wait()

    @pl.when(i > 0)
    def _():
        drain(1 - buf)

    @pl.when(i == pl.num_programs(0) - 1)
    def _():
        drain(buf)


def moe_dispatch(x, g, pos, n_valid, fill, n_tiles, tm):
    m, d = x.shape
    tm_x = min(256, m)
    return pl.pallas_call(
        functools.partial(_moe_dispatch_kernel, n_tiles=n_tiles),
        grid_spec=pltpu.PrefetchScalarGridSpec(
            num_scalar_prefetch=3,
            grid=(m // tm_x,),
            in_specs=[pl.BlockSpec((tm_x, d), lambda i, p, nv, fl: (i, 0)),
                      pl.BlockSpec((1, d), lambda i, p, nv, fl: (0, 0))],
            out_specs=pl.BlockSpec(memory_space=pl.ANY),
            scratch_shapes=[pltpu.VMEM((2, tm_x, d), F32), pltpu.VMEM((tm, d), F32),
                            pltpu.SemaphoreType.DMA((2,)), pltpu.SemaphoreType.DMA(())]),
        out_shape=jax.ShapeDtypeStruct((n_tiles * tm, d), F32),
        compiler_params=_cparams(("arbitrary",)),
        name="moe_dispatch",
    )(pos, n_valid, fill, x, g.reshape(1, d))


def _moe_ffn_kernel(tile_e_ref, nvalid_ref, h_ref, wg_ref, wu_ref, wd_ref, y_ref, wg_b, wu_b, wd_b):
    i = pl.program_id(0)

    @pl.when((i == 0) | (tile_e_ref[i] != tile_e_ref[jnp.maximum(i - 1, 0)]))
    def _():
        wg_b[...] = wg_ref[0, 0].astype(BF16)
        wu_b[...] = wu_ref[0, 0].astype(BF16)
        wd_b[...] = wd_ref[0, 0].astype(BF16)

    @pl.when(i < nvalid_ref[0])
    def _():
        x = h_ref[...].astype(BF16)
        hid = _silu(_dot(x, wg_b[...])) * _dot(x, wu_b[...])
        y_ref[...] = _dot(hid.astype(BF16), wd_b[...])

    @pl.when(i >= nvalid_ref[0])
    def _():
        y_ref[...] = jnp.zeros_like(y_ref)


def moe_ffn(h_sorted, tile_e, n_valid, w_gate, w_up, w_down, layer, tm):
    _, d = h_sorted.shape
    n_tiles = tile_e.shape[0]
    ff = w_gate.shape[-1]
    wspec = lambda shape: pl.BlockSpec((1, 1) + shape, lambda i, te, nv: (layer, te[i], 0, 0))
    return pl.pallas_call(
        _moe_ffn_kernel,
        grid_spec=pltpu.PrefetchScalarGridSpec(
            num_scalar_prefetch=2,
            grid=(n_tiles,),
            in_specs=[pl.BlockSpec((tm, d), lambda i, te, nv: (i, 0)), wspec((d, ff)), wspec((d, ff)), wspec((ff, d))],
            out_specs=pl.BlockSpec((tm, d), lambda i, te, nv: (i, 0)),
            scratch_shapes=[pltpu.VMEM((d, ff), BF16), pltpu.VMEM((d, ff), BF16), pltpu.VMEM((ff, d), BF16)]),
        out_shape=jax.ShapeDtypeStruct((n_tiles * tm, d), F32),
        compiler_params=_cparams(("arbitrary",)),
        name="moe_ffn",
    )(tile_e, n_valid, h_sorted, w_gate, w_up, w_down)


def _moe_combine_kernel(pos_ref, x_ref, route_ref, y_hbm, g_ref, o_ref, ybuf, sems, *, final_norm):
    i = pl.program_id(0)
    tm = x_ref.shape[0]

    def gather(tile, buf):
        def issue(r, carry):
            base = 2 * (tile * tm + r)
            for slot in range(2):
                pltpu.make_async_copy(y_hbm.at[pl.ds(pos_ref[base + slot], 1)], ybuf.at[buf, slot, pl.ds(r, 1)],
                                      sems.at[buf]).start(priority=slot)
            return carry

        lax.fori_loop(0, tm, issue, 0, unroll=8)

    @pl.when(i == 0)
    def _():
        gather(0, 0)

    buf = i % 2

    @pl.when(i + 1 < pl.num_programs(0))
    def _():
        gather(i + 1, 1 - buf)

    for slot in range(2):
        pltpu.make_async_copy(y_hbm.at[pl.ds(0, tm)], ybuf.at[buf, slot], sems.at[buf]).wait()
    route = route_ref[...]
    o = x_ref[...] + route[:, 2:3] * ybuf[buf, 0] + route[:, 3:4] * ybuf[buf, 1]
    if final_norm:
        ms = jnp.mean(o * o, axis=-1, keepdims=True)
        o = o * lax.rsqrt(ms + EPS) * g_ref[...]
    o_ref[...] = o


def moe_combine(x, route, y_sorted, pos, norm_g, tm):
    m, d = x.shape
    tm = min(tm, m)
    final_norm = norm_g is not None
    g = (norm_g if final_norm else jnp.ones((d,), F32)).reshape(1, d)
    return pl.pallas_call(
        functools.partial(_moe_combine_kernel, final_norm=final_norm),
        grid_spec=pltpu.PrefetchScalarGridSpec(
            num_scalar_prefetch=1,
            grid=(m // tm,),
            in_specs=[pl.BlockSpec((tm, d), lambda i, p: (i, 0)), pl.BlockSpec((tm, LANES), lambda i, p: (i, 0)),
                      pl.BlockSpec(memory_space=pl.ANY), pl.BlockSpec((1, d), lambda i, p: (0, 0))],
            out_specs=pl.BlockSpec((tm, d), lambda i, p: (i, 0)),
            scratch_shapes=[pltpu.VMEM((2, 2, tm, d), F32), pltpu.SemaphoreType.DMA((2,))]),
        out_shape=jax.ShapeDtypeStruct((m, d), F32),
        compiler_params=_cparams(("arbitrary",)),
        name="moe_combine",
    )(pos, x, route, y_sorted, g)


def _pad_rows(w, rows):
    return jnp.pad(w, ((0, rows - w.shape[0]), (0, 0)))


def _even_layer(x, norm_g, w_in, gla_w_alpha, gla_b_alpha, gla_norm, mu, w0, w2, a0, a2, g2,
                k_k, k_a, r_k, ln_w, ln_b, w_out):
    b, s, d = x.shape
    gla_cols, rw = w_in[:, :3104], w_in[:, 3104:]
    mu_g = lambda lo, hi, width: jnp.pad(mu[:, lo:hi], ((0, 0), (0, width - (hi - lo))))
    pad_c = lambda w, width: jnp.pad(w, ((0, 0), (0, width - w.shape[1])))
    w_cat = jnp.concatenate([
        gla_cols[:, 0:3072],
        rw[:, 0:3072], rw[:, 3360:3616],
        pad_c(rw[:, 3072:3168], LANES), pad_c(rw[:, 3168:3264], LANES), pad_c(rw[:, 3264:3360], LANES),
        pad_c(gla_cols[:, 3072:3104], LANES)], axis=1).astype(BF16)
    mu_cat = jnp.concatenate([
        mu[:, 0:3072], mu[:, 3360:3616], mu_g(3072, 3168, LANES), mu_g(3168, 3264, LANES),
        mu_g(3264, 3360, LANES)], axis=1)
    z = norm_matmul(x.reshape(b * s, d), norm_g, w_cat, 1024, 1152).reshape(b, s, EVEN_PAD)

    wa = jnp.stack([_pad_rows(gla_w_alpha[0], LANES),
                    jnp.pad(gla_w_alpha[1], ((GLA_LOWRANK, LANES - 2 * GLA_LOWRANK), (0, 0)))])
    gla_f, gla_b = gla_scan(z, wa, gla_b_alpha)
    y_gla = gla_post(gla_f, gla_b, z, gla_norm, 256)

    head_of_lane = jnp.arange(RWKV_WIDTH) // RWKV_HEAD
    seg_e = (head_of_lane[:, None] == jnp.arange(LANES)[None, :]).astype(F32)
    seg_et = seg_e.T
    row = lambda p: p.reshape(1, RWKV_WIDTH)
    r, k, v, kk, bb, cum_f, exc_f, cum_b, exc_b, g, bonus = rwkv_prep(
        z, mu_cat, w0, _pad_rows(w2[0], LANES), _pad_rows(w2[1], LANES), row(a0), _pad_rows(a2, LANES), g2,
        row(k_k), row(k_a), row(r_k), seg_e, seg_et, 256)
    rw_f, rw_b = rwkv_scan(r, k, v, kk, bb, cum_f, exc_f, cum_b, exc_b, RWKV_PAIRS_PER_STEP)
    y_rwkv = rwkv_post(rw_f, rw_b, bonus, g, ln_w, ln_b, seg_e, seg_et, 256)

    w_out = w_out.astype(BF16)
    out = proj_residual([y_gla.reshape(b * s, GLA_WIDTH), y_rwkv.reshape(b * s, RWKV_WIDTH)],
                        [w_out[:GLA_WIDTH], w_out[GLA_WIDTH:]], x.reshape(b * s, d), 1024, 1024)
    return out.reshape(b, s, d)


def _odd_layer(x, norm_g, w_in, conv_w, a_log, dt_bias, norm_w, w_out):
    b, s, d = x.shape
    x2 = x.reshape(b * s, d)
    main = GDN_QKV + GDN_VW
    z = norm_matmul(x2, norm_g, w_in[:, :main].astype(BF16), 1024, 2048, BF16).reshape(b, s, main)
    zs = norm_matmul(x2, norm_g, w_in[:, main:].astype(BF16), 512, LANES).reshape(b, s, LANES)
    scale = jnp.concatenate([jnp.full((1, GDN_KW), GDN_DK ** -0.5, F32), jnp.ones((1, GDN_KW), F32)], axis=1)
    qk = gdn_conv(z, conv_w, scale, 0, 2 * GDN_KW, True, 256, 1024)
    v = gdn_conv(z, conv_w, scale, 2 * GDN_KW, GDN_VW, False, 256, 1024)
    zero = jnp.zeros((2 * GDN_VHEADS,), F32)
    neg_a = jnp.concatenate([zero, -jnp.exp(a_log.reshape(-1))]).reshape(1, LANES)
    dtb = jnp.concatenate([zero, dt_bias.reshape(-1)]).reshape(1, LANES)
    gates = gdn_gates(zs, neg_a, dtb, 256)
    o_f, o_b = gdn_scan(qk, v, gates, GDN_HEADS_PER_STEP)
    y = gdn_post(o_f, o_b, z, GDN_QKV, norm_w, 256, 1024)
    out = proj_residual([y.reshape(b * s, GDN_VW)], [w_out.astype(BF16)], x2, 1024, 1024)
    return out.reshape(b, s, d)


def _moe_layer(x, norm_g, w_group, b_group, w_router, b_router, w_gate, w_up, w_down, layer, final_g):
    b, s, d = x.shape
    x2 = x.reshape(b * s, d)
    route, counts = moe_router(x2, norm_g, w_group, b_group, w_router, b_router, 256)
    n_tiles = -(-2 * b * s // MOE_ROW_TILE) + N_EXPERTS
    pos, tile_e, n_valid, fill = _route_tables(route, counts, MOE_ROW_TILE, n_tiles)
    h_sorted = moe_dispatch(x2, norm_g, pos, n_valid, fill, n_tiles, MOE_ROW_TILE)
    y_sorted = moe_ffn(h_sorted, tile_e, n_valid, w_gate, w_up, w_down, layer, MOE_ROW_TILE)
    out = moe_combine(x2, route, y_sorted, pos, final_g, 256)
    return out.reshape(b, s, d)


def kernel(x, norm_mix, norm_ffn, norm_final, ev_w_in, ev_gla_w_alpha, ev_gla_b_alpha, ev_gla_norm, ev_rwkv_mu, ev_rwkv_w0, ev_rwkv_w2, ev_rwkv_a0, ev_rwkv_a2, ev_rwkv_g2, ev_rwkv_k_k, ev_rwkv_k_a, ev_rwkv_r_k, ev_rwkv_ln_w, ev_rwkv_ln_b, ev_w_out, od_w_in, od_conv, od_a_log, od_dt_bias, od_norm, od_w_out, moe_w_group, moe_b_group, moe_w_router, moe_b_router, moe_w_gate, moe_w_up, moe_w_down):
    depth = norm_mix.shape[0]
    for i in range(depth):
        j = i // 2
        if i % 2 == 0:
            x = _even_layer(x, norm_mix[i], ev_w_in[j], ev_gla_w_alpha[j], ev_gla_b_alpha[j], ev_gla_norm[j],
                            ev_rwkv_mu[j], ev_rwkv_w0[j], ev_rwkv_w2[j], ev_rwkv_a0[j], ev_rwkv_a2[j],
                            ev_rwkv_g2[j], ev_rwkv_k_k[j], ev_rwkv_k_a[j], ev_rwkv_r_k[j],
                            ev_rwkv_ln_w[j], ev_rwkv_ln_b[j], ev_w_out[j])
        else:
            x = _odd_layer(x, norm_mix[i], od_w_in[j], od_conv[j], od_a_log[j], od_dt_bias[j],
                           od_norm[j], od_w_out[j])
        x = _moe_layer(x, norm_ffn[i], moe_w_group[i], moe_b_group[i], moe_w_router[i], moe_b_router[i],
                       moe_w_gate, moe_w_up, moe_w_down, i, norm_final if i == depth - 1 else None)
    return x
```
